```python
import math
import jax, jax.numpy as jnp
from jax import lax
import numpy as np

D_MODEL = 2048
BATCH = 2
SEQ = 8192
DEPTH = 4

CHUNK = 64
HEAD_DIM = 128
RET_HEADS = 6
GLA_HEADS = 4
GLA_DK = 64
GLA_DV = 128
GLA_RANK = 16
GLA_TAU = 16.0
ATT_HEADS = 6
ATT_BAND_CHUNKS = 8
MAX_REL = 128
N_GROUPS = 4
EXPERTS_PER_GROUP = 8
TOP_K = 2
EXPERT_FF = 256
PLE_DIM = 256
EPS = 1e-6

RET_W = RET_HEADS * HEAD_DIM
GLA_QK = GLA_HEADS * GLA_DK
GLA_V = GLA_HEADS * GLA_DV
ATT_W = ATT_HEADS * HEAD_DIM
D_MIX = RET_W + GLA_V + ATT_W
SPLIT_SIZES = (RET_W, RET_W, RET_W, RET_W, GLA_QK, GLA_QK, GLA_V, GLA_V, GLA_RANK, ATT_W, ATT_W, ATT_W)
SPLIT_POINTS = tuple(sum(SPLIT_SIZES[:i + 1]) for i in range(len(SPLIT_SIZES) - 1))
D_IN = sum(SPLIT_SIZES)

kernel_name = "hybrid_ret_gla_chunkattn_hmoe_ple"

F32 = jnp.float32


def rms_norm(x, g):
    x32 = x.astype(F32)
    y = x32 * lax.rsqrt(jnp.mean(x32 * x32, axis=-1, keepdims=True) + EPS)
    return (y * g.astype(F32)).astype(x.dtype)


def head_group_norm(o):
    mu = jnp.mean(o, axis=-1, keepdims=True)
    var = jnp.mean(jnp.square(o - mu), axis=-1, keepdims=True)
    return (o - mu) * lax.rsqrt(var + EPS)


def head_rms_norm(o):
    return o * lax.rsqrt(jnp.mean(o * o, axis=-1, keepdims=True) + EPS)


def to_chunks(t):
    B, S, H, d = t.shape
    return t.reshape(B, S // CHUNK, CHUNK, H, d).transpose(0, 3, 1, 2, 4)


def from_chunks(t):
    B, H, N, C, d = t.shape
    return t.transpose(0, 2, 3, 1, 4).reshape(B, N * C, H, d)


def rotary(t, pos):
    half = t.shape[-1] // 2
    inv = 1.0 / (10000.0 ** (jnp.arange(half, dtype=F32) / half))
    ang = pos.astype(F32)[:, None] * inv[None, :]
    cos = jnp.cos(ang)[None, :, None, :]
    sin = jnp.sin(ang)[None, :, None, :]
    t32 = t.astype(F32)
    t1, t2 = t32[..., :half], t32[..., half:]
    return jnp.concatenate([t1 * cos - t2 * sin, t1 * sin + t2 * cos], axis=-1)


def retention(q, k, v, pos):
    H, d = q.shape[2], q.shape[3]
    qc = to_chunks(rotary(q, pos))
    kc = to_chunks(rotary(k, pos) * (d ** -0.5))
    vc = to_chunks(v.astype(F32))
    log_gamma = jnp.log(1.0 - 2.0 ** (-5.0 - jnp.arange(H, dtype=F32)))
    j = jnp.arange(CHUNK, dtype=F32)
    intra_decay = jnp.exp(log_gamma[:, None, None] * jnp.abs(j[:, None] - j[None, :]))
    zeta = jnp.exp(log_gamma[:, None] * (CHUNK - 1.0 - j[None, :]))
    xi = jnp.exp(log_gamma[:, None] * (j[None, :] + 1.0))
    chunk_decay = jnp.exp(log_gamma * CHUNK)

    scores = jnp.einsum('bhncd,bhnld->bhncl', qc, kc) * intra_decay[None, :, None]
    intra = jnp.einsum('bhncl,bhnle->bhnce', scores, vc)
    upd = jnp.einsum('bhnld,bhnle,hl->bhnde', kc, vc, zeta)

    def step(state, u):
        return chunk_decay[None, :, None, None] * state + u, state

    _, state_prev = lax.scan(step, jnp.zeros_like(upd[:, :, 0]), jnp.moveaxis(upd, 2, 0))
    state_prev = jnp.moveaxis(state_prev, 0, 2)
    cross = jnp.einsum('bhncd,bhnde->bhnce', qc, state_prev) * xi[None, :, None, :, None]
    return from_chunks(intra + cross)


def gla(q, k, v, log_alpha):
    dk = q.shape[-1]
    qc = to_chunks(q.astype(F32) * (dk ** -0.5))
    kc = to_chunks(k.astype(F32))
    vc = to_chunks(v.astype(F32))
    b = jnp.cumsum(to_chunks(log_alpha.astype(F32)), axis=3)
    b_last = b[:, :, :, -1:, :]
    q_in = qc * jnp.exp(b)
    k_in = kc * jnp.exp(-b)
    causal = jnp.tril(jnp.ones((CHUNK, CHUNK), dtype=bool))
    a = jnp.where(causal, jnp.einsum('bhncd,bhnld->bhncl', q_in, k_in), 0.0)
    intra = jnp.einsum('bhncl,bhnle->bhnce', a, vc)
    upd = jnp.einsum('bhnld,bhnle->bhnde', kc * jnp.exp(b_last - b), vc)
    g = jnp.exp(b_last[:, :, :, 0, :])

    def step(state, inp):
        gc, u = inp
        return gc[..., None] * state + u, state

    _, state_prev = lax.scan(step, jnp.zeros_like(upd[:, :, 0]),
                             (jnp.moveaxis(g, 2, 0), jnp.moveaxis(upd, 2, 0)))
    state_prev = jnp.moveaxis(state_prev, 0, 2)
    cross = jnp.einsum('bhncd,bhnde->bhnce', q_in, state_prev)
    return from_chunks(intra + cross)


def chunk_band_attention(q, k, v, rel_bias):
    B, S, H, d = q.shape
    N = S // CHUNK
    W = ATT_BAND_CHUNKS + 1
    qc = to_chunks(q) * (d ** -0.5)
    kc, vc = to_chunks(k), to_chunks(v)
    pad = ((0, 0), (0, 0), (ATT_BAND_CHUNKS, 0), (0, 0), (0, 0))
    kp, vp = jnp.pad(kc, pad), jnp.pad(vc, pad)
    k_band = jnp.stack([kp[:, :, o:o + N] for o in range(W)], axis=3).reshape(B, H, N, W * CHUNK, d)
    v_band = jnp.stack([vp[:, :, o:o + N] for o in range(W)], axis=3).reshape(B, H, N, W * CHUNK, d)
    s = jnp.einsum('bhncd,bhnkd->bhnck', qc, k_band).astype(F32)
    rel = (ATT_BAND_CHUNKS * CHUNK + jnp.arange(CHUNK))[:, None] - jnp.arange(W * CHUNK)[None, :]
    bias = rel_bias.astype(F32)[:, jnp.clip(rel, -MAX_REL, MAX_REL) + MAX_REL]
    valid = (jnp.arange(N)[:, None] + jnp.arange(W)[None, :] - ATT_BAND_CHUNKS) >= 0
    valid = jnp.repeat(valid, CHUNK, axis=1)
    s = jnp.where(valid[None, None, :, None, :], s + bias[None, :, None], -1e30)
    probs = jax.nn.softmax(s, axis=-1)
    out = jnp.einsum('bhnck,bhnkd->bhncd', probs, v_band.astype(F32))
    return from_chunks(out)


def mixing_block(x, g_mix, w_in, gla_w_alpha, gla_b_alpha, rel_bias, mix_scale, w_out, pos):
    B, S, _ = x.shape
    h = rms_norm(x, g_mix)
    z = h @ w_in
    rq, rk, rv, rg, gq, gk, gv, gg, ga, aq, ak, av = jnp.split(z, SPLIT_POINTS, axis=-1)
    heads = lambda t, n: t.reshape(B, S, n, -1)
    o_ret = head_group_norm(retention(heads(rq, RET_HEADS), heads(rk, RET_HEADS), heads(rv, RET_HEADS), pos))
    o_ret = o_ret.reshape(B, S, RET_W) * jax.nn.silu(rg.astype(F32))
    log_alpha = jax.nn.log_sigmoid((ga @ gla_w_alpha + gla_b_alpha).astype(F32)) / GLA_TAU
    o_gla = head_rms_norm(gla(heads(gq, GLA_HEADS), heads(gk, GLA_HEADS), heads(gv, GLA_HEADS),
                              log_alpha.reshape(B, S, GLA_HEADS, GLA_DK)))
    o_gla = o_gla.reshape(B, S, GLA_V) * jax.nn.silu(gg.astype(F32))
    o_att = head_rms_norm(chunk_band_attention(heads(aq, ATT_HEADS), heads(ak, ATT_HEADS),
                                               heads(av, ATT_HEADS), rel_bias)).reshape(B, S, ATT_W)
    y = jnp.concatenate([o_ret, o_gla, o_att], axis=-1) * mix_scale.astype(F32)
    return y.astype(x.dtype) @ w_out


def hierarchical_moe(h, w_rg, b_rg, w_re, b_re, w_gate, w_up, w_down):
    gp = jax.nn.softmax((h @ w_rg + b_rg).astype(F32), axis=-1)
    p_group, g_idx = lax.top_k(gp, 1)
    el = (jnp.einsum('bsd,dge->bsge', h, w_re) + b_re).astype(F32)
    el = jnp.take_along_axis(el, g_idx[..., None], axis=2)[:, :, 0]
    w_top, e_idx = lax.top_k(jax.nn.softmax(el, axis=-1), TOP_K)
    w_top = w_top / jnp.sum(w_top, axis=-1, keepdims=True)
    w_e = jnp.sum(jax.nn.one_hot(e_idx, EXPERTS_PER_GROUP, dtype=F32) * w_top[..., None], axis=2)
    comb = jax.nn.one_hot(g_idx[..., 0], N_GROUPS, dtype=F32)[..., None] * (p_group * w_e)[:, :, None, :]
    comb = comb.astype(h.dtype)
    y = jnp.zeros_like(h)
    for gi in range(N_GROUPS):
        a = jnp.einsum('bsd,edf->bsef', h, w_gate[gi])
        u = jnp.einsum('bsd,edf->bsef', h, w_up[gi])
        hid = jax.nn.silu(a) * u * comb[:, :, gi, :, None]
        y = y + jnp.einsum('bsef,efd->bsd', hid, w_down[gi])
    return y


def setup_inputs(seed: int = 0) -> dict:
    key = jax.random.key(seed)
    ks = jax.random.split(key, 21)
    nrm = lambda k, shape, scale: jax.random.normal(k, shape, F32) * scale
    gain = lambda k, shape: 1.0 + 0.02 * jax.random.normal(k, shape, F32)
    G, E, F = N_GROUPS, EXPERTS_PER_GROUP, EXPERT_FF
    return {
        "x": nrm(ks[0], (BATCH, SEQ, D_MODEL), 1.0),
        "p": nrm(ks[1], (DEPTH, BATCH, SEQ, PLE_DIM), 1.0),
        "g_mix": gain(ks[2], (DEPTH, D_MODEL)),
        "w_in": nrm(ks[3], (DEPTH, D_MODEL, D_IN), D_MODEL ** -0.5),
        "gla_w_alpha": nrm(ks[4], (DEPTH, GLA_RANK, GLA_QK), GLA_RANK ** -0.5),
        "gla_b_alpha": nrm(ks[5], (DEPTH, GLA_QK), 0.1),
        "rel_bias": nrm(ks[6], (DEPTH, ATT_HEADS, 2 * MAX_REL + 1), 0.2),
        "mix_scale": gain(ks[7], (DEPTH, D_MIX)),
        "w_out": nrm(ks[8], (DEPTH, D_MIX, D_MODEL), D_MIX ** -0.5),
        "g_ffn": gain(ks[9], (DEPTH, D_MODEL)),
        "w_router_group": nrm(ks[10], (DEPTH, D_MODEL, G), D_MODEL ** -0.5),
        "b_router_group": nrm(ks[11], (DEPTH, G), 0.01),
        "w_router_expert": nrm(ks[12], (DEPTH, D_MODEL, G, E), D_MODEL ** -0.5),
        "b_router_expert": nrm(ks[13], (DEPTH, G, E), 0.01),
        "w_expert_gate": nrm(ks[14], (DEPTH, G, E, D_MODEL, F), D_MODEL ** -0.5),
        "w_expert_up": nrm(ks[15], (DEPTH, G, E, D_MODEL, F), D_MODEL ** -0.5),
        "w_expert_down": nrm(ks[16], (DEPTH, G, E, F, D_MODEL), F ** -0.5),
        "g_ple": gain(ks[17], (DEPTH, D_MODEL)),
        "w_ple_gate": nrm(ks[18], (DEPTH, D_MODEL, D_MODEL), D_MODEL ** -0.5),
        "w_ple_proj": nrm(ks[19], (DEPTH, PLE_DIM, D_MODEL), PLE_DIM ** -0.5),
        "g_final": gain(ks[20], (D_MODEL,)),
    }


def reference(x, p, g_mix, w_in, gla_w_alpha, gla_b_alpha, rel_bias, mix_scale, w_out, g_ffn,
              w_router_group, b_router_group, w_router_expert, b_router_expert,
              w_expert_gate, w_expert_up, w_expert_down, g_ple, w_ple_gate, w_ple_proj, g_final):
    pos = jnp.arange(x.shape[1], dtype=jnp.int32)
    for i in range(DEPTH):
        x = x + mixing_block(x, g_mix[i], w_in[i], gla_w_alpha[i], gla_b_alpha[i], rel_bias[i],
                             mix_scale[i], w_out[i], pos)
        x = x + hierarchical_moe(rms_norm(x, g_ffn[i]), w_router_group[i], b_router_group[i],
                                 w_router_expert[i], b_router_expert[i],
                                 w_expert_gate[i], w_expert_up[i], w_expert_down[i])
        gate = jax.nn.sigmoid(rms_norm(x, g_ple[i]) @ w_ple_gate[i])
        x = x + gate * (p[i] @ w_ple_proj[i])
    return rms_norm(x, g_final)
```

```python
import functools
import math

import numpy as np
import jax
import jax.numpy as jnp
from jax import lax
from jax.experimental import pallas as pl
from jax.experimental.pallas import tpu as pltpu

F32 = jnp.float32
BF16 = jnp.bfloat16
U32 = jnp.uint32
I32 = jnp.int32

CHUNK = 64
HEAD_DIM = 128
RET_HEADS = 6
GLA_HEADS = 4
GLA_DK = 64
GLA_DV = 128
GLA_RANK = 16
GLA_TAU = 16.0
ATT_HEADS = 6
ATT_BAND_CHUNKS = 8
MAX_REL = 128
N_GROUPS = 4
EXPERTS_PER_GROUP = 8
N_EXPERTS = N_GROUPS * EXPERTS_PER_GROUP
EXPERT_FF = 256
EPS = 1e-6

RET_W = RET_HEADS * HEAD_DIM
GLA_QK = GLA_HEADS * GLA_DK
GLA_V = GLA_HEADS * GLA_DV
ATT_W = ATT_HEADS * HEAD_DIM

LANES = 128
V7X_VMEM_LIMIT = 56 * 1024 * 1024

Z_RET = 0
Z_GQ = 4 * RET_W
Z_GK = Z_GQ + GLA_QK
Z_GV = Z_GK + GLA_QK
Z_GG = Z_GV + GLA_V
Z_ATT = Z_GG + GLA_V
Z_GA = Z_ATT + 3 * ATT_W
Z_W = Z_GA + LANES

SUPER = 4 * CHUNK
ROUTE_LANE0 = N_GROUPS
MOE_TILE = 256


def _cparams(n_axes):
    return pltpu.CompilerParams(
        dimension_semantics=("arbitrary",) * n_axes,
        vmem_limit_bytes=V7X_VMEM_LIMIT,
    )


def _dot(a, b):
    return jnp.dot(a, b, preferred_element_type=F32)


def _dot_nt(a, b):
    return lax.dot_general(a, b, (((1,), (1,)), ((), ())), preferred_element_type=F32)


def _dot_tn(a, b):
    return lax.dot_general(a, b, (((0,), (0,)), ((), ())), preferred_element_type=F32)


def _rms(x, g):
    return x * lax.rsqrt(jnp.mean(x * x, axis=-1, keepdims=True) + EPS) * g


def _silu(x):
    return x / (1.0 + jnp.exp(-x))


def _pack_bf16_pair(lo, hi):
    lo_b = lax.bitcast_convert_type(lo.astype(BF16).astype(F32), U32)
    hi_b = lax.bitcast_convert_type(hi.astype(BF16).astype(F32), U32)
    return (lo_b >> 16) | (hi_b & jnp.uint32(0xFFFF0000))


def _unpack_bf16_pair(w):
    lo = lax.bitcast_convert_type(w << 16, F32)
    hi = lax.bitcast_convert_type(w & jnp.uint32(0xFFFF0000), F32)
    return lo, hi


def _rot_kernel(inv_ref, cos_ref, sin_ref):
    rows = cos_ref.shape[0]
    pos = (lax.broadcasted_iota(I32, (rows, LANES), 0) + pl.program_id(0) * rows).astype(F32)
    lane = lax.broadcasted_iota(I32, (rows, LANES), 1)
    ang = pos * inv_ref[...]
    cos_ref[...] = jnp.cos(ang)
    s = jnp.sin(ang)
    sin_ref[...] = jnp.where(lane < HEAD_DIM // 2, -s, s)


def _rotary_tables(seq):
    half = HEAD_DIM // 2
    inv = np.float32(1.0) / (np.float32(10000.0) ** (np.arange(half, dtype=np.float32) / np.float32(half)))
    inv2 = jnp.asarray(np.concatenate([inv, inv]).reshape(1, LANES).astype(np.float32))
    rows = min(seq, 1024)
    return pl.pallas_call(
        _rot_kernel,
        grid=(seq // rows,),
        in_specs=[pl.BlockSpec((1, LANES), lambda i: (0, 0))],
        out_specs=[pl.BlockSpec((rows, LANES), lambda i: (i, 0))] * 2,
        out_shape=[jax.ShapeDtypeStruct((seq, LANES), F32)] * 2,
        compiler_params=_cparams(1),
        name="rotary_tables",
    )(inv2)


def _in_kernel(x_ref, g_ref, w_ref, z_ref, h_ref):
    @pl.when(pl.program_id(1) == 0)
    def _():
        h_ref[...] = _rms(x_ref[...], g_ref[...]).astype(BF16)

    z_ref[...] = _dot(h_ref[...], w_ref[...]).astype(BF16)


def _in_proj(x, g, w):
    t, d = x.shape
    n = w.shape[1]
    tm = min(t, 1024)
    tn = 640
    return pl.pallas_call(
        _in_kernel,
        grid=(t // tm, n // tn),
        in_specs=[
            pl.BlockSpec((tm, d), lambda i, j: (i, 0)),
            pl.BlockSpec((1, d), lambda i, j: (0, 0)),
            pl.BlockSpec((d, tn), lambda i, j: (0, j)),
        ],
        out_specs=pl.BlockSpec((tm, tn), lambda i, j: (i, j)),
        out_shape=jax.ShapeDtypeStruct((t, n), BF16),
        scratch_shapes=[pltpu.VMEM((tm, d), BF16)],
        compiler_params=_cparams(2),
        name="in_proj",
    )(x, g.reshape(1, d), w)


def _ret_log_gamma(h):
    return math.log(1.0 - 2.0 ** (-5.0 - h))


def _ret_kernel(q_ref, k_ref, v_ref, g_ref, cos_ref, sin_ref, ms_ref, o_ref,
                state_ref, dmat_ref, xi_ref, zeta_ref):
    s = pl.program_id(1)

    @pl.when(s == 0)
    def _():
        state_ref[...] = jnp.zeros_like(state_ref)
        row = lax.broadcasted_iota(I32, (SUPER, SUPER), 0)
        col = lax.broadcasted_iota(I32, (SUPER, SUPER), 1)
        dist = jnp.abs(row - col).astype(F32)
        keep = (col <= row) | ((row // CHUNK) == (col // CHUNK))
        t = lax.broadcasted_iota(I32, (SUPER, HEAD_DIM), 0).astype(F32)
        for h in range(RET_HEADS):
            lg = _ret_log_gamma(h)
            dmat_ref[h] = jnp.where(keep, jnp.exp(lg * dist), 0.0)
            xi_ref[h] = jnp.exp(lg * (t + 1.0))
            zeta_ref[h] = jnp.exp(lg * (SUPER - 1.0 - t))

    cos = cos_ref[...]
    sin = sin_ref[...]
    scale = HEAD_DIM ** -0.5
    for h in range(RET_HEADS):
        sl = slice(h * HEAD_DIM, (h + 1) * HEAD_DIM)
        q = q_ref[:, sl].astype(F32)
        k = k_ref[:, sl].astype(F32)
        vb = v_ref[:, sl]
        qr = q * cos + pltpu.roll(q, HEAD_DIM // 2, 1) * sin
        kr = (k * cos + pltpu.roll(k, HEAD_DIM // 2, 1) * sin) * scale
        qb = qr.astype(BF16)
        kb = kr.astype(BF16)
        sc = _dot_nt(qb, kb) * dmat_ref[h]
        intra = _dot(sc.astype(BF16), vb)
        st = state_ref[h]
        cross = _dot(qb, st.astype(BF16)) * xi_ref[h]
        kz = (kr * zeta_ref[h]).astype(BF16)
        upd = _dot_tn(kz, vb)
        state_ref[h] = math.exp(_ret_log_gamma(h) * SUPER) * st + upd
        o = intra + cross
        mu = jnp.mean(o, axis=-1, keepdims=True)
        oc = o - mu
        var = jnp.mean(oc * oc, axis=-1, keepdims=True)
        on = oc * lax.rsqrt(var + EPS)
        gate = _silu(g_ref[:, sl].astype(F32))
        o_ref[:, sl] = (on * gate * ms_ref[:, sl]).astype(BF16)


def _retention(z, cos2, sin2, ms, batch, seq):
    t = z.shape[0]
    ns = seq // SUPER
    zspec = lambda c: pl.BlockSpec((SUPER, RET_W), lambda b, s, c=c: (b * ns + s, c))
    return pl.pallas_call(
        _ret_kernel,
        grid=(batch, ns),
        in_specs=[
            zspec(0), zspec(1), zspec(2), zspec(3),
            pl.BlockSpec((SUPER, LANES), lambda b, s: (s, 0)),
            pl.BlockSpec((SUPER, LANES), lambda b, s: (s, 0)),
            pl.BlockSpec((1, RET_W), lambda b, s: (0, 0)),
        ],
        out_specs=pl.BlockSpec((SUPER, RET_W), lambda b, s: (b * ns + s, 0)),
        out_shape=jax.ShapeDtypeStruct((t, RET_W), BF16),
        scratch_shapes=[
            pltpu.VMEM((RET_HEADS, HEAD_DIM, HEAD_DIM), F32),
            pltpu.VMEM((RET_HEADS, SUPER, SUPER), F32),
            pltpu.VMEM((RET_HEADS, SUPER, HEAD_DIM), F32),
            pltpu.VMEM((RET_HEADS, SUPER, HEAD_DIM), F32),
        ],
        compiler_params=_cparams(2),
        name="retention",
    )(z, z, z, z, cos2, sin2, ms.reshape(1, RET_W))


def _gla_kernel(q_ref, k_ref, v_ref, g_ref, a_ref, wa_ref, ba_ref, ms_ref, o_ref, state_ref):
    s = pl.program_id(1)

    @pl.when(s == 0)
    def _():
        state_ref[...] = jnp.zeros_like(state_ref)

    pre = _dot(a_ref[...], wa_ref[...]) + ba_ref[...]
    la = (jnp.minimum(pre, 0.0) - jnp.log(1.0 + jnp.exp(-jnp.abs(pre)))) * (1.0 / GLA_TAU)
    r = lax.broadcasted_iota(I32, (CHUNK, CHUNK), 0)
    c = lax.broadcasted_iota(I32, (CHUNK, CHUNK), 1)
    causal = c <= r
    tril = jnp.where(causal, 1.0, 0.0).astype(BF16)
    ones = jnp.ones((CHUNK, GLA_DV), BF16)
    scale = GLA_DK ** -0.5
    for ci in range(SUPER // CHUNK):
        rows = slice(ci * CHUNK, (ci + 1) * CHUNK)
        la_c = la[rows, :]
        hi = la_c.astype(BF16)
        lo = (la_c - hi.astype(F32)).astype(BF16)
        bcs = _dot(tril, hi) + _dot(tril, lo)
        gl = _dot_tn(hi, ones) + _dot_tn(lo, ones)
        blast = bcs[CHUNK - 1:CHUNK, :]
        eb = jnp.exp(bcs)
        enb = jnp.exp(-bcs)
        ekb = jnp.exp(blast - bcs)
        qc = q_ref[rows, :].astype(F32) * scale * eb
        kc = k_ref[rows, :].astype(F32)
        kin = (kc * enb).astype(BF16)
        kup = (kc * ekb).astype(BF16)
        qin = qc.astype(BF16)
        for h in range(GLA_HEADS):
            ks = slice(h * GLA_DK, (h + 1) * GLA_DK)
            vs = slice(h * GLA_DV, (h + 1) * GLA_DV)
            vb = v_ref[rows, vs]
            a = jnp.where(causal, _dot_nt(qin[:, ks], kin[:, ks]), 0.0)
            intra = _dot(a.astype(BF16), vb)
            st = state_ref[h]
            cross = _dot(qin[:, ks], st.astype(BF16))
            upd = _dot_tn(kup[:, ks], vb)
            state_ref[h] = jnp.exp(gl[ks, :]) * st + upd
            o = intra + cross
            on = o * lax.rsqrt(jnp.mean(o * o, axis=-1, keepdims=True) + EPS)
            gate = _silu(g_ref[rows, vs].astype(F32))
            o_ref[rows, vs] = (on * gate * ms_ref[:, vs]).astype(BF16)


def _gla(z, wa, ba, ms, batch, seq):
    t = z.shape[0]
    ns = seq // SUPER
    row = lambda b, s: b * ns + s
    return pl.pallas_call(
        _gla_kernel,
        grid=(batch, ns),
        in_specs=[
            pl.BlockSpec((SUPER, GLA_QK), lambda b, s: (row(b, s), Z_GQ // GLA_QK)),
            pl.BlockSpec((SUPER, GLA_QK), lambda b, s: (row(b, s), Z_GK // GLA_QK)),
            pl.BlockSpec((SUPER, GLA_V), lambda b, s: (row(b, s), Z_GV // GLA_V)),
            pl.BlockSpec((SUPER, GLA_V), lambda b, s: (row(b, s), Z_GG // GLA_V)),
            pl.BlockSpec((SUPER, LANES), lambda b, s: (row(b, s), Z_GA // LANES)),
            pl.BlockSpec((LANES, GLA_QK), lambda b, s: (0, 0)),
            pl.BlockSpec((1, GLA_QK), lambda b, s: (0, 0)),
            pl.BlockSpec((1, GLA_V), lambda b, s: (0, 0)),
        ],
        out_specs=pl.BlockSpec((SUPER, GLA_V), lambda b, s: (row(b, s), 0)),
        out_shape=jax.ShapeDtypeStruct((t, GLA_V), BF16),
        scratch_shapes=[pltpu.VMEM((GLA_HEADS, GLA_DK, GLA_DV), F32)],
        compiler_params=_cparams(2),
        name="gla",
    )(z, z, z, z, z, wa, ba.reshape(1, GLA_QK), ms.reshape(1, GLA_V))


BAND = 3 * SUPER
REL_PAD = 384


def _bias_kernel(rb_ref, o_ref):
    h = pl.program_id(0)
    row = lax.broadcasted_iota(I32, (SUPER, BAND), 0)
    col = lax.broadcasted_iota(I32, (SUPER, BAND), 1)
    idx = jnp.clip(2 * SUPER + row - col, -MAX_REL, MAX_REL) + MAX_REL

    def body(j, acc):
        return jnp.where(idx == j, rb_ref[h, j], acc)

    o_ref[0] = lax.fori_loop(0, 2 * MAX_REL + 1, body, jnp.zeros((SUPER, BAND), F32))


def _att_bias(rel_bias):
    return pl.pallas_call(
        _bias_kernel,
        grid=(ATT_HEADS,),
        in_specs=[pl.BlockSpec(memory_space=pltpu.SMEM)],
        out_specs=pl.BlockSpec((1, SUPER, BAND), lambda h: (h, 0, 0)),
        out_shape=jax.ShapeDtypeStruct((ATT_HEADS, SUPER, BAND), F32),
        compiler_params=_cparams(1),
        name="att_bias",
    )(rel_bias)


def _att_kernel(q_ref, k0_ref, k1_ref, k2_ref, v0_ref, v1_ref, v2_ref, bias_ref, ms_ref, o_ref):
    s = pl.program_id(1)
    rc = lax.broadcasted_iota(I32, (SUPER, SUPER), 0) // CHUNK
    cc = lax.broadcasted_iota(I32, (SUPER, SUPER), 1) // CHUNK
    masks = (cc >= rc, None, cc <= rc)
    valid = (s >= 2, s >= 1, None)
    k_refs = (k0_ref, k1_ref, k2_ref)
    v_refs = (v0_ref, v1_ref, v2_ref)
    scale = HEAD_DIM ** -0.5
    neg = jnp.float32(-1e30)
    for h in range(ATT_HEADS):
        sl = slice(h * HEAD_DIM, (h + 1) * HEAD_DIM)
        qb = (q_ref[:, sl].astype(F32) * scale).astype(BF16)
        sc = []
        for j in range(3):
            sj = _dot_nt(qb, k_refs[j][:, sl]) + bias_ref[h, :, j * SUPER:(j + 1) * SUPER]
            if masks[j] is not None:
                sj = jnp.where(masks[j], sj, neg)
            if valid[j] is not None:
                sj = jnp.where(valid[j], sj, neg)
            sc.append(sj)
        m = jnp.maximum(jnp.maximum(jnp.max(sc[0], axis=-1, keepdims=True),
                                    jnp.max(sc[1], axis=-1, keepdims=True)),
                        jnp.max(sc[2], axis=-1, keepdims=True))
        acc = jnp.zeros((SUPER, HEAD_DIM), F32)
        den = jnp.zeros((SUPER, 1), F32)
        for j in range(3):
            pj = jnp.exp(sc[j] - m)
            den = den + jnp.sum(pj, axis=-1, keepdims=True)
            acc = acc + _dot(pj.astype(BF16), v_refs[j][:, sl])
        o = acc / den
        on = o * lax.rsqrt(jnp.mean(o * o, axis=-1, keepdims=True) + EPS)
        o_ref[:, sl] = (on * ms_ref[:, sl]).astype(BF16)


def _band_attention(z, bias, ms, batch, seq):
    t = z.shape[0]
    ns = seq // SUPER
    qc, kc, vc = Z_ATT // ATT_W, Z_ATT // ATT_W + 1, Z_ATT // ATT_W + 2

    def kv_spec(col, back):
        return pl.BlockSpec((SUPER, ATT_W), lambda b, s: (b * ns + jnp.maximum(s - back, 0), col))

    return pl.pallas_call(
        _att_kernel,
        grid=(batch, ns),
        in_specs=[
            pl.BlockSpec((SUPER, ATT_W), lambda b, s: (b * ns + s, qc)),
            kv_spec(kc, 2), kv_spec(kc, 1), kv_spec(kc, 0),
            kv_spec(vc, 2), kv_spec(vc, 1), kv_spec(vc, 0),
            pl.BlockSpec((ATT_HEADS, SUPER, BAND), lambda b, s: (0, 0, 0)),
            pl.BlockSpec((1, ATT_W), lambda b, s: (0, 0)),
        ],
        out_specs=pl.BlockSpec((SUPER, ATT_W), lambda b, s: (b * ns + s, 0)),
        out_shape=jax.ShapeDtypeStruct((t, ATT_W), BF16),
        compiler_params=_cparams(2),
        name="band_attention",
    )(z, z, z, z, z, z, z, bias, ms.reshape(1, ATT_W))


def _out_kernel(x_ref, a_ref, b_ref, c_ref, wa_ref, wb_ref, wc_ref, o_ref):
    o_ref[...] = (x_ref[...] + _dot(a_ref[...], wa_ref[...]) + _dot(b_ref[...], wb_ref[...])
                  + _dot(c_ref[...], wc_ref[...]))


def _out_proj(x, o_ret, o_gla, o_att, w_out):
    t, d = x.shape
    tm = min(t, 512)
    wa, wb, wc = w_out[:RET_W], w_out[RET_W:RET_W + GLA_V], w_out[RET_W + GLA_V:]
    row = lambda w: pl.BlockSpec((tm, w), lambda i: (i, 0))
    full = lambda w: pl.BlockSpec((w, d), lambda i: (0, 0))
    return pl.pallas_call(
        _out_kernel,
        grid=(t // tm,),
        in_specs=[row(d), row(RET_W), row(GLA_V), row(ATT_W), full(RET_W), full(GLA_V), full(ATT_W)],
        out_specs=row(d),
        out_shape=jax.ShapeDtypeStruct((t, d), F32),
        compiler_params=_cparams(1),
        name="out_proj",
    )(x, o_ret, o_gla, o_att, wa, wb, wc)


def _route_kernel(x_ref, g_ref, wr_ref, br_ref, hp_ref, meta_ref, cnt_ref, run_ref):
    i = pl.program_id(0)
    tm = x_ref.shape[0]
    half = x_ref.shape[1] // 2

    @pl.when(i == 0)
    def _():
        run_ref[...] = jnp.zeros_like(run_ref)

    h = _rms(x_ref[...], g_ref[...])
    hp_ref[...] = _pack_bf16_pair(h[:, :half], h[:, half:])
    lg = _dot(h.astype(BF16), wr_ref[...]) + br_ref[...]
    lane = lax.broadcasted_iota(I32, (tm, LANES), 1)
    neg = jnp.float32(-1e30)
    big = jnp.int32(LANES)
    gl = jnp.where(lane < N_GROUPS, lg, neg)
    gm = jnp.max(gl, axis=-1, keepdims=True)
    p_group = 1.0 / jnp.sum(jnp.exp(gl - gm), axis=-1, keepdims=True)
    g_idx = jnp.min(jnp.where(gl == gm, lane, big), axis=-1, keepdims=True)
    lo = ROUTE_LANE0 + EXPERTS_PER_GROUP * g_idx
    el = jnp.where((lane >= lo) & (lane < lo + EXPERTS_PER_GROUP), lg, neg)
    m1 = jnp.max(el, axis=-1, keepdims=True)
    i1 = jnp.min(jnp.where(el == m1, lane, big), axis=-1, keepdims=True)
    el2 = jnp.where(lane == i1, neg, el)
    m2 = jnp.max(el2, axis=-1, keepdims=True)
    i2 = jnp.min(jnp.where(el2 == m2, lane, big), axis=-1, keepdims=True)
    e2 = jnp.exp(m2 - m1)
    c1 = p_group / (1.0 + e2)
    c2 = p_group * e2 / (1.0 + e2)
    oh1 = jnp.where(lane == i1, 1.0, 0.0)
    oh2 = jnp.where(lane == i2, 1.0, 0.0)
    oh = oh1 + oh2
    r = lax.broadcasted_iota(I32, (tm, tm), 0)
    c = lax.broadcasted_iota(I32, (tm, tm), 1)
    stril = jnp.where(c < r, 1.0, 0.0).astype(BF16)
    before = _dot(stril, oh.astype(BF16)) + run_ref[0:1, :]
    rank1 = jnp.sum(before * oh1, axis=-1, keepdims=True)
    rank2 = jnp.sum(before * oh2, axis=-1, keepdims=True)
    run_ref[0:1, :] = run_ref[0:1, :] + jnp.sum(oh, axis=0, keepdims=True)
    cnt_ref[...] = run_ref[...]
    e1f = (i1 - ROUTE_LANE0).astype(F32)
    e2f = (i2 - ROUTE_LANE0).astype(F32)
    meta = jnp.where(lane == 0, e1f, 0.0)
    meta = jnp.where(lane == 1, e2f, meta)
    meta = jnp.where(lane == 2, rank1, meta)
    meta = jnp.where(lane == 3, rank2, meta)
    meta = jnp.where(lane == 4, c1, meta)
    meta = jnp.where(lane == 5, c2, meta)
    meta_ref[...] = meta


def _route(x, g, wr, br):
    t, d = x.shape
    tm = min(t, 512)
    return pl.pallas_call(
        _route_kernel,
        grid=(t // tm,),
        in_specs=[
            pl.BlockSpec((tm, d), lambda i: (i, 0)),
            pl.BlockSpec((1, d), lambda i: (0, 0)),
            pl.BlockSpec((d, LANES), lambda i: (0, 0)),
            pl.BlockSpec((1, LANES), lambda i: (0, 0)),
        ],
        out_specs=[
            pl.BlockSpec((tm, d // 2), lambda i: (i, 0)),
            pl.BlockSpec((tm, LANES), lambda i: (i, 0)),
            pl.BlockSpec((8, LANES), lambda i: (0, 0)),
        ],
        out_shape=[
            jax.ShapeDtypeStruct((t, d // 2), U32),
            jax.ShapeDtypeStruct((t, LANES), F32),
            jax.ShapeDtypeStruct((8, LANES), F32),
        ],
        scratch_shapes=[pltpu.VMEM((8, LANES), F32)],
        compiler_params=_cparams(1),
        name="router",
    )(x, g.reshape(1, d), wr, br)


def _row_copy(src_ref, row, dst_ref, slot, r, sem_ref):
    return pltpu.make_async_copy(src_ref.at[pl.ds(row, 1), :], dst_ref.at[slot, pl.ds(r, 1), :],
                                 sem_ref.at[slot])


def _gather_rows(idx_ref, n_rows, src_ref, dst_ref, slot, sem_ref, start):
    def body(r, carry):
        cp = _row_copy(src_ref, idx_ref[0, 0, r], dst_ref, slot, r, sem_ref)
        if start:
            cp.start()
        else:
            cp.wait()
        return carry

    lax.fori_loop(0, n_rows, body, 0, unroll=8)


def _moe_kernel(te_ref, nu_ref, first_ref, nxt_ref, cur_ref, h_ref, wg_ref, wu_ref, wd_ref, y_ref,
                buf_ref, sem_ref, wgb_ref, wub_ref, wdb_ref):
    i = pl.program_id(0)
    n_used = nu_ref[0]
    slot = i % 2
    tm = buf_ref.shape[1]
    half = buf_ref.shape[2]

    @pl.when(i == 0)
    def _():
        _gather_rows(first_ref, tm, h_ref, buf_ref, 0, sem_ref, True)

    @pl.when(i + 1 < n_used)
    def _():
        _gather_rows(nxt_ref, tm, h_ref, buf_ref, 1 - slot, sem_ref, True)

    e = te_ref[i]
    e_prev = te_ref[jnp.maximum(i - 1, 0)]

    @pl.when((i == 0) | (e != e_prev))
    def _():
        wgb_ref[...] = wg_ref[...].astype(BF16)
        wub_ref[...] = wu_ref[...].astype(BF16)
        wdb_ref[...] = wd_ref[...].astype(BF16)

    @pl.when(i < n_used)
    def _():
        _gather_rows(cur_ref, tm, h_ref, buf_ref, slot, sem_ref, False)
        lo, hi = _unpack_bf16_pair(buf_ref[slot])
        xl = lo.astype(BF16)
        xh = hi.astype(BF16)
        a = _dot(xl, wgb_ref[:half, :]) + _dot(xh, wgb_ref[half:, :])
        u = _dot(xl, wub_ref[:half, :]) + _dot(xh, wub_ref[half:, :])
        hid = (_silu(a) * u).astype(BF16)
        y = _dot(hid, wdb_ref[...])
        y_ref[...] = _pack_bf16_pair(y[:, :half], y[:, half:])

    @pl.when(i >= n_used)
    def _():
        y_ref[...] = jnp.zeros_like(y_ref)


def _moe(hp, row_token, tile_expert, n_used, wg, wu, wd, layer):
    t, half = hp.shape
    d = 2 * half
    nt = row_token.shape[0]
    tm = row_token.shape[2]
    ff = wg.shape[-1]

    def w_spec(shape):
        return pl.BlockSpec((None, None, None) + shape,
                            lambda i, te, nu: (layer, te[i] // EXPERTS_PER_GROUP, te[i] % EXPERTS_PER_GROUP, 0, 0))

    idx_spec = lambda f: pl.BlockSpec((1, 1, tm), f, memory_space=pltpu.SMEM)
    grid_spec = pltpu.PrefetchScalarGridSpec(
        num_scalar_prefetch=2,
        grid=(nt,),
        in_specs=[
            idx_spec(lambda i, te, nu: (0, 0, 0)),
            idx_spec(lambda i, te, nu: (jnp.minimum(i + 1, nt - 1), 0, 0)),
            idx_spec(lambda i, te, nu: (i, 0, 0)),
            pl.BlockSpec(memory_space=pl.ANY),
            w_spec((d, ff)), w_spec((d, ff)), w_spec((ff, d)),
        ],
        out_specs=pl.BlockSpec((tm, half), lambda i, te, nu: (i, 0)),
        scratch_shapes=[
            pltpu.VMEM((2, tm, half), U32),
            pltpu.SemaphoreType.DMA((2,)),
            pltpu.VMEM((d, ff), BF16),
            pltpu.VMEM((d, ff), BF16),
            pltpu.VMEM((ff, d), BF16),
        ],
    )
    return pl.pallas_call(
        _moe_kernel,
        grid_spec=grid_spec,
        out_shape=jax.ShapeDtypeStruct((nt * tm, half), U32),
        compiler_params=_cparams(1),
        name="expert_mlp",
    )(tile_expert, n_used, row_token, row_token, row_token, hp, wg, wu, wd)


def _ple_kernel(first_ref, nxt_ref, cur_ref, x_ref, meta_ref, y_ref, p_ref, g_ref, wg_ref, wp_ref,
                gf_ref, o_ref, buf_ref, sem_ref, *, final):
    i = pl.program_id(0)
    n = pl.num_programs(0)
    slot = i % 2
    tm = x_ref.shape[0]
    half = x_ref.shape[1] // 2

    @pl.when(i == 0)
    def _():
        _gather_rows(first_ref, 2 * tm, y_ref, buf_ref, 0, sem_ref, True)

    @pl.when(i + 1 < n)
    def _():
        _gather_rows(nxt_ref, 2 * tm, y_ref, buf_ref, 1 - slot, sem_ref, True)

    _gather_rows(cur_ref, 2 * tm, y_ref, buf_ref, slot, sem_ref, False)
    meta = meta_ref[...]
    c1 = meta[:, 4:5]
    c2 = meta[:, 5:6]
    lo1, hi1 = _unpack_bf16_pair(buf_ref[slot, :tm, :])
    lo2, hi2 = _unpack_bf16_pair(buf_ref[slot, tm:, :])
    xl = x_ref[:, :half] + c1 * lo1 + c2 * lo2
    xh = x_ref[:, half:] + c1 * hi1 + c2 * hi2
    ms = (jnp.sum(xl * xl, axis=-1, keepdims=True) + jnp.sum(xh * xh, axis=-1, keepdims=True)) / (2 * half)
    inv = lax.rsqrt(ms + EPS)
    hl = (xl * inv * g_ref[:, :half]).astype(BF16)
    hh = (xh * inv * g_ref[:, half:]).astype(BF16)
    gate = _dot(hl, wg_ref[:half, :]) + _dot(hh, wg_ref[half:, :])
    gate = 1.0 / (1.0 + jnp.exp(-gate))
    pp = _dot(p_ref[...].astype(BF16), wp_ref[...])
    ol = xl + gate[:, :half] * pp[:, :half]
    oh = xh + gate[:, half:] * pp[:, half:]
    if final:
        ms2 = (jnp.sum(ol * ol, axis=-1, keepdims=True) + jnp.sum(oh * oh, axis=-1, keepdims=True)) / (2 * half)
        inv2 = lax.rsqrt(ms2 + EPS)
        ol = ol * inv2 * gf_ref[:, :half]
        oh = oh * inv2 * gf_ref[:, half:]
    o_ref[:, :half] = ol
    o_ref[:, half:] = oh


def _combine_ple(x, meta, pos_tiles, ys, p, g_ple, w_gate, w_proj, g_final, final):
    t, d = x.shape
    nt = pos_tiles.shape[0]
    tm = pos_tiles.shape[2] // 2
    pd = p.shape[1]
    idx_spec = lambda f: pl.BlockSpec((1, 1, 2 * tm), f, memory_space=pltpu.SMEM)
    const = lambda shape: pl.BlockSpec(shape, lambda i: (0, 0))
    return pl.pallas_call(
        functools.partial(_ple_kernel, final=final),
        grid=(nt,),
        in_specs=[
            idx_spec(lambda i: (0, 0, 0)),
            idx_spec(lambda i: (jnp.minimum(i + 1, nt - 1), 0, 0)),
            idx_spec(lambda i: (i, 0, 0)),
            pl.BlockSpec((tm, d), lambda i: (i, 0)),
            pl.BlockSpec((tm, LANES), lambda i: (i, 0)),
            pl.BlockSpec(memory_space=pl.ANY),
            pl.BlockSpec((tm, pd), lambda i: (i, 0)),
            const((1, d)), const((d, d)), const((pd, d)), const((1, d)),
        ],
        out_specs=pl.BlockSpec((tm, d), lambda i: (i, 0)),
        out_shape=jax.ShapeDtypeStruct((t, d), F32),
        scratch_shapes=[pltpu.VMEM((2, 2 * tm, d // 2), U32), pltpu.SemaphoreType.DMA((2,))],
        compiler_params=_cparams(1),
        name="combine_ple",
    )(pos_tiles, pos_tiles, pos_tiles, x, meta, ys, p, g_ple.reshape(1, d), w_gate, w_proj,
      g_final.reshape(1, d))


def _prep_w_in(w):
    d = w.shape[0]
    ga0 = 4 * RET_W + 2 * GLA_QK + 2 * GLA_V
    pad = jnp.zeros((d, LANES - GLA_RANK), w.dtype)
    return jnp.concatenate([w[:, :ga0], w[:, ga0 + GLA_RANK:], w[:, ga0:ga0 + GLA_RANK], pad],
                           axis=1).astype(BF16)


def _routing_tables(meta, counts, n_tok, tile):
    eid = meta[:, 0:2].astype(I32)
    rank = meta[:, 2:4].astype(I32)
    cnt = counts[0, ROUTE_LANE0:ROUTE_LANE0 + N_EXPERTS].astype(I32)
    padded = ((cnt + tile - 1) // tile) * tile
    ends = jnp.cumsum(padded)
    offs = ends - padded
    pos = offs[eid] + rank
    n_rows = 2 * n_tok + N_EXPERTS * tile
    nt = n_rows // tile
    tile_expert = jnp.minimum(
        jnp.searchsorted(ends, jnp.arange(nt, dtype=I32) * tile, side="right"), N_EXPERTS - 1).astype(I32)
    n_used = (ends[-1] // tile).astype(I32).reshape(1)
    tok = jnp.broadcast_to(jnp.arange(n_tok, dtype=I32)[:, None], (n_tok, 2))
    row_token = jnp.zeros((n_rows,), I32).at[pos.reshape(-1)].set(tok.reshape(-1), unique_indices=True)
    return pos, row_token.reshape(nt, 1, tile), tile_expert, n_used


def kernel(x, p, g_mix, w_in, gla_w_alpha, gla_b_alpha, rel_bias, mix_scale, w_out, g_ffn,
           w_router_group, b_router_group, w_router_expert, b_router_expert,
           w_expert_gate, w_expert_up, w_expert_down, g_ple, w_ple_gate, w_ple_proj, g_final):
    batch, seq, d = x.shape
    depth = g_mix.shape[0]
    t = batch * seq
    assert seq % SUPER == 0 and d % (2 * LANES) == 0
    xf = x.reshape(t, d)
    cos2, sin2 = _rotary_tables(seq)
    ple_tile = min(t, 256)
    for i in range(depth):
        z = _in_proj(xf, g_mix[i], _prep_w_in(w_in[i]))
        ms = mix_scale[i]
        o_ret = _retention(z, cos2, sin2, ms[:RET_W], batch, seq)
        wa = jnp.zeros((LANES, GLA_QK), F32).at[:GLA_RANK].set(gla_w_alpha[i]).astype(BF16)
        o_gla = _gla(z, wa, gla_b_alpha[i], ms[RET_W:RET_W + GLA_V], batch, seq)
        o_att = _band_attention(z, _att_bias(rel_bias[i]), ms[RET_W + GLA_V:], batch, seq)
        x1 = _out_proj(xf, o_ret, o_gla, o_att, w_out[i].astype(BF16))

        wr = jnp.zeros((d, LANES), F32)
        wr = wr.at[:, :N_GROUPS].set(w_router_group[i])
        wr = wr.at[:, ROUTE_LANE0:ROUTE_LANE0 + N_EXPERTS].set(w_router_expert[i].reshape(d, N_EXPERTS))
        br = jnp.zeros((1, LANES), F32)
        br = br.at[0, :N_GROUPS].set(b_router_group[i])
        br = br.at[0, ROUTE_LANE0:ROUTE_LANE0 + N_EXPERTS].set(b_router_expert[i].reshape(N_EXPERTS))
        hp, meta, counts = _route(x1, g_ffn[i], wr.astype(BF16), br)
        pos, row_token, tile_expert, n_used = _routing_tables(meta, counts, t, MOE_TILE)
        ys = _moe(hp, row_token, tile_expert, n_used, w_expert_gate, w_expert_up, w_expert_down, i)
        pos_tiles = pos.reshape(t // ple_tile, ple_tile, 2).transpose(0, 2, 1).reshape(t // ple_tile, 1, 2 * ple_tile)
        xf = _combine_ple(x1, meta, pos_tiles, ys, p[i].reshape(t, -1), g_ple[i],
                          w_ple_gate[i].astype(BF16), w_ple_proj[i].astype(BF16), g_final,
                          final=(i == depth - 1))
    return xf.reshape(batch, seq, d)
```

```python
import functools
import math

import numpy as np
import jax
import jax.numpy as jnp
from jax import lax
from jax.experimental import pallas as pl
from jax.experimental.pallas import tpu as pltpu

F32 = jnp.float32
BF16 = jnp.bfloat16
U32 = jnp.uint32
I32 = jnp.int32

CHUNK = 64
HEAD_DIM = 128
RET_HEADS = 6
GLA_HEADS = 4
GLA_DK = 64
GLA_DV = 128
GLA_RANK = 16
GLA_TAU = 16.0
ATT_HEADS = 6
ATT_BAND_CHUNKS = 8
MAX_REL = 128
N_GROUPS = 4
EXPERTS_PER_GROUP = 8
N_EXPERTS = N_GROUPS * EXPERTS_PER_GROUP
EXPERT_FF = 256
EPS = 1e-6

RET_W = RET_HEADS * HEAD_DIM
GLA_QK = GLA_HEADS * GLA_DK
GLA_V = GLA_HEADS * GLA_DV
ATT_W = ATT_HEADS * HEAD_DIM

LANES = 128
V7X_VMEM_LIMIT = 56 * 1024 * 1024

Z_RET = 0
Z_GQ = 4 * RET_W
Z_GK = Z_GQ + GLA_QK
Z_GV = Z_GK + GLA_QK
Z_GG = Z_GV + GLA_V
Z_ATT = Z_GG + GLA_V
Z_GA = Z_ATT + 3 * ATT_W
Z_W = 7 * 1024

SUPER = 4 * CHUNK
ROUTE_LANE0 = N_GROUPS
MOE_TILE = 256
TOK_TILE = 512
PLE_TILE = 256
ROW_SUB = 8


def _cparams(n_axes):
    return pltpu.CompilerParams(
        dimension_semantics=("arbitrary",) * n_axes,
        vmem_limit_bytes=V7X_VMEM_LIMIT,
    )


def _dot(a, b):
    return jnp.dot(a, b, preferred_element_type=F32)


def _dot_nt(a, b):
    return lax.dot_general(a, b, (((1,), (1,)), ((), ())), preferred_element_type=F32)


def _dot_tn(a, b):
    return lax.dot_general(a, b, (((0,), (0,)), ((), ())), preferred_element_type=F32)


def _rms(x, g):
    return x * lax.rsqrt(jnp.mean(x * x, axis=-1, keepdims=True) + EPS) * g


def _silu(x):
    return x / (1.0 + jnp.exp(-x))


def _pack_bf16_pair(lo, hi):
    lo_b = lax.bitcast_convert_type(lo.astype(BF16).astype(F32), U32)
    hi_b = lax.bitcast_convert_type(hi.astype(BF16).astype(F32), U32)
    return (lo_b >> 16) | (hi_b & jnp.uint32(0xFFFF0000))


def _unpack_bf16_pair(w):
    lo = lax.bitcast_convert_type(w << 16, F32)
    hi = lax.bitcast_convert_type(w & jnp.uint32(0xFFFF0000), F32)
    return lo, hi


def _rot_kernel(inv_ref, cos_ref, sin_ref):
    rows = cos_ref.shape[0]
    pos = (lax.broadcasted_iota(I32, (rows, LANES), 0) + pl.program_id(0) * rows).astype(F32)
    lane = lax.broadcasted_iota(I32, (rows, LANES), 1)
    ang = pos * inv_ref[...]
    cos_ref[...] = jnp.cos(ang)
    s = jnp.sin(ang)
    sin_ref[...] = jnp.where(lane < HEAD_DIM // 2, -s, s)


def _rotary_tables(seq):
    half = HEAD_DIM // 2
    inv = np.float32(1.0) / (np.float32(10000.0) ** (np.arange(half, dtype=np.float32) / np.float32(half)))
    inv2 = jnp.asarray(np.concatenate([inv, inv]).reshape(1, LANES).astype(np.float32))
    rows = min(seq, 1024)
    return pl.pallas_call(
        _rot_kernel,
        grid=(seq // rows,),
        in_specs=[pl.BlockSpec((1, LANES), lambda i: (0, 0))],
        out_specs=[pl.BlockSpec((rows, LANES), lambda i: (i, 0))] * 2,
        out_shape=[jax.ShapeDtypeStruct((seq, LANES), F32)] * 2,
        compiler_params=_cparams(1),
        name="rotary_tables",
    )(inv2)


def _in_kernel(x_ref, g_ref, w_ref, z_ref, h_ref):
    @pl.when(pl.program_id(1) == 0)
    def _():
        h_ref[...] = _rms(x_ref[...], g_ref[...]).astype(BF16)

    z_ref[...] = _dot(h_ref[...], w_ref[...]).astype(BF16)


def _in_proj(x, g, w):
    t, d = x.shape
    n = w.shape[1]
    tm = min(t, 1024)
    tn = min(n, 1024)
    return pl.pallas_call(
        _in_kernel,
        grid=(t // tm, n // tn),
        in_specs=[
            pl.BlockSpec((tm, d), lambda i, j: (i, 0)),
            pl.BlockSpec((1, d), lambda i, j: (0, 0)),
            pl.BlockSpec((d, tn), lambda i, j: (0, j)),
        ],
        out_specs=pl.BlockSpec((tm, tn), lambda i, j: (i, j)),
        out_shape=jax.ShapeDtypeStruct((t, n), BF16),
        scratch_shapes=[pltpu.VMEM((tm, d), BF16)],
        compiler_params=_cparams(2),
        name="in_proj",
    )(x, g.reshape(1, d), w)


def _ret_log_gamma(h):
    return math.log(1.0 - 2.0 ** (-5.0 - h))


def _ret_kernel(q_ref, k_ref, v_ref, g_ref, cos_ref, sin_ref, ms_ref, o_ref,
                state_ref, dmat_ref, xi_ref, zeta_ref):
    s = pl.program_id(1)

    @pl.when(s == 0)
    def _():
        state_ref[...] = jnp.zeros_like(state_ref)
        row = lax.broadcasted_iota(I32, (SUPER, SUPER), 0)
        col = lax.broadcasted_iota(I32, (SUPER, SUPER), 1)
        dist = jnp.abs(row - col).astype(F32)
        keep = (col <= row) | ((row // CHUNK) == (col // CHUNK))
        t = lax.broadcasted_iota(I32, (SUPER, HEAD_DIM), 0).astype(F32)
        for h in range(RET_HEADS):
            lg = _ret_log_gamma(h)
            dmat_ref[h] = jnp.where(keep, jnp.exp(lg * dist), 0.0)
            xi_ref[h] = jnp.exp(lg * (t + 1.0))
            zeta_ref[h] = jnp.exp(lg * (SUPER - 1.0 - t))

    cos = cos_ref[...]
    sin = sin_ref[...]
    scale = HEAD_DIM ** -0.5
    for h in range(RET_HEADS):
        sl = slice(h * HEAD_DIM, (h + 1) * HEAD_DIM)
        q = q_ref[:, sl].astype(F32)
        k = k_ref[:, sl].astype(F32)
        vb = v_ref[:, sl]
        qr = q * cos + pltpu.roll(q, HEAD_DIM // 2, 1) * sin
        kr = (k * cos + pltpu.roll(k, HEAD_DIM // 2, 1) * sin) * scale
        qb = qr.astype(BF16)
        kb = kr.astype(BF16)
        sc = _dot_nt(qb, kb) * dmat_ref[h]
        intra = _dot(sc.astype(BF16), vb)
        st = state_ref[h]
        cross = _dot(qb, st.astype(BF16)) * xi_ref[h]
        kz = (kr * zeta_ref[h]).astype(BF16)
        upd = _dot_tn(kz, vb)
        state_ref[h] = math.exp(_ret_log_gamma(h) * SUPER) * st + upd
        o = intra + cross
        mu = jnp.mean(o, axis=-1, keepdims=True)
        oc = o - mu
        var = jnp.mean(oc * oc, axis=-1, keepdims=True)
        on = oc * lax.rsqrt(var + EPS)
        gate = _silu(g_ref[:, sl].astype(F32))
        o_ref[:, sl] = (on * gate * ms_ref[:, sl]).astype(BF16)


def _retention(z, cos2, sin2, ms, batch, seq):
    t = z.shape[0]
    ns = seq // SUPER
    zspec = lambda c: pl.BlockSpec((SUPER, RET_W), lambda b, s, c=c: (b * ns + s, c))
    return pl.pallas_call(
        _ret_kernel,
        grid=(batch, ns),
        in_specs=[
            zspec(0), zspec(1), zspec(2), zspec(3),
            pl.BlockSpec((SUPER, LANES), lambda b, s: (s, 0)),
            pl.BlockSpec((SUPER, LANES), lambda b, s: (s, 0)),
            pl.BlockSpec((1, RET_W), lambda b, s: (0, 0)),
        ],
        out_specs=pl.BlockSpec((SUPER, RET_W), lambda b, s: (b * ns + s, 0)),
        out_shape=jax.ShapeDtypeStruct((t, RET_W), BF16),
        scratch_shapes=[
            pltpu.VMEM((RET_HEADS, HEAD_DIM, HEAD_DIM), F32),
            pltpu.VMEM((RET_HEADS, SUPER, SUPER), F32),
            pltpu.VMEM((RET_HEADS, SUPER, HEAD_DIM), F32),
            pltpu.VMEM((RET_HEADS, SUPER, HEAD_DIM), F32),
        ],
        compiler_params=_cparams(2),
        name="retention",
    )(z, z, z, z, cos2, sin2, ms.reshape(1, RET_W))


def _gla_kernel(q_ref, k_ref, v_ref, g_ref, a_ref, wa_ref, ba_ref, ms_ref, o_ref, state_ref):
    s = pl.program_id(1)

    @pl.when(s == 0)
    def _():
        state_ref[...] = jnp.zeros_like(state_ref)

    pre = _dot(a_ref[...], wa_ref[...]) + ba_ref[...]
    la = (jnp.minimum(pre, 0.0) - jnp.log(1.0 + jnp.exp(-jnp.abs(pre)))) * (1.0 / GLA_TAU)
    r = lax.broadcasted_iota(I32, (CHUNK, CHUNK), 0)
    c = lax.broadcasted_iota(I32, (CHUNK, CHUNK), 1)
    causal = c <= r
    tril = jnp.where(causal, 1.0, 0.0).astype(BF16)
    ones = jnp.ones((CHUNK, GLA_DV), BF16)
    scale = GLA_DK ** -0.5
    for ci in range(SUPER // CHUNK):
        rows = slice(ci * CHUNK, (ci + 1) * CHUNK)
        la_c = la[rows, :]
        hi = la_c.astype(BF16)
        lo = (la_c - hi.astype(F32)).astype(BF16)
        bcs = _dot(tril, hi) + _dot(tril, lo)
        gl = _dot_tn(hi, ones) + _dot_tn(lo, ones)
        blast = bcs[CHUNK - 1:CHUNK, :]
        eb = jnp.exp(bcs)
        enb = jnp.exp(-bcs)
        ekb = jnp.exp(blast - bcs)
        qc = q_ref[rows, :].astype(F32) * scale * eb
        kc = k_ref[rows, :].astype(F32)
        kin = (kc * enb).astype(BF16)
        kup = (kc * ekb).astype(BF16)
        qin = qc.astype(BF16)
        for h in range(GLA_HEADS):
            ks = slice(h * GLA_DK, (h + 1) * GLA_DK)
            vs = slice(h * GLA_DV, (h + 1) * GLA_DV)
            vb = v_ref[rows, vs]
            a = jnp.where(causal, _dot_nt(qin[:, ks], kin[:, ks]), 0.0)
            intra = _dot(a.astype(BF16), vb)
            st = state_ref[h]
            cross = _dot(qin[:, ks], st.astype(BF16))
            upd = _dot_tn(kup[:, ks], vb)
            state_ref[h] = jnp.exp(gl[ks, :]) * st + upd
            o = intra + cross
            on = o * lax.rsqrt(jnp.mean(o * o, axis=-1, keepdims=True) + EPS)
            gate = _silu(g_ref[rows, vs].astype(F32))
            o_ref[rows, vs] = (on * gate * ms_ref[:, vs]).astype(BF16)


def _gla(z, wa, ba, ms, batch, seq):
    t = z.shape[0]
    ns = seq // SUPER
    row = lambda b, s: b * ns + s
    return pl.pallas_call(
        _gla_kernel,
        grid=(batch, ns),
        in_specs=[
            pl.BlockSpec((SUPER, GLA_QK), lambda b, s: (row(b, s), Z_GQ // GLA_QK)),
            pl.BlockSpec((SUPER, GLA_QK), lambda b, s: (row(b, s), Z_GK // GLA_QK)),
            pl.BlockSpec((SUPER, GLA_V), lambda b, s: (row(b, s), Z_GV // GLA_V)),
            pl.BlockSpec((SUPER, GLA_V), lambda b, s: (row(b, s), Z_GG // GLA_V)),
            pl.BlockSpec((SUPER, LANES), lambda b, s: (row(b, s), Z_GA // LANES)),
            pl.BlockSpec((LANES, GLA_QK), lambda b, s: (0, 0)),
            pl.BlockSpec((1, GLA_QK), lambda b, s: (0, 0)),
            pl.BlockSpec((1, GLA_V), lambda b, s: (0, 0)),
        ],
        out_specs=pl.BlockSpec((SUPER, GLA_V), lambda b, s: (row(b, s), 0)),
        out_shape=jax.ShapeDtypeStruct((t, GLA_V), BF16),
        scratch_shapes=[pltpu.VMEM((GLA_HEADS, GLA_DK, GLA_DV), F32)],
        compiler_params=_cparams(2),
        name="gla",
    )(z, z, z, z, z, wa, ba.reshape(1, GLA_QK), ms.reshape(1, GLA_V))


BAND = 3 * SUPER


def _bias_kernel(rb_ref, o_ref):
    h = pl.program_id(0)
    row = lax.broadcasted_iota(I32, (SUPER, BAND), 0)
    col = lax.broadcasted_iota(I32, (SUPER, BAND), 1)
    idx = jnp.clip(2 * SUPER + row - col, -MAX_REL, MAX_REL) + MAX_REL

    def body(j, acc):
        return jnp.where(idx == j, rb_ref[h, j], acc)

    o_ref[0] = lax.fori_loop(0, 2 * MAX_REL + 1, body, jnp.zeros((SUPER, BAND), F32))


def _att_bias(rel_bias):
    return pl.pallas_call(
        _bias_kernel,
        grid=(ATT_HEADS,),
        in_specs=[pl.BlockSpec(memory_space=pltpu.SMEM)],
        out_specs=pl.BlockSpec((1, SUPER, BAND), lambda h: (h, 0, 0)),
        out_shape=jax.ShapeDtypeStruct((ATT_HEADS, SUPER, BAND), F32),
        compiler_params=_cparams(1),
        name="att_bias",
    )(rel_bias)


def _att_kernel(q_ref, k0_ref, k1_ref, k2_ref, v0_ref, v1_ref, v2_ref, bias_ref, ms_ref, o_ref):
    s = pl.program_id(1)
    rc = lax.broadcasted_iota(I32, (SUPER, SUPER), 0) // CHUNK
    cc = lax.broadcasted_iota(I32, (SUPER, SUPER), 1) // CHUNK
    masks = (cc >= rc, None, cc <= rc)
    valid = (s >= 2, s >= 1, None)
    k_refs = (k0_ref, k1_ref, k2_ref)
    v_refs = (v0_ref, v1_ref, v2_ref)
    scale = HEAD_DIM ** -0.5
    neg = jnp.float32(-1e30)
    for h in range(ATT_HEADS):
        sl = slice(h * HEAD_DIM, (h + 1) * HEAD_DIM)
        qb = (q_ref[:, sl].astype(F32) * scale).astype(BF16)
        sc = []
        for j in range(3):
            sj = _dot_nt(qb, k_refs[j][:, sl]) + bias_ref[h, :, j * SUPER:(j + 1) * SUPER]
            if masks[j] is not None:
                sj = jnp.where(masks[j], sj, neg)
            if valid[j] is not None:
                sj = jnp.where(valid[j], sj, neg)
            sc.append(sj)
        m = jnp.maximum(jnp.maximum(jnp.max(sc[0], axis=-1, keepdims=True),
                                    jnp.max(sc[1], axis=-1, keepdims=True)),
                        jnp.max(sc[2], axis=-1, keepdims=True))
        acc = jnp.zeros((SUPER, HEAD_DIM), F32)
        den = jnp.zeros((SUPER, 1), F32)
        for j in range(3):
            pj = jnp.exp(sc[j] - m)
            den = den + jnp.sum(pj, axis=-1, keepdims=True)
            acc = acc + _dot(pj.astype(BF16), v_refs[j][:, sl])
        o = acc / den
        on = o * lax.rsqrt(jnp.mean(o * o, axis=-1, keepdims=True) + EPS)
        o_ref[:, sl] = (on * ms_ref[:, sl]).astype(BF16)


def _band_attention(z, bias, ms, batch, seq):
    t = z.shape[0]
    ns = seq // SUPER
    qc, kc, vc = Z_ATT // ATT_W, Z_ATT // ATT_W + 1, Z_ATT // ATT_W + 2

    def kv_spec(col, back):
        return pl.BlockSpec((SUPER, ATT_W), lambda b, s: (b * ns + jnp.maximum(s - back, 0), col))

    return pl.pallas_call(
        _att_kernel,
        grid=(batch, ns),
        in_specs=[
            pl.BlockSpec((SUPER, ATT_W), lambda b, s: (b * ns + s, qc)),
            kv_spec(kc, 2), kv_spec(kc, 1), kv_spec(kc, 0),
            kv_spec(vc, 2), kv_spec(vc, 1), kv_spec(vc, 0),
            pl.BlockSpec((ATT_HEADS, SUPER, BAND), lambda b, s: (0, 0, 0)),
            pl.BlockSpec((1, ATT_W), lambda b, s: (0, 0)),
        ],
        out_specs=pl.BlockSpec((SUPER, ATT_W), lambda b, s: (b * ns + s, 0)),
        out_shape=jax.ShapeDtypeStruct((t, ATT_W), BF16),
        compiler_params=_cparams(2),
        name="band_attention",
    )(z, z, z, z, z, z, z, bias, ms.reshape(1, ATT_W))


def _out_kernel(x_ref, a_ref, b_ref, c_ref, wa_ref, wb_ref, wc_ref, o_ref):
    o_ref[...] = (x_ref[...] + _dot(a_ref[...], wa_ref[...]) + _dot(b_ref[...], wb_ref[...])
                  + _dot(c_ref[...], wc_ref[...]))


def _out_proj(x, o_ret, o_gla, o_att, w_out):
    t, d = x.shape
    tm = min(t, 512)
    wa, wb, wc = w_out[:RET_W], w_out[RET_W:RET_W + GLA_V], w_out[RET_W + GLA_V:]
    row = lambda w: pl.BlockSpec((tm, w), lambda i: (i, 0))
    full = lambda w: pl.BlockSpec((w, d), lambda i: (0, 0))
    return pl.pallas_call(
        _out_kernel,
        grid=(t // tm,),
        in_specs=[row(d), row(RET_W), row(GLA_V), row(ATT_W), full(RET_W), full(GLA_V), full(ATT_W)],
        out_specs=row(d),
        out_shape=jax.ShapeDtypeStruct((t, d), F32),
        compiler_params=_cparams(1),
        name="out_proj",
    )(x, o_ret, o_gla, o_att, wa, wb, wc)


def _store_token_tiles(ref, packed):
    tm = packed.shape[0]
    for s in range(ROW_SUB):
        ref[pl.ds(s, tm, stride=ROW_SUB), :] = packed[:, s * LANES:(s + 1) * LANES]


def _load_token_tiles(ref, tm):
    return jnp.concatenate([ref[pl.ds(s, tm, stride=ROW_SUB), :] for s in range(ROW_SUB)], axis=1)


def _tile_rows(ref, row, n_rows=1):
    start = row * ROW_SUB
    if not isinstance(start, int):
        start = pl.multiple_of(start, ROW_SUB)
    return ref.at[pl.ds(start, n_rows * ROW_SUB), :]


def _route_kernel(x_ref, g_ref, wr_ref, br_ref, hp_ref, meta_ref, metat_ref, cnt_ref, run_ref):
    i = pl.program_id(0)
    tm = x_ref.shape[0]
    half = x_ref.shape[1] // 2

    @pl.when(i == 0)
    def _():
        run_ref[...] = jnp.zeros_like(run_ref)

    h = _rms(x_ref[...], g_ref[...])
    _store_token_tiles(hp_ref, _pack_bf16_pair(h[:, :half], h[:, half:]))
    lg = _dot(h.astype(BF16), wr_ref[...]) + br_ref[...]
    lane = lax.broadcasted_iota(I32, (tm, LANES), 1)
    neg = jnp.float32(-1e30)
    big = jnp.int32(LANES)
    gl = jnp.where(lane < N_GROUPS, lg, neg)
    gm = jnp.max(gl, axis=-1, keepdims=True)
    p_group = 1.0 / jnp.sum(jnp.exp(gl - gm), axis=-1, keepdims=True)
    g_idx = jnp.min(jnp.where(gl == gm, lane, big), axis=-1, keepdims=True)
    lo = ROUTE_LANE0 + EXPERTS_PER_GROUP * g_idx
    el = jnp.where((lane >= lo) & (lane < lo + EXPERTS_PER_GROUP), lg, neg)
    m1 = jnp.max(el, axis=-1, keepdims=True)
    i1 = jnp.min(jnp.where(el == m1, lane, big), axis=-1, keepdims=True)
    el2 = jnp.where(lane == i1, neg, el)
    m2 = jnp.max(el2, axis=-1, keepdims=True)
    i2 = jnp.min(jnp.where(el2 == m2, lane, big), axis=-1, keepdims=True)
    e2 = jnp.exp(m2 - m1)
    c1 = p_group / (1.0 + e2)
    c2 = p_group * e2 / (1.0 + e2)
    oh1 = jnp.where(lane == i1, 1.0, 0.0)
    oh2 = jnp.where(lane == i2, 1.0, 0.0)
    oh = oh1 + oh2
    r = lax.broadcasted_iota(I32, (tm, tm), 0)
    c = lax.broadcasted_iota(I32, (tm, tm), 1)
    stril = jnp.where(c < r, 1.0, 0.0).astype(BF16)
    before = _dot(stril, oh.astype(BF16)) + run_ref[0:1, :]
    rank1 = jnp.sum(before * oh1, axis=-1, keepdims=True)
    rank2 = jnp.sum(before * oh2, axis=-1, keepdims=True)
    run_ref[0:1, :] = run_ref[0:1, :] + jnp.sum(oh, axis=0, keepdims=True)
    cnt_ref[...] = run_ref[...]
    e1f = (i1 - ROUTE_LANE0).astype(F32)
    e2f = (i2 - ROUTE_LANE0).astype(F32)
    meta = jnp.where(lane == 0, e1f, 0.0)
    meta = jnp.where(lane == 1, e2f, meta)
    meta = jnp.where(lane == 2, rank1, meta)
    meta = jnp.where(lane == 3, rank2, meta)
    meta = jnp.where(lane == 4, c1, meta)
    meta = jnp.where(lane == 5, c2, meta)
    meta_ref[...] = meta
    metat_ref[0] = meta.T[0:8, :]


def _route(x, g, wr, br):
    t, d = x.shape
    tm = min(t, TOK_TILE)
    nt = t // tm
    return pl.pallas_call(
        _route_kernel,
        grid=(nt,),
        in_specs=[
            pl.BlockSpec((tm, d), lambda i: (i, 0)),
            pl.BlockSpec((1, d), lambda i: (0, 0)),
            pl.BlockSpec((d, LANES), lambda i: (0, 0)),
            pl.BlockSpec((1, LANES), lambda i: (0, 0)),
        ],
        out_specs=[
            pl.BlockSpec((tm * ROW_SUB, LANES), lambda i: (i, 0)),
            pl.BlockSpec((tm, LANES), lambda i: (i, 0)),
            pl.BlockSpec((1, 8, tm), lambda i: (i, 0, 0)),
            pl.BlockSpec((8, LANES), lambda i: (0, 0)),
        ],
        out_shape=[
            jax.ShapeDtypeStruct((t * ROW_SUB, LANES), U32),
            jax.ShapeDtypeStruct((t, LANES), F32),
            jax.ShapeDtypeStruct((nt, 8, tm), F32),
            jax.ShapeDtypeStruct((8, LANES), F32),
        ],
        scratch_shapes=[pltpu.VMEM((8, LANES), F32)],
        compiler_params=_cparams(1),
        name="router",
    )(x, g.reshape(1, d), wr, br)


_PAD_BITS = tuple(reversed(range(MOE_TILE.bit_length() - 1)))


def _dispatch_kernel(ps_ref, pn_ref, nu_ref, pos_ref, hp_ref, xs_ref, zero_ref, sem_ref, zsem_ref):
    i = pl.program_id(0)
    tm = hp_ref.shape[0] // ROW_SUB
    zrows = zero_ref.shape[0] // ROW_SUB
    n_tiles = xs_ref.shape[0] // (ROW_SUB * MOE_TILE)

    def pad_copies(start):
        def zero_copy(off, k):
            cp = pltpu.make_async_copy(_tile_rows(zero_ref, 0, k), _tile_rows(xs_ref, off, k), zsem_ref.at[0])
            if start:
                cp.start()
            else:
                cp.wait()

        def body(e, carry):
            n = pn_ref[e]
            base = ps_ref[e]
            for bit in _PAD_BITS:
                k = 1 << bit

                @pl.when((n & k) != 0)
                def _():
                    zero_copy(base + ((n >> (bit + 1)) << (bit + 1)), k)
            return carry

        lax.fori_loop(0, N_EXPERTS, body, 0)

        def tail(tile, carry):
            for part in range(MOE_TILE // zrows):
                zero_copy(tile * MOE_TILE + part * zrows, zrows)
            return carry

        lax.fori_loop(nu_ref[0], n_tiles, tail, 0)

    @pl.when(i == 0)
    def _():
        zero_ref[...] = jnp.zeros_like(zero_ref)
        pad_copies(True)
        pad_copies(False)

    def row_copy(j):
        return pltpu.make_async_copy(_tile_rows(hp_ref, j % tm), _tile_rows(xs_ref, pos_ref[0, 0, j]),
                                     sem_ref.at[0])

    for j in range(2 * tm):
        row_copy(j).start()
    for j in range(2 * tm):
        row_copy(j).wait()


def _dispatch(hp, pos_tiles, pad_start, pad_n, n_used, n_rows):
    nt = pos_tiles.shape[0]
    tm = pos_tiles.shape[2] // 2
    grid_spec = pltpu.PrefetchScalarGridSpec(
        num_scalar_prefetch=3,
        grid=(nt,),
        in_specs=[
            pl.BlockSpec((1, 1, 2 * tm), lambda i, ps, pn, nu: (i, 0, 0), memory_space=pltpu.SMEM),
            pl.BlockSpec((tm * ROW_SUB, LANES), lambda i, ps, pn, nu: (i, 0)),
        ],
        out_specs=pl.BlockSpec(memory_space=pl.ANY),
        scratch_shapes=[
            pltpu.VMEM((MOE_TILE // 2 * ROW_SUB, LANES), U32),
            pltpu.SemaphoreType.DMA((1,)),
            pltpu.SemaphoreType.DMA((1,)),
        ],
    )
    return pl.pallas_call(
        _dispatch_kernel,
        grid_spec=grid_spec,
        out_shape=jax.ShapeDtypeStruct((n_rows * ROW_SUB, LANES), U32),
        compiler_params=_cparams(1),
        name="dispatch",
    )(pad_start, pad_n, n_used, pos_tiles, hp)


def _moe_kernel(te_ref, nu_ref, xs_ref, wg_ref, wu_ref, wd_ref, y_ref, wgb_ref, wub_ref, wdb_ref):
    i = pl.program_id(0)
    n_used = nu_ref[0]
    tm = xs_ref.shape[0] // ROW_SUB
    half = ROW_SUB * LANES
    e = te_ref[i]
    e_prev = te_ref[jnp.maximum(i - 1, 0)]

    @pl.when((i == 0) | (e != e_prev))
    def _():
        wgb_ref[...] = wg_ref[...].astype(BF16)
        wub_ref[...] = wu_ref[...].astype(BF16)
        wdb_ref[...] = wd_ref[...].astype(BF16)

    @pl.when(i < n_used)
    def _():
        lo, hi = _unpack_bf16_pair(_load_token_tiles(xs_ref, tm))
        xl = lo.astype(BF16)
        xh = hi.astype(BF16)
        a = _dot(xl, wgb_ref[:half, :]) + _dot(xh, wgb_ref[half:, :])
        u = _dot(xl, wub_ref[:half, :]) + _dot(xh, wub_ref[half:, :])
        hid = (_silu(a) * u).astype(BF16)
        y = _dot(hid, wdb_ref[...])
        _store_token_tiles(y_ref, _pack_bf16_pair(y[:, :half], y[:, half:]))

    @pl.when(i >= n_used)
    def _():
        y_ref[...] = jnp.zeros_like(y_ref)


def _moe(xs, tile_expert, n_used, wg, wu, wd, layer):
    nt = tile_expert.shape[0]
    tm = xs.shape[0] // ROW_SUB // nt
    d, ff = wg.shape[-2], wg.shape[-1]

    def w_spec(shape):
        return pl.BlockSpec((None, None, None) + shape,
                            lambda i, te, nu: (layer, te[i] // EXPERTS_PER_GROUP, te[i] % EXPERTS_PER_GROUP, 0, 0))

    grid_spec = pltpu.PrefetchScalarGridSpec(
        num_scalar_prefetch=2,
        grid=(nt,),
        in_specs=[
            pl.BlockSpec((tm * ROW_SUB, LANES), lambda i, te, nu: (jnp.minimum(i, nu[0] - 1), 0)),
            w_spec((d, ff)), w_spec((d, ff)), w_spec((ff, d)),
        ],
        out_specs=pl.BlockSpec((tm * ROW_SUB, LANES), lambda i, te, nu: (i, 0)),
        scratch_shapes=[
            pltpu.VMEM((d, ff), BF16),
            pltpu.VMEM((d, ff), BF16),
            pltpu.VMEM((ff, d), BF16),
        ],
    )
    return pl.pallas_call(
        _moe_kernel,
        grid_spec=grid_spec,
        out_shape=jax.ShapeDtypeStruct(xs.shape, U32),
        compiler_params=_cparams(1),
        name="expert_mlp",
    )(tile_expert, n_used, xs, wg, wu, wd)


def _ple_kernel(pos_ref, x_ref, meta_ref, y_ref, p_ref, g_ref, wg_ref, wp_ref, gf_ref, o_ref,
                buf_ref, sem_ref, *, final):
    i = pl.program_id(0)
    n = pl.num_programs(0) - 1
    tm = x_ref.shape[0]
    half = x_ref.shape[1] // 2

    @pl.when(i < n)
    def _():
        slot = i % 2
        for j in range(2 * tm):
            pltpu.make_async_copy(_tile_rows(y_ref, pos_ref[0, 0, j]), _tile_rows(buf_ref.at[slot], j),
                                  sem_ref.at[slot]).start()

    @pl.when(i > 0)
    def _():
        slot = (i + 1) % 2
        for j in range(2 * tm):
            pltpu.make_async_copy(_tile_rows(y_ref, 0), _tile_rows(buf_ref.at[slot], j), sem_ref.at[slot]).wait()
        rows = _load_token_tiles(buf_ref.at[slot], 2 * tm)
        meta = meta_ref[...]
        c1 = meta[:, 4:5]
        c2 = meta[:, 5:6]
        lo1, hi1 = _unpack_bf16_pair(rows[:tm])
        lo2, hi2 = _unpack_bf16_pair(rows[tm:])
        xl = x_ref[:, :half] + c1 * lo1 + c2 * lo2
        xh = x_ref[:, half:] + c1 * hi1 + c2 * hi2
        ms = (jnp.sum(xl * xl, axis=-1, keepdims=True) + jnp.sum(xh * xh, axis=-1, keepdims=True)) / (2 * half)
        inv = lax.rsqrt(ms + EPS)
        hl = (xl * inv * g_ref[:, :half]).astype(BF16)
        hh = (xh * inv * g_ref[:, half:]).astype(BF16)
        gate = _dot(hl, wg_ref[:half, :]) + _dot(hh, wg_ref[half:, :])
        gate = 1.0 / (1.0 + jnp.exp(-gate))
        pp = _dot(p_ref[...].astype(BF16), wp_ref[...])
        ol = xl + gate[:, :half] * pp[:, :half]
        oh = xh + gate[:, half:] * pp[:, half:]
        if final:
            ms2 = (jnp.sum(ol * ol, axis=-1, keepdims=True) + jnp.sum(oh * oh, axis=-1, keepdims=True)) / (2 * half)
            inv2 = lax.rsqrt(ms2 + EPS)
            ol = ol * inv2 * gf_ref[:, :half]
            oh = oh * inv2 * gf_ref[:, half:]
        o_ref[:, :half] = ol
        o_ref[:, half:] = oh


def _combine_ple(x, meta, pos_tiles, ys, p, g_ple, w_gate, w_proj, g_final, final):
    t, d = x.shape
    nt = pos_tiles.shape[0]
    tm = pos_tiles.shape[2] // 2
    pd = p.shape[1]
    const = lambda shape: pl.BlockSpec(shape, lambda i: (0, 0))
    prev = lambda w: pl.BlockSpec((tm, w), lambda i: (jnp.maximum(i - 1, 0), 0))
    return pl.pallas_call(
        functools.partial(_ple_kernel, final=final),
        grid=(nt + 1,),
        in_specs=[
            pl.BlockSpec((1, 1, 2 * tm), lambda i: (jnp.minimum(i, nt - 1), 0, 0), memory_space=pltpu.SMEM),
            prev(d), prev(LANES),
            pl.BlockSpec(memory_space=pl.ANY),
            prev(pd),
            const((1, d)), const((d, d)), const((pd, d)), const((1, d)),
        ],
        out_specs=prev(d),
        out_shape=jax.ShapeDtypeStruct((t, d), F32),
        scratch_shapes=[pltpu.VMEM((2, 2 * tm * ROW_SUB, LANES), U32), pltpu.SemaphoreType.DMA((2,))],
        compiler_params=_cparams(1),
        name="combine_ple",
    )(pos_tiles, x, meta, ys, p, g_ple.reshape(1, d), w_gate, w_proj, g_final.reshape(1, d))


def _prep_w_in(w):
    d = w.shape[0]
    ga0 = 4 * RET_W + 2 * GLA_QK + 2 * GLA_V
    pad = jnp.zeros((d, Z_W - Z_GA - GLA_RANK), w.dtype)
    return jnp.concatenate([w[:, :ga0], w[:, ga0 + GLA_RANK:], w[:, ga0:ga0 + GLA_RANK], pad],
                           axis=1).astype(BF16)


def _routing_tables(meta_t, counts, n_tok, tile):
    eid = meta_t[:, 0:2, :].astype(I32)
    rank = meta_t[:, 2:4, :].astype(I32)
    cnt = counts[0, ROUTE_LANE0:ROUTE_LANE0 + N_EXPERTS].astype(I32)
    padded = ((cnt + tile - 1) // tile) * tile
    ends = jnp.cumsum(padded)
    offs = ends - padded
    base = jnp.zeros_like(eid)
    for e in range(N_EXPERTS):
        base = jnp.where(eid == e, offs[e], base)
    pos = base + rank
    n_rows = 2 * n_tok + N_EXPERTS * tile
    nt = n_rows // tile
    tile_start = jnp.arange(nt, dtype=I32) * tile
    tile_expert = jnp.minimum(jnp.sum((tile_start[:, None] >= ends[None, :]).astype(I32), axis=1),
                              N_EXPERTS - 1)
    n_used = (ends[-1] // tile).reshape(1)
    return pos, tile_expert, n_used, offs + cnt, padded - cnt, n_rows


def kernel(x, p, g_mix, w_in, gla_w_alpha, gla_b_alpha, rel_bias, mix_scale, w_out, g_ffn,
           w_router_group, b_router_group, w_router_expert, b_router_expert,
           w_expert_gate, w_expert_up, w_expert_down, g_ple, w_ple_gate, w_ple_proj, g_final):
    batch, seq, d = x.shape
    depth = g_mix.shape[0]
    t = batch * seq
    assert seq % SUPER == 0 and d == 2 * ROW_SUB * LANES
    tok_tile = min(t, TOK_TILE)
    ple_tile = min(t, PLE_TILE)
    xf = x.reshape(t, d)
    cos2, sin2 = _rotary_tables(seq)
    for i in range(depth):
        z = _in_proj(xf, g_mix[i], _prep_w_in(w_in[i]))
        ms = mix_scale[i]
        o_ret = _retention(z, cos2, sin2, ms[:RET_W], batch, seq)
        wa = jnp.zeros((LANES, GLA_QK), F32).at[:GLA_RANK].set(gla_w_alpha[i]).astype(BF16)
        o_gla = _gla(z, wa, gla_b_alpha[i], ms[RET_W:RET_W + GLA_V], batch, seq)
        o_att = _band_attention(z, _att_bias(rel_bias[i]), ms[RET_W + GLA_V:], batch, seq)
        x1 = _out_proj(xf, o_ret, o_gla, o_att, w_out[i].astype(BF16))

        wr = jnp.zeros((d, LANES), F32)
        wr = wr.at[:, :N_GROUPS].set(w_router_group[i])
        wr = wr.at[:, ROUTE_LANE0:ROUTE_LANE0 + N_EXPERTS].set(w_router_expert[i].reshape(d, N_EXPERTS))
        br = jnp.zeros((1, LANES), F32)
        br = br.at[0, :N_GROUPS].set(b_router_group[i])
        br = br.at[0, ROUTE_LANE0:ROUTE_LANE0 + N_EXPERTS].set(b_router_expert[i].reshape(N_EXPERTS))
        hp, meta, meta_t, counts = _route(x1, g_ffn[i], wr.astype(BF16), br)
        pos, tile_expert, n_used, pad_start, pad_n, n_rows = _routing_tables(meta_t, counts, t, MOE_TILE)
        xs = _dispatch(hp, pos.reshape(t // tok_tile, 1, 2 * tok_tile), pad_start, pad_n, n_used, n_rows)
        ys = _moe(xs, tile_expert, n_used, w_expert_gate, w_expert_up, w_expert_down, i)
        r = tok_tile // ple_tile
        pos_ple = pos.reshape(t // tok_tile, 2, r, ple_tile).transpose(0, 2, 1, 3).reshape(t // ple_tile, 1, 2 * ple_tile)
        xf = _combine_ple(x1, meta, pos_ple, ys, p[i].reshape(t, -1), g_ple[i],
                          w_ple_gate[i].astype(BF16), w_ple_proj[i].astype(BF16), g_final,
                          final=(i == depth - 1))
    return xf.reshape(batch, seq, d)
```

```python
import functools
import math

import numpy as np
import jax
import jax.numpy as jnp
from jax import lax
from jax.experimental import pallas as pl
from jax.experimental.pallas import tpu as pltpu

F32 = jnp.float32
BF16 = jnp.bfloat16
U32 = jnp.uint32
I32 = jnp.int32

CHUNK = 64
HEAD_DIM = 128
RET_HEADS = 6
GLA_HEADS = 4
GLA_DK = 64
GLA_DV = 128
GLA_RANK = 16
GLA_TAU = 16.0
ATT_HEADS = 6
ATT_BAND_CHUNKS = 8
MAX_REL = 128
N_GROUPS = 4
EXPERTS_PER_GROUP = 8
N_EXPERTS = N_GROUPS * EXPERTS_PER_GROUP
EXPERT_FF = 256
EPS = 1e-6

RET_W = RET_HEADS * HEAD_DIM
GLA_QK = GLA_HEADS * GLA_DK
GLA_V = GLA_HEADS * GLA_DV
ATT_W = ATT_HEADS * HEAD_DIM

LANES = 128
V7X_VMEM_LIMIT = 56 * 1024 * 1024

Z_RET = 0
Z_GQ = 4 * RET_W
Z_GK = Z_GQ + GLA_QK
Z_GV = Z_GK + GLA_QK
Z_GG = Z_GV + GLA_V
Z_ATT = Z_GG + GLA_V
Z_GA = Z_ATT + 3 * ATT_W
Z_W = 7 * 1024

SUPER = 4 * CHUNK
ROUTE_LANE0 = N_GROUPS
MOE_TILE = 256
TOK_TILE = 512
PLE_TILE = 256
ROW_SUB = 8


def _cparams(n_axes):
    return pltpu.CompilerParams(
        dimension_semantics=("arbitrary",) * n_axes,
        vmem_limit_bytes=V7X_VMEM_LIMIT,
    )


def _dot(a, b):
    return jnp.dot(a, b, preferred_element_type=F32)


def _dot_nt(a, b):
    return lax.dot_general(a, b, (((1,), (1,)), ((), ())), preferred_element_type=F32)


def _dot_tn(a, b):
    return lax.dot_general(a, b, (((0,), (0,)), ((), ())), preferred_element_type=F32)


def _rms(x, g):
    return x * lax.rsqrt(jnp.mean(x * x, axis=-1, keepdims=True) + EPS) * g


def _silu(x):
    return x / (1.0 + jnp.exp(-x))


def _pack_bf16_pair(lo, hi):
    lo_b = lax.bitcast_convert_type(lo.astype(BF16).astype(F32), U32)
    hi_b = lax.bitcast_convert_type(hi.astype(BF16).astype(F32), U32)
    return (lo_b >> 16) | (hi_b & jnp.uint32(0xFFFF0000))


def _unpack_bf16_pair(w):
    lo = lax.bitcast_convert_type(w << 16, F32)
    hi = lax.bitcast_convert_type(w & jnp.uint32(0xFFFF0000), F32)
    return lo, hi


def _rot_kernel(inv_ref, cos_ref, sin_ref):
    rows = cos_ref.shape[0]
    pos = (lax.broadcasted_iota(I32, (rows, LANES), 0) + pl.program_id(0) * rows).astype(F32)
    lane = lax.broadcasted_iota(I32, (rows, LANES), 1)
    ang = pos * inv_ref[...]
    cos_ref[...] = jnp.cos(ang)
    s = jnp.sin(ang)
    sin_ref[...] = jnp.where(lane < HEAD_DIM // 2, -s, s)


def _rotary_tables(seq):
    half = HEAD_DIM // 2
    inv = np.float32(1.0) / (np.float32(10000.0) ** (np.arange(half, dtype=np.float32) / np.float32(half)))
    inv2 = jnp.asarray(np.concatenate([inv, inv]).reshape(1, LANES).astype(np.float32))
    rows = min(seq, 1024)
    return pl.pallas_call(
        _rot_kernel,
        grid=(seq // rows,),
        in_specs=[pl.BlockSpec((1, LANES), lambda i: (0, 0))],
        out_specs=[pl.BlockSpec((rows, LANES), lambda i: (i, 0))] * 2,
        out_shape=[jax.ShapeDtypeStruct((seq, LANES), F32)] * 2,
        compiler_params=_cparams(1),
        name="rotary_tables",
    )(inv2)


def _in_kernel(x_ref, g_ref, w_ref, z_ref, h_ref):
    @pl.when(pl.program_id(1) == 0)
    def _():
        h_ref[...] = _rms(x_ref[...], g_ref[...]).astype(BF16)

    z_ref[...] = _dot(h_ref[...], w_ref[...]).astype(BF16)


def _in_proj(x, g, w):
    t, d = x.shape
    n = w.shape[1]
    tm = min(t, 1024)
    tn = min(n, 1024)
    return pl.pallas_call(
        _in_kernel,
        grid=(t // tm, n // tn),
        in_specs=[
            pl.BlockSpec((tm, d), lambda i, j: (i, 0)),
            pl.BlockSpec((1, d), lambda i, j: (0, 0)),
            pl.BlockSpec((d, tn), lambda i, j: (0, j)),
        ],
        out_specs=pl.BlockSpec((tm, tn), lambda i, j: (i, j)),
        out_shape=jax.ShapeDtypeStruct((t, n), BF16),
        scratch_shapes=[pltpu.VMEM((tm, d), BF16)],
        compiler_params=_cparams(2),
        name="in_proj",
    )(x, g.reshape(1, d), w)


def _ret_log_gamma(h):
    return math.log(1.0 - 2.0 ** (-5.0 - h))


def _ret_kernel(q_ref, k_ref, v_ref, g_ref, cos_ref, sin_ref, ms_ref, o_ref,
                state_ref, dmat_ref, xi_ref, zeta_ref):
    s = pl.program_id(1)

    @pl.when(s == 0)
    def _():
        state_ref[...] = jnp.zeros_like(state_ref)
        row = lax.broadcasted_iota(I32, (SUPER, SUPER), 0)
        col = lax.broadcasted_iota(I32, (SUPER, SUPER), 1)
        dist = jnp.abs(row - col).astype(F32)
        keep = (col <= row) | ((row // CHUNK) == (col // CHUNK))
        t = lax.broadcasted_iota(I32, (SUPER, HEAD_DIM), 0).astype(F32)
        for h in range(RET_HEADS):
            lg = _ret_log_gamma(h)
            dmat_ref[h] = jnp.where(keep, jnp.exp(lg * dist), 0.0)
            xi_ref[h] = jnp.exp(lg * (t + 1.0))
            zeta_ref[h] = jnp.exp(lg * (SUPER - 1.0 - t))

    cos = cos_ref[...]
    sin = sin_ref[...]
    scale = HEAD_DIM ** -0.5
    for h in range(RET_HEADS):
        sl = slice(h * HEAD_DIM, (h + 1) * HEAD_DIM)
        q = q_ref[:, sl].astype(F32)
        k = k_ref[:, sl].astype(F32)
        vb = v_ref[:, sl]
        qr = q * cos + pltpu.roll(q, HEAD_DIM // 2, 1) * sin
        kr = (k * cos + pltpu.roll(k, HEAD_DIM // 2, 1) * sin) * scale
        qb = qr.astype(BF16)
        kb = kr.astype(BF16)
        sc = _dot_nt(qb, kb) * dmat_ref[h]
        intra = _dot(sc.astype(BF16), vb)
        st = state_ref[h]
        cross = _dot(qb, st.astype(BF16)) * xi_ref[h]
        kz = (kr * zeta_ref[h]).astype(BF16)
        upd = _dot_tn(kz, vb)
        state_ref[h] = math.exp(_ret_log_gamma(h) * SUPER) * st + upd
        o = intra + cross
        mu = jnp.mean(o, axis=-1, keepdims=True)
        oc = o - mu
        var = jnp.mean(oc * oc, axis=-1, keepdims=True)
        on = oc * lax.rsqrt(var + EPS)
        gate = _silu(g_ref[:, sl].astype(F32))
        o_ref[:, sl] = (on * gate * ms_ref[:, sl]).astype(BF16)


def _retention(z, cos2, sin2, ms, batch, seq):
    t = z.shape[0]
    ns = seq // SUPER
    zspec = lambda c: pl.BlockSpec((SUPER, RET_W), lambda b, s, c=c: (b * ns + s, c))
    return pl.pallas_call(
        _ret_kernel,
        grid=(batch, ns),
        in_specs=[
            zspec(0), zspec(1), zspec(2), zspec(3),
            pl.BlockSpec((SUPER, LANES), lambda b, s: (s, 0)),
            pl.BlockSpec((SUPER, LANES), lambda b, s: (s, 0)),
            pl.BlockSpec((1, RET_W), lambda b, s: (0, 0)),
        ],
        out_specs=pl.BlockSpec((SUPER, RET_W), lambda b, s: (b * ns + s, 0)),
        out_shape=jax.ShapeDtypeStruct((t, RET_W), BF16),
        scratch_shapes=[
            pltpu.VMEM((RET_HEADS, HEAD_DIM, HEAD_DIM), F32),
            pltpu.VMEM((RET_HEADS, SUPER, SUPER), F32),
            pltpu.VMEM((RET_HEADS, SUPER, HEAD_DIM), F32),
            pltpu.VMEM((RET_HEADS, SUPER, HEAD_DIM), F32),
        ],
        compiler_params=_cparams(2),
        name="retention",
    )(z, z, z, z, cos2, sin2, ms.reshape(1, RET_W))


def _gla_kernel(q_ref, k_ref, v_ref, g_ref, a_ref, wa_ref, ba_ref, ms_ref, o_ref, state_ref):
    s = pl.program_id(0)
    nb = q_ref.shape[0]

    @pl.when(s == 0)
    def _():
        state_ref[...] = jnp.zeros_like(state_ref)

    r = lax.broadcasted_iota(I32, (CHUNK, CHUNK), 0)
    c = lax.broadcasted_iota(I32, (CHUNK, CHUNK), 1)
    causal = c <= r
    tril = jnp.where(causal, 1.0, 0.0).astype(BF16)
    ones = jnp.ones((CHUNK, GLA_DV), BF16)
    scale = GLA_DK ** -0.5
    las = []
    for b in range(nb):
        pre = _dot(a_ref[b], wa_ref[...]) + ba_ref[...]
        las.append((jnp.minimum(pre, 0.0) - jnp.log(1.0 + jnp.exp(-jnp.abs(pre)))) * (1.0 / GLA_TAU))
    for ci in range(SUPER // CHUNK):
        rows = slice(ci * CHUNK, (ci + 1) * CHUNK)
        for b in range(nb):
            la_c = las[b][rows, :]
            hi = la_c.astype(BF16)
            lo = (la_c - hi.astype(F32)).astype(BF16)
            bcs = _dot(tril, hi) + _dot(tril, lo)
            gl = _dot_tn(hi, ones) + _dot_tn(lo, ones)
            blast = bcs[CHUNK - 1:CHUNK, :]
            eb = jnp.exp(bcs)
            enb = jnp.exp(-bcs)
            ekb = jnp.exp(blast - bcs)
            qc = q_ref[b, rows, :].astype(F32) * scale * eb
            kc = k_ref[b, rows, :].astype(F32)
            kin = (kc * enb).astype(BF16)
            kup = (kc * ekb).astype(BF16)
            qin = qc.astype(BF16)
            for h in range(GLA_HEADS):
                ks = slice(h * GLA_DK, (h + 1) * GLA_DK)
                vs = slice(h * GLA_DV, (h + 1) * GLA_DV)
                vb = v_ref[b, rows, vs]
                a = jnp.where(causal, _dot_nt(qin[:, ks], kin[:, ks]), 0.0)
                intra = _dot(a.astype(BF16), vb)
                st = state_ref[b, h]
                cross = _dot(qin[:, ks], st.astype(BF16))
                upd = _dot_tn(kup[:, ks], vb)
                state_ref[b, h] = jnp.exp(gl[ks, :]) * st + upd
                o = intra + cross
                on = o * lax.rsqrt(jnp.mean(o * o, axis=-1, keepdims=True) + EPS)
                gate = _silu(g_ref[b, rows, vs].astype(F32))
                o_ref[b, rows, vs] = (on * gate * ms_ref[:, vs]).astype(BF16)


def _gla(z, wa, ba, ms, batch, seq):
    t = z.shape[0]
    ns = seq // SUPER
    z3 = z.reshape(batch, seq, z.shape[1])
    zspec = lambda width, off: pl.BlockSpec((batch, SUPER, width), lambda s: (0, s, off // width))
    out = pl.pallas_call(
        _gla_kernel,
        grid=(ns,),
        in_specs=[
            zspec(GLA_QK, Z_GQ), zspec(GLA_QK, Z_GK), zspec(GLA_V, Z_GV), zspec(GLA_V, Z_GG),
            zspec(LANES, Z_GA),
            pl.BlockSpec((LANES, GLA_QK), lambda s: (0, 0)),
            pl.BlockSpec((1, GLA_QK), lambda s: (0, 0)),
            pl.BlockSpec((1, GLA_V), lambda s: (0, 0)),
        ],
        out_specs=pl.BlockSpec((batch, SUPER, GLA_V), lambda s: (0, s, 0)),
        out_shape=jax.ShapeDtypeStruct((batch, seq, GLA_V), BF16),
        scratch_shapes=[pltpu.VMEM((batch, GLA_HEADS, GLA_DK, GLA_DV), F32)],
        compiler_params=_cparams(1),
        name="gla",
    )(z3, z3, z3, z3, z3, wa, ba.reshape(1, GLA_QK), ms.reshape(1, GLA_V))
    return out.reshape(t, GLA_V)


BAND = 3 * SUPER


BAND_KINDS = 3
ROLL_W = 1024
NEG_INF = -1e30


def _bias_kernel(rb_ref, o_ref):
    l = pl.program_id(0)
    h = pl.program_id(1)
    m = lax.broadcasted_iota(I32, (8, ROLL_W), 1)
    d = jnp.where(m < BAND, m, m - ROLL_W)
    idx = jnp.clip(2 * SUPER - d, -MAX_REL, MAX_REL) + MAX_REL

    def body(j, acc):
        return jnp.where(idx == j, rb_ref[l, h, j], acc)

    diag = lax.fori_loop(0, 2 * MAX_REL + 1, body, jnp.zeros((8, ROLL_W), F32))
    full = pltpu.roll(jnp.broadcast_to(diag[0:1, :], (SUPER, ROLL_W)), 0, 1, stride=1, stride_axis=0)
    bias = full[:, :BAND]
    row = lax.broadcasted_iota(I32, (SUPER, BAND), 0)
    col = lax.broadcasted_iota(I32, (SUPER, BAND), 1)
    dist = 2 * (SUPER // CHUNK) + row // CHUNK - col // CHUNK
    in_band = (dist >= 0) & (dist <= ATT_BAND_CHUNKS)
    for kind in range(BAND_KINDS):
        first_valid_col = (BAND_KINDS - 1 - kind) * SUPER
        o_ref[kind, 0] = jnp.where(in_band & (col >= first_valid_col), bias, NEG_INF)


def _att_bias(rel_bias):
    depth = rel_bias.shape[0]
    return pl.pallas_call(
        _bias_kernel,
        grid=(depth, ATT_HEADS),
        in_specs=[pl.BlockSpec(memory_space=pltpu.SMEM)],
        out_specs=pl.BlockSpec((None, BAND_KINDS, 1, SUPER, BAND), lambda l, h: (l, 0, h, 0, 0)),
        out_shape=jax.ShapeDtypeStruct((depth, BAND_KINDS, ATT_HEADS, SUPER, BAND), F32),
        compiler_params=_cparams(2),
        name="att_bias",
    )(rel_bias)


def _att_kernel(q_ref, k0_ref, k1_ref, k2_ref, v0_ref, v1_ref, v2_ref, bias_ref, ms_ref, o_ref):
    k_refs = (k0_ref, k1_ref, k2_ref)
    v_refs = (v0_ref, v1_ref, v2_ref)
    scale = HEAD_DIM ** -0.5
    for h in range(ATT_HEADS):
        sl = slice(h * HEAD_DIM, (h + 1) * HEAD_DIM)
        qb = (q_ref[:, sl].astype(F32) * scale).astype(BF16)
        sc = [_dot_nt(qb, k_refs[j][:, sl]) + bias_ref[h, :, j * SUPER:(j + 1) * SUPER] for j in range(3)]
        m = jnp.max(jnp.maximum(jnp.maximum(sc[0], sc[1]), sc[2]), axis=-1, keepdims=True)
        ps = [jnp.exp(sj - m) for sj in sc]
        den = jnp.sum(ps[0] + ps[1] + ps[2], axis=-1, keepdims=True)
        acc = _dot(ps[0].astype(BF16), v_refs[0][:, sl])
        for j in (1, 2):
            acc = acc + _dot(ps[j].astype(BF16), v_refs[j][:, sl])
        o = acc / den
        on = o * lax.rsqrt(jnp.mean(o * o, axis=-1, keepdims=True) + EPS)
        o_ref[:, sl] = (on * ms_ref[:, sl]).astype(BF16)


def _band_attention(z, bias, layer, ms, batch, seq):
    t = z.shape[0]
    ns = seq // SUPER
    qc, kc, vc = Z_ATT // ATT_W, Z_ATT // ATT_W + 1, Z_ATT // ATT_W + 2

    def kv_spec(col, back):
        return pl.BlockSpec((SUPER, ATT_W), lambda b, s: (b * ns + jnp.maximum(s - back, 0), col))

    return pl.pallas_call(
        _att_kernel,
        grid=(batch, ns),
        in_specs=[
            pl.BlockSpec((SUPER, ATT_W), lambda b, s: (b * ns + s, qc)),
            kv_spec(kc, 2), kv_spec(kc, 1), kv_spec(kc, 0),
            kv_spec(vc, 2), kv_spec(vc, 1), kv_spec(vc, 0),
            pl.BlockSpec((None, None, ATT_HEADS, SUPER, BAND),
                         lambda b, s: (layer, jnp.minimum(s, BAND_KINDS - 1), 0, 0, 0)),
            pl.BlockSpec((1, ATT_W), lambda b, s: (0, 0)),
        ],
        out_specs=pl.BlockSpec((SUPER, ATT_W), lambda b, s: (b * ns + s, 0)),
        out_shape=jax.ShapeDtypeStruct((t, ATT_W), BF16),
        compiler_params=_cparams(2),
        name="band_attention",
    )(z, z, z, z, z, z, z, bias, ms.reshape(1, ATT_W))


def _out_kernel(x_ref, a_ref, b_ref, c_ref, wa_ref, wb_ref, wc_ref, o_ref):
    o_ref[...] = (x_ref[...] + _dot(a_ref[...], wa_ref[...]) + _dot(b_ref[...], wb_ref[...])
                  + _dot(c_ref[...], wc_ref[...]))


def _out_proj(x, o_ret, o_gla, o_att, w_out):
    t, d = x.shape
    tm = min(t, 512)
    wa, wb, wc = w_out[:RET_W], w_out[RET_W:RET_W + GLA_V], w_out[RET_W + GLA_V:]
    row = lambda w: pl.BlockSpec((tm, w), lambda i: (i, 0))
    full = lambda w: pl.BlockSpec((w, d), lambda i: (0, 0))
    return pl.pallas_call(
        _out_kernel,
        grid=(t // tm,),
        in_specs=[row(d), row(RET_W), row(GLA_V), row(ATT_W), full(RET_W), full(GLA_V), full(ATT_W)],
        out_specs=row(d),
        out_shape=jax.ShapeDtypeStruct((t, d), F32),
        compiler_params=_cparams(1),
        name="out_proj",
    )(x, o_ret, o_gla, o_att, wa, wb, wc)


def _store_token_tiles(ref, packed):
    tm = packed.shape[0]
    for s in range(ROW_SUB):
        ref[pl.ds(s, tm, stride=ROW_SUB), :] = packed[:, s * LANES:(s + 1) * LANES]


def _load_token_tiles(ref, tm):
    return jnp.concatenate([ref[pl.ds(s, tm, stride=ROW_SUB), :] for s in range(ROW_SUB)], axis=1)


def _tile_rows(ref, row, n_rows=1):
    start = row * ROW_SUB
    if not isinstance(start, int):
        start = pl.multiple_of(start, ROW_SUB)
    return ref.at[pl.ds(start, n_rows * ROW_SUB), :]


def _route_kernel(x_ref, g_ref, wr_ref, br_ref, hp_ref, meta_ref, metat_ref, cnt_ref, run_ref):
    i = pl.program_id(0)
    tm = x_ref.shape[0]
    half = x_ref.shape[1] // 2

    @pl.when(i == 0)
    def _():
        run_ref[...] = jnp.zeros_like(run_ref)

    h = _rms(x_ref[...], g_ref[...])
    _store_token_tiles(hp_ref, _pack_bf16_pair(h[:, :half], h[:, half:]))
    lg = _dot(h.astype(BF16), wr_ref[...]) + br_ref[...]
    lane = lax.broadcasted_iota(I32, (tm, LANES), 1)
    neg = jnp.float32(-1e30)
    big = jnp.int32(LANES)
    gl = jnp.where(lane < N_GROUPS, lg, neg)
    gm = jnp.max(gl, axis=-1, keepdims=True)
    p_group = 1.0 / jnp.sum(jnp.exp(gl - gm), axis=-1, keepdims=True)
    g_idx = jnp.min(jnp.where(gl == gm, lane, big), axis=-1, keepdims=True)
    lo = ROUTE_LANE0 + EXPERTS_PER_GROUP * g_idx
    el = jnp.where((lane >= lo) & (lane < lo + EXPERTS_PER_GROUP), lg, neg)
    m1 = jnp.max(el, axis=-1, keepdims=True)
    i1 = jnp.min(jnp.where(el == m1, lane, big), axis=-1, keepdims=True)
    el2 = jnp.where(lane == i1, neg, el)
    m2 = jnp.max(el2, axis=-1, keepdims=True)
    i2 = jnp.min(jnp.where(el2 == m2, lane, big), axis=-1, keepdims=True)
    e2 = jnp.exp(m2 - m1)
    c1 = p_group / (1.0 + e2)
    c2 = p_group * e2 / (1.0 + e2)
    oh1 = jnp.where(lane == i1, 1.0, 0.0)
    oh2 = jnp.where(lane == i2, 1.0, 0.0)
    oh = oh1 + oh2
    r = lax.broadcasted_iota(I32, (tm, tm), 0)
    c = lax.broadcasted_iota(I32, (tm, tm), 1)
    stril = jnp.where(c < r, 1.0, 0.0).astype(BF16)
    before = _dot(stril, oh.astype(BF16)) + run_ref[0:1, :]
    rank1 = jnp.sum(before * oh1, axis=-1, keepdims=True)
    rank2 = jnp.sum(before * oh2, axis=-1, keepdims=True)
    run_ref[0:1, :] = run_ref[0:1, :] + jnp.sum(oh, axis=0, keepdims=True)
    cnt_ref[...] = run_ref[...]
    e1f = (i1 - ROUTE_LANE0).astype(F32)
    e2f = (i2 - ROUTE_LANE0).astype(F32)
    meta = jnp.where(lane == 0, e1f, 0.0)
    meta = jnp.where(lane == 1, e2f, meta)
    meta = jnp.where(lane == 2, rank1, meta)
    meta = jnp.where(lane == 3, rank2, meta)
    meta = jnp.where(lane == 4, c1, meta)
    meta = jnp.where(lane == 5, c2, meta)
    meta_ref[...] = meta
    metat_ref[0] = meta.T[0:8, :]


def _route(x, g, wr, br):
    t, d = x.shape
    tm = min(t, TOK_TILE)
    nt = t // tm
    return pl.pallas_call(
        _route_kernel,
        grid=(nt,),
        in_specs=[
            pl.BlockSpec((tm, d), lambda i: (i, 0)),
            pl.BlockSpec((1, d), lambda i: (0, 0)),
            pl.BlockSpec((d, LANES), lambda i: (0, 0)),
            pl.BlockSpec((1, LANES), lambda i: (0, 0)),
        ],
        out_specs=[
            pl.BlockSpec((tm * ROW_SUB, LANES), lambda i: (i, 0)),
            pl.BlockSpec((tm, LANES), lambda i: (i, 0)),
            pl.BlockSpec((1, 8, tm), lambda i: (i, 0, 0)),
            pl.BlockSpec((8, LANES), lambda i: (0, 0)),
        ],
        out_shape=[
            jax.ShapeDtypeStruct((t * ROW_SUB, LANES), U32),
            jax.ShapeDtypeStruct((t, LANES), F32),
            jax.ShapeDtypeStruct((nt, 8, tm), F32),
            jax.ShapeDtypeStruct((8, LANES), F32),
        ],
        scratch_shapes=[pltpu.VMEM((8, LANES), F32)],
        compiler_params=_cparams(1),
        name="router",
    )(x, g.reshape(1, d), wr, br)


_PAD_BITS = tuple(reversed(range(MOE_TILE.bit_length() - 1)))


def _dispatch_kernel(ps_ref, pn_ref, nu_ref, pos_ref, hp_ref, xs_ref, zero_ref, sem_ref, zsem_ref):
    i = pl.program_id(0)
    tm = pos_ref.shape[2] // 2
    zrows = zero_ref.shape[0] // ROW_SUB
    n_tiles = xs_ref.shape[0] // (ROW_SUB * MOE_TILE)

    def pad_copies(start):
        def zero_copy(off, k):
            cp = pltpu.make_async_copy(_tile_rows(zero_ref, 0, k), _tile_rows(xs_ref, off, k), zsem_ref.at[0])
            if start:
                cp.start()
            else:
                cp.wait()

        def body(e, carry):
            n = pn_ref[e]
            base = ps_ref[e]
            for bit in _PAD_BITS:
                k = 1 << bit

                @pl.when((n & k) != 0)
                def _():
                    zero_copy(base + ((n >> (bit + 1)) << (bit + 1)), k)
            return carry

        lax.fori_loop(0, N_EXPERTS, body, 0)

        def tail(tile, carry):
            for part in range(MOE_TILE // zrows):
                zero_copy(tile * MOE_TILE + part * zrows, zrows)
            return carry

        lax.fori_loop(nu_ref[0], n_tiles, tail, 0)

    @pl.when(i == 0)
    def _():
        zero_ref[...] = jnp.zeros_like(zero_ref)
        pad_copies(True)
        pad_copies(False)

    tok0 = i * tm
    for j in range(2 * tm):
        pltpu.make_async_copy(_tile_rows(hp_ref, tok0 + j % tm), _tile_rows(xs_ref, pos_ref[0, 0, j]),
                              sem_ref.at[i % 2]).start()

    def wait_step(slot):
        for j in range(2 * tm):
            pltpu.make_async_copy(_tile_rows(hp_ref, 0), _tile_rows(xs_ref, 0), sem_ref.at[slot]).wait()

    @pl.when(i > 0)
    def _():
        wait_step((i + 1) % 2)

    @pl.when(i == pl.num_programs(0) - 1)
    def _():
        wait_step(i % 2)


def _dispatch(hp, pos_tiles, pad_start, pad_n, n_used, n_rows):
    nt = pos_tiles.shape[0]
    tm = pos_tiles.shape[2] // 2
    grid_spec = pltpu.PrefetchScalarGridSpec(
        num_scalar_prefetch=3,
        grid=(nt,),
        in_specs=[
            pl.BlockSpec((1, 1, 2 * tm), lambda i, ps, pn, nu: (i, 0, 0), memory_space=pltpu.SMEM),
            pl.BlockSpec(memory_space=pl.ANY),
        ],
        out_specs=pl.BlockSpec(memory_space=pl.ANY),
        scratch_shapes=[
            pltpu.VMEM((MOE_TILE // 2 * ROW_SUB, LANES), U32),
            pltpu.SemaphoreType.DMA((2,)),
            pltpu.SemaphoreType.DMA((1,)),
        ],
    )
    return pl.pallas_call(
        _dispatch_kernel,
        grid_spec=grid_spec,
        out_shape=jax.ShapeDtypeStruct((n_rows * ROW_SUB, LANES), U32),
        compiler_params=_cparams(1),
        name="dispatch",
    )(pad_start, pad_n, n_used, pos_tiles, hp)


def _moe_kernel(te_ref, nu_ref, xs_ref, wg_ref, wu_ref, wd_ref, y_ref, wgb_ref, wub_ref, wdb_ref):
    i = pl.program_id(0)
    n_used = nu_ref[0]
    tm = xs_ref.shape[0] // ROW_SUB
    half = ROW_SUB * LANES
    e = te_ref[i]
    e_prev = te_ref[jnp.maximum(i - 1, 0)]

    @pl.when((i == 0) | (e != e_prev))
    def _():
        wgb_ref[...] = wg_ref[...].astype(BF16)
        wub_ref[...] = wu_ref[...].astype(BF16)
        wdb_ref[...] = wd_ref[...].astype(BF16)

    @pl.when(i < n_used)
    def _():
        lo, hi = _unpack_bf16_pair(_load_token_tiles(xs_ref, tm))
        xl = lo.astype(BF16)
        xh = hi.astype(BF16)
        a = _dot(xl, wgb_ref[:half, :]) + _dot(xh, wgb_ref[half:, :])
        u = _dot(xl, wub_ref[:half, :]) + _dot(xh, wub_ref[half:, :])
        hid = (_silu(a) * u).astype(BF16)
        y = _dot(hid, wdb_ref[...])
        _store_token_tiles(y_ref, _pack_bf16_pair(y[:, :half], y[:, half:]))

    @pl.when(i >= n_used)
    def _():
        y_ref[...] = jnp.zeros_like(y_ref)


def _moe(xs, tile_expert, n_used, wg, wu, wd, layer):
    nt = tile_expert.shape[0]
    tm = xs.shape[0] // ROW_SUB // nt
    d, ff = wg.shape[-2], wg.shape[-1]

    def w_spec(shape):
        return pl.BlockSpec((None, None, None) + shape,
                            lambda i, te, nu: (layer, te[i] // EXPERTS_PER_GROUP, te[i] % EXPERTS_PER_GROUP, 0, 0))

    grid_spec = pltpu.PrefetchScalarGridSpec(
        num_scalar_prefetch=2,
        grid=(nt,),
        in_specs=[
            pl.BlockSpec((tm * ROW_SUB, LANES), lambda i, te, nu: (jnp.minimum(i, nu[0] - 1), 0)),
            w_spec((d, ff)), w_spec((d, ff)), w_spec((ff, d)),
        ],
        out_specs=pl.BlockSpec((tm * ROW_SUB, LANES), lambda i, te, nu: (i, 0)),
        scratch_shapes=[
            pltpu.VMEM((d, ff), BF16),
            pltpu.VMEM((d, ff), BF16),
            pltpu.VMEM((ff, d), BF16),
        ],
    )
    return pl.pallas_call(
        _moe_kernel,
        grid_spec=grid_spec,
        out_shape=jax.ShapeDtypeStruct(xs.shape, U32),
        compiler_params=_cparams(1),
        name="expert_mlp",
    )(tile_expert, n_used, xs, wg, wu, wd)


def _ple_kernel(pos_ref, x_ref, meta_ref, y_ref, p_ref, g_ref, wg_ref, wp_ref, gf_ref, o_ref,
                buf_ref, sem_ref, *, final):
    i = pl.program_id(0)
    n = pl.num_programs(0) - 1
    tm = x_ref.shape[0]
    half = x_ref.shape[1] // 2

    @pl.when(i < n)
    def _():
        slot = i % 2
        for j in range(2 * tm):
            pltpu.make_async_copy(_tile_rows(y_ref, pos_ref[0, 0, j]), _tile_rows(buf_ref.at[slot], j),
                                  sem_ref.at[slot]).start()

    @pl.when(i > 0)
    def _():
        slot = (i + 1) % 2
        for j in range(2 * tm):
            pltpu.make_async_copy(_tile_rows(y_ref, 0), _tile_rows(buf_ref.at[slot], j), sem_ref.at[slot]).wait()
        rows = _load_token_tiles(buf_ref.at[slot], 2 * tm)
        meta = meta_ref[...]
        c1 = meta[:, 4:5]
        c2 = meta[:, 5:6]
        lo1, hi1 = _unpack_bf16_pair(rows[:tm])
        lo2, hi2 = _unpack_bf16_pair(rows[tm:])
        xl = x_ref[:, :half] + c1 * lo1 + c2 * lo2
        xh = x_ref[:, half:] + c1 * hi1 + c2 * hi2
        ms = (jnp.sum(xl * xl, axis=-1, keepdims=True) + jnp.sum(xh * xh, axis=-1, keepdims=True)) / (2 * half)
        inv = lax.rsqrt(ms + EPS)
        hl = (xl * inv * g_ref[:, :half]).astype(BF16)
        hh = (xh * inv * g_ref[:, half:]).astype(BF16)
        gate = _dot(hl, wg_ref[:half, :]) + _dot(hh, wg_ref[half:, :])
        gate = 1.0 / (1.0 + jnp.exp(-gate))
        pp = _dot(p_ref[...].astype(BF16), wp_ref[...])
        ol = xl + gate[:, :half] * pp[:, :half]
        oh = xh + gate[:, half:] * pp[:, half:]
        if final:
            ms2 = (jnp.sum(ol * ol, axis=-1, keepdims=True) + jnp.sum(oh * oh, axis=-1, keepdims=True)) / (2 * half)
            inv2 = lax.rsqrt(ms2 + EPS)
            ol = ol * inv2 * gf_ref[:, :half]
            oh = oh * inv2 * gf_ref[:, half:]
        o_ref[:, :half] = ol
        o_ref[:, half:] = oh


def _combine_ple(x, meta, pos_tiles, ys, p, g_ple, w_gate, w_proj, g_final, final):
    t, d = x.shape
    nt = pos_tiles.shape[0]
    tm = pos_tiles.shape[2] // 2
    pd = p.shape[1]
    const = lambda shape: pl.BlockSpec(shape, lambda i: (0, 0))
    prev = lambda w: pl.BlockSpec((tm, w), lambda i: (jnp.maximum(i - 1, 0), 0))
    return pl.pallas_call(
        functools.partial(_ple_kernel, final=final),
        grid=(nt + 1,),
        in_specs=[
            pl.BlockSpec((1, 1, 2 * tm), lambda i: (jnp.minimum(i, nt - 1), 0, 0), memory_space=pltpu.SMEM),
            prev(d), prev(LANES),
            pl.BlockSpec(memory_space=pl.ANY),
            prev(pd),
            const((1, d)), const((d, d)), const((pd, d)), const((1, d)),
        ],
        out_specs=prev(d),
        out_shape=jax.ShapeDtypeStruct((t, d), F32),
        scratch_shapes=[pltpu.VMEM((2, 2 * tm * ROW_SUB, LANES), U32), pltpu.SemaphoreType.DMA((2,))],
        compiler_params=_cparams(1),
        name="combine_ple",
    )(pos_tiles, x, meta, ys, p, g_ple.reshape(1, d), w_gate, w_proj, g_final.reshape(1, d))


def _prep_w_in(w):
    ga0 = 4 * RET_W + 2 * GLA_QK + 2 * GLA_V
    w = w.astype(BF16)
    pad = jnp.zeros(w.shape[:-1] + (Z_W - Z_GA - GLA_RANK,), BF16)
    return jnp.concatenate([w[..., :ga0], w[..., ga0 + GLA_RANK:], w[..., ga0:ga0 + GLA_RANK], pad], axis=-1)


def _pad_lanes(parts, width, dtype):
    cat = jnp.concatenate(parts, axis=-1)
    pad = jnp.zeros(cat.shape[:-1] + (width - cat.shape[-1],), cat.dtype)
    return jnp.concatenate([cat, pad], axis=-1).astype(dtype)


def _routing_tables(meta_t, counts, n_tok, tile):
    eid = meta_t[:, 0:2, :].astype(I32)
    rank = meta_t[:, 2:4, :].astype(I32)
    cnt = counts[0, ROUTE_LANE0:ROUTE_LANE0 + N_EXPERTS].astype(I32)
    padded = ((cnt + tile - 1) // tile) * tile
    ends = jnp.cumsum(padded)
    offs = ends - padded
    base = jnp.zeros_like(eid)
    for e in range(N_EXPERTS):
        base = jnp.where(eid == e, offs[e], base)
    pos = base + rank
    n_rows = 2 * n_tok + N_EXPERTS * tile
    nt = n_rows // tile
    tile_start = jnp.arange(nt, dtype=I32) * tile
    tile_expert = jnp.minimum(jnp.sum((tile_start[:, None] >= ends[None, :]).astype(I32), axis=1),
                              N_EXPERTS - 1)
    n_used = (ends[-1] // tile).reshape(1)
    return pos, tile_expert, n_used, offs + cnt, padded - cnt, n_rows


def kernel(x, p, g_mix, w_in, gla_w_alpha, gla_b_alpha, rel_bias, mix_scale, w_out, g_ffn,
           w_router_group, b_router_group, w_router_expert, b_router_expert,
           w_expert_gate, w_expert_up, w_expert_down, g_ple, w_ple_gate, w_ple_proj, g_final):
    batch, seq, d = x.shape
    depth = g_mix.shape[0]
    t = batch * seq
    assert seq % SUPER == 0 and d == 2 * ROW_SUB * LANES
    tok_tile = min(t, TOK_TILE)
    ple_tile = min(t, PLE_TILE)
    xf = x.reshape(t, d)
    cos2, sin2 = _rotary_tables(seq)
    w_in_r = _prep_w_in(w_in)
    w_out_b = w_out.astype(BF16)
    wa_all = jnp.concatenate(
        [gla_w_alpha, jnp.zeros((depth, LANES - GLA_RANK, GLA_QK), gla_w_alpha.dtype)], axis=1).astype(BF16)
    wr_all = _pad_lanes([w_router_group, w_router_expert.reshape(depth, d, N_EXPERTS)], LANES, BF16)
    br_all = _pad_lanes([b_router_group, b_router_expert.reshape(depth, N_EXPERTS)], LANES, F32)
    w_pg_b = w_ple_gate.astype(BF16)
    w_pp_b = w_ple_proj.astype(BF16)
    bias_all = _att_bias(rel_bias)
    for i in range(depth):
        z = _in_proj(xf, g_mix[i], w_in_r[i])
        ms = mix_scale[i]
        o_ret = _retention(z, cos2, sin2, ms[:RET_W], batch, seq)
        o_gla = _gla(z, wa_all[i], gla_b_alpha[i], ms[RET_W:RET_W + GLA_V], batch, seq)
        o_att = _band_attention(z, bias_all, i, ms[RET_W + GLA_V:], batch, seq)
        x1 = _out_proj(xf, o_ret, o_gla, o_att, w_out_b[i])
        hp, meta, meta_t, counts = _route(x1, g_ffn[i], wr_all[i], br_all[i:i + 1])
        pos, tile_expert, n_used, pad_start, pad_n, n_rows = _routing_tables(meta_t, counts, t, MOE_TILE)
        xs = _dispatch(hp, pos.reshape(t // tok_tile, 1, 2 * tok_tile), pad_start, pad_n, n_used, n_rows)
        ys = _moe(xs, tile_expert, n_used, w_expert_gate, w_expert_up, w_expert_down, i)
        r = tok_tile // ple_tile
        pos_ple = pos.reshape(t // tok_tile, 2, r, ple_tile).transpose(0, 2, 1, 3).reshape(t // ple_tile, 1, 2 * ple_tile)
        xf = _combine_ple(x1, meta, pos_ple, ys, p[i].reshape(t, -1), g_ple[i], w_pg_b[i], w_pp_b[i], g_final,
                          final=(i == depth - 1))
    return xf.reshape(batch, seq, d)
```

```python
import functools
import math

import numpy as np
import jax
import jax.numpy as jnp
from jax import lax
from jax.experimental import pallas as pl
from jax.experimental.pallas import tpu as pltpu

F32 = jnp.float32
BF16 = jnp.bfloat16
U32 = jnp.uint32
I32 = jnp.int32

CHUNK = 64
HEAD_DIM = 128
RET_HEADS = 6
GLA_HEADS = 4
GLA_DK = 64
GLA_DV = 128
GLA_RANK = 16
GLA_TAU = 16.0
ATT_HEADS = 6
ATT_BAND_CHUNKS = 8
MAX_REL = 128
N_GROUPS = 4
EXPERTS_PER_GROUP = 8
N_EXPERTS = N_GROUPS * EXPERTS_PER_GROUP
EXPERT_FF = 256
EPS = 1e-6

RET_W = RET_HEADS * HEAD_DIM
GLA_QK = GLA_HEADS * GLA_DK
GLA_V = GLA_HEADS * GLA_DV
ATT_W = ATT_HEADS * HEAD_DIM

LANES = 128
V7X_VMEM_LIMIT = 56 * 1024 * 1024

Z_RET = 0
Z_GQ = 4 * RET_W
Z_GK = Z_GQ + GLA_QK
Z_GV = Z_GK + GLA_QK
Z_GG = Z_GV + GLA_V
Z_ATT = Z_GG + GLA_V
Z_GA = Z_ATT + 3 * ATT_W
Z_W = 7 * 1024

SUPER = 4 * CHUNK
ROUTE_LANE0 = N_GROUPS
MOE_TILE = 256
TOK_TILE = 512
PLE_TILE = 256
ROW_SUB = 8


def _cparams(n_axes):
    return pltpu.CompilerParams(
        dimension_semantics=("arbitrary",) * n_axes,
        vmem_limit_bytes=V7X_VMEM_LIMIT,
    )


def _dot(a, b):
    return jnp.dot(a, b, preferred_element_type=F32)


def _dot_nt(a, b):
    return lax.dot_general(a, b, (((1,), (1,)), ((), ())), preferred_element_type=F32)


def _dot_tn(a, b):
    return lax.dot_general(a, b, (((0,), (0,)), ((), ())), preferred_element_type=F32)


def _rms(x, g):
    return x * lax.rsqrt(jnp.mean(x * x, axis=-1, keepdims=True) + EPS) * g


def _silu(x):
    return x / (1.0 + jnp.exp(-x))


def _pack_bf16_pair(lo, hi):
    lo_b = lax.bitcast_convert_type(lo.astype(BF16).astype(F32), U32)
    hi_b = lax.bitcast_convert_type(hi.astype(BF16).astype(F32), U32)
    return (lo_b >> 16) | (hi_b & jnp.uint32(0xFFFF0000))


def _unpack_bf16_pair(w):
    lo = lax.bitcast_convert_type(w << 16, F32)
    hi = lax.bitcast_convert_type(w & jnp.uint32(0xFFFF0000), F32)
    return lo, hi


def _rot_kernel(inv_ref, cos_ref, sin_ref):
    rows = cos_ref.shape[0]
    pos = (lax.broadcasted_iota(I32, (rows, LANES), 0) + pl.program_id(0) * rows).astype(F32)
    lane = lax.broadcasted_iota(I32, (rows, LANES), 1)
    ang = pos * inv_ref[...]
    cos_ref[...] = jnp.cos(ang)
    s = jnp.sin(ang)
    sin_ref[...] = jnp.where(lane < HEAD_DIM // 2, -s, s)


def _rotary_tables(seq):
    half = HEAD_DIM // 2
    inv = np.float32(1.0) / (np.float32(10000.0) ** (np.arange(half, dtype=np.float32) / np.float32(half)))
    inv2 = jnp.asarray(np.concatenate([inv, inv]).reshape(1, LANES).astype(np.float32))
    rows = min(seq, 1024)
    return pl.pallas_call(
        _rot_kernel,
        grid=(seq // rows,),
        in_specs=[pl.BlockSpec((1, LANES), lambda i: (0, 0))],
        out_specs=[pl.BlockSpec((rows, LANES), lambda i: (i, 0))] * 2,
        out_shape=[jax.ShapeDtypeStruct((seq, LANES), F32)] * 2,
        compiler_params=_cparams(1),
        name="rotary_tables",
    )(inv2)


def _in_kernel(x_ref, g_ref, w_ref, z_ref, h_ref):
    @pl.when(pl.program_id(1) == 0)
    def _():
        h_ref[...] = _rms(x_ref[...], g_ref[...]).astype(BF16)

    z_ref[...] = _dot(h_ref[...], w_ref[...]).astype(BF16)


def _in_proj(x, g, w_all, layer):
    t, d = x.shape
    n = w_all.shape[2]
    tm = min(t, 1024)
    tn = min(n, 1024)
    return pl.pallas_call(
        _in_kernel,
        grid=(t // tm, n // tn),
        in_specs=[
            pl.BlockSpec((tm, d), lambda i, j: (i, 0)),
            pl.BlockSpec((1, d), lambda i, j: (0, 0)),
            pl.BlockSpec((None, d, tn), lambda i, j: (layer, 0, j)),
        ],
        out_specs=pl.BlockSpec((tm, tn), lambda i, j: (i, j)),
        out_shape=jax.ShapeDtypeStruct((t, n), BF16),
        scratch_shapes=[pltpu.VMEM((tm, d), BF16)],
        compiler_params=_cparams(2),
        name="in_proj",
    )(x, g.reshape(1, d), w_all)


def _ret_log_gamma(h):
    return math.log(1.0 - 2.0 ** (-5.0 - h))


def _ret_kernel(q_ref, k_ref, v_ref, g_ref, cos_ref, sin_ref, ms_ref, o_ref,
                state_ref, dmat_ref, xi_ref, zeta_ref):
    s = pl.program_id(1)

    @pl.when(s == 0)
    def _():
        state_ref[...] = jnp.zeros_like(state_ref)
        row = lax.broadcasted_iota(I32, (SUPER, SUPER), 0)
        col = lax.broadcasted_iota(I32, (SUPER, SUPER), 1)
        dist = jnp.abs(row - col).astype(F32)
        keep = (col <= row) | ((row // CHUNK) == (col // CHUNK))
        t = lax.broadcasted_iota(I32, (SUPER, HEAD_DIM), 0).astype(F32)
        for h in range(RET_HEADS):
            lg = _ret_log_gamma(h)
            dmat_ref[h] = jnp.where(keep, jnp.exp(lg * dist), 0.0)
            xi_ref[h] = jnp.exp(lg * (t + 1.0))
            zeta_ref[h] = jnp.exp(lg * (SUPER - 1.0 - t))

    cos = cos_ref[...]
    sin = sin_ref[...]
    scale = HEAD_DIM ** -0.5
    for h in range(RET_HEADS):
        sl = slice(h * HEAD_DIM, (h + 1) * HEAD_DIM)
        q = q_ref[:, sl].astype(F32)
        k = k_ref[:, sl].astype(F32)
        vb = v_ref[:, sl]
        qr = q * cos + pltpu.roll(q, HEAD_DIM // 2, 1) * sin
        kr = (k * cos + pltpu.roll(k, HEAD_DIM // 2, 1) * sin) * scale
        qb = qr.astype(BF16)
        kb = kr.astype(BF16)
        sc = _dot_nt(qb, kb) * dmat_ref[h]
        intra = _dot(sc.astype(BF16), vb)
        st = state_ref[h]
        cross = _dot(qb, st.astype(BF16)) * xi_ref[h]
        kz = (kr * zeta_ref[h]).astype(BF16)
        upd = _dot_tn(kz, vb)
        state_ref[h] = math.exp(_ret_log_gamma(h) * SUPER) * st + upd
        o = intra + cross
        mu = jnp.mean(o, axis=-1, keepdims=True)
        oc = o - mu
        var = jnp.mean(oc * oc, axis=-1, keepdims=True)
        on = oc * lax.rsqrt(var + EPS)
        gate = _silu(g_ref[:, sl].astype(F32))
        o_ref[:, sl] = (on * gate * ms_ref[:, sl]).astype(BF16)


def _retention(z, cos2, sin2, ms, batch, seq):
    t = z.shape[0]
    ns = seq // SUPER
    zspec = lambda c: pl.BlockSpec((SUPER, RET_W), lambda b, s, c=c: (b * ns + s, c))
    return pl.pallas_call(
        _ret_kernel,
        grid=(batch, ns),
        in_specs=[
            zspec(0), zspec(1), zspec(2), zspec(3),
            pl.BlockSpec((SUPER, LANES), lambda b, s: (s, 0)),
            pl.BlockSpec((SUPER, LANES), lambda b, s: (s, 0)),
            pl.BlockSpec((1, RET_W), lambda b, s: (0, 0)),
        ],
        out_specs=pl.BlockSpec((SUPER, RET_W), lambda b, s: (b * ns + s, 0)),
        out_shape=jax.ShapeDtypeStruct((t, RET_W), BF16),
        scratch_shapes=[
            pltpu.VMEM((RET_HEADS, HEAD_DIM, HEAD_DIM), F32),
            pltpu.VMEM((RET_HEADS, SUPER, SUPER), F32),
            pltpu.VMEM((RET_HEADS, SUPER, HEAD_DIM), F32),
            pltpu.VMEM((RET_HEADS, SUPER, HEAD_DIM), F32),
        ],
        compiler_params=_cparams(2),
        name="retention",
    )(z, z, z, z, cos2, sin2, ms.reshape(1, RET_W))


def _gla_kernel(q_ref, k_ref, v_ref, g_ref, a_ref, wa_ref, ba_ref, ms_ref, o_ref, state_ref):
    s = pl.program_id(0)
    nb = q_ref.shape[0]

    @pl.when(s == 0)
    def _():
        state_ref[...] = jnp.zeros_like(state_ref)

    r = lax.broadcasted_iota(I32, (CHUNK, CHUNK), 0)
    c = lax.broadcasted_iota(I32, (CHUNK, CHUNK), 1)
    causal = c <= r
    tril = jnp.where(causal, 1.0, 0.0).astype(BF16)
    ones = jnp.ones((CHUNK, GLA_DV), BF16)
    scale = GLA_DK ** -0.5
    las = []
    for b in range(nb):
        pre = _dot(a_ref[b], wa_ref[...]) + ba_ref[...]
        las.append((jnp.minimum(pre, 0.0) - jnp.log(1.0 + jnp.exp(-jnp.abs(pre)))) * (1.0 / GLA_TAU))
    for ci in range(SUPER // CHUNK):
        rows = slice(ci * CHUNK, (ci + 1) * CHUNK)
        for b in range(nb):
            la_c = las[b][rows, :]
            hi = la_c.astype(BF16)
            lo = (la_c - hi.astype(F32)).astype(BF16)
            bcs = _dot(tril, hi) + _dot(tril, lo)
            gl = _dot_tn(hi, ones) + _dot_tn(lo, ones)
            blast = bcs[CHUNK - 1:CHUNK, :]
            eb = jnp.exp(bcs)
            enb = jnp.exp(-bcs)
            ekb = jnp.exp(blast - bcs)
            qc = q_ref[b, rows, :].astype(F32) * scale * eb
            kc = k_ref[b, rows, :].astype(F32)
            kin = (kc * enb).astype(BF16)
            kup = (kc * ekb).astype(BF16)
            qin = qc.astype(BF16)
            for h in range(GLA_HEADS):
                ks = slice(h * GLA_DK, (h + 1) * GLA_DK)
                vs = slice(h * GLA_DV, (h + 1) * GLA_DV)
                vb = v_ref[b, rows, vs]
                a = jnp.where(causal, _dot_nt(qin[:, ks], kin[:, ks]), 0.0)
                intra = _dot(a.astype(BF16), vb)
                st = state_ref[b, h]
                cross = _dot(qin[:, ks], st.astype(BF16))
                upd = _dot_tn(kup[:, ks], vb)
                state_ref[b, h] = jnp.exp(gl[ks, :]) * st + upd
                o = intra + cross
                on = o * lax.rsqrt(jnp.mean(o * o, axis=-1, keepdims=True) + EPS)
                gate = _silu(g_ref[b, rows, vs].astype(F32))
                o_ref[b, rows, vs] = (on * gate * ms_ref[:, vs]).astype(BF16)


def _gla(z, wa_all, layer, ba, ms, batch, seq):
    t = z.shape[0]
    ns = seq // SUPER
    z3 = z.reshape(batch, seq, z.shape[1])
    zspec = lambda width, off: pl.BlockSpec((batch, SUPER, width), lambda s: (0, s, off // width))
    out = pl.pallas_call(
        _gla_kernel,
        grid=(ns,),
        in_specs=[
            zspec(GLA_QK, Z_GQ), zspec(GLA_QK, Z_GK), zspec(GLA_V, Z_GV), zspec(GLA_V, Z_GG),
            zspec(LANES, Z_GA),
            pl.BlockSpec((None, LANES, GLA_QK), lambda s: (layer, 0, 0)),
            pl.BlockSpec((1, GLA_QK), lambda s: (0, 0)),
            pl.BlockSpec((1, GLA_V), lambda s: (0, 0)),
        ],
        out_specs=pl.BlockSpec((batch, SUPER, GLA_V), lambda s: (0, s, 0)),
        out_shape=jax.ShapeDtypeStruct((batch, seq, GLA_V), BF16),
        scratch_shapes=[pltpu.VMEM((batch, GLA_HEADS, GLA_DK, GLA_DV), F32)],
        compiler_params=_cparams(1),
        name="gla",
    )(z3, z3, z3, z3, z3, wa_all, ba.reshape(1, GLA_QK), ms.reshape(1, GLA_V))
    return out.reshape(t, GLA_V)


BAND = 3 * SUPER


BAND_KINDS = 3
ROLL_W = 1024
NEG_INF = -1e30


def _bias_kernel(rb_ref, o_ref):
    l = pl.program_id(0)
    h = pl.program_id(1)
    m = lax.broadcasted_iota(I32, (8, ROLL_W), 1)
    d = jnp.where(m < BAND, m, m - ROLL_W)
    idx = jnp.clip(2 * SUPER - d, -MAX_REL, MAX_REL) + MAX_REL

    def body(j, acc):
        return jnp.where(idx == j, rb_ref[l, h, j], acc)

    diag = lax.fori_loop(0, 2 * MAX_REL + 1, body, jnp.zeros((8, ROLL_W), F32))
    full = pltpu.roll(jnp.broadcast_to(diag[0:1, :], (SUPER, ROLL_W)), 0, 1, stride=1, stride_axis=0)
    bias = full[:, :BAND]
    row = lax.broadcasted_iota(I32, (SUPER, BAND), 0)
    col = lax.broadcasted_iota(I32, (SUPER, BAND), 1)
    dist = 2 * (SUPER // CHUNK) + row // CHUNK - col // CHUNK
    in_band = (dist >= 0) & (dist <= ATT_BAND_CHUNKS)
    for kind in range(BAND_KINDS):
        first_valid_col = (BAND_KINDS - 1 - kind) * SUPER
        o_ref[kind, 0] = jnp.where(in_band & (col >= first_valid_col), bias, NEG_INF)


def _att_bias(rel_bias):
    depth = rel_bias.shape[0]
    return pl.pallas_call(
        _bias_kernel,
        grid=(depth, ATT_HEADS),
        in_specs=[pl.BlockSpec(memory_space=pltpu.SMEM)],
        out_specs=pl.BlockSpec((None, BAND_KINDS, 1, SUPER, BAND), lambda l, h: (l, 0, h, 0, 0)),
        out_shape=jax.ShapeDtypeStruct((depth, BAND_KINDS, ATT_HEADS, SUPER, BAND), F32),
        compiler_params=_cparams(2),
        name="att_bias",
    )(rel_bias)


def _att_kernel(q_ref, k0_ref, k1_ref, k2_ref, v0_ref, v1_ref, v2_ref, bias_ref, ms_ref, o_ref):
    k_refs = (k0_ref, k1_ref, k2_ref)
    v_refs = (v0_ref, v1_ref, v2_ref)
    scale = HEAD_DIM ** -0.5
    for h in range(ATT_HEADS):
        sl = slice(h * HEAD_DIM, (h + 1) * HEAD_DIM)
        qb = (q_ref[:, sl].astype(F32) * scale).astype(BF16)
        sc = [_dot_nt(qb, k_refs[j][:, sl]) + bias_ref[h, :, j * SUPER:(j + 1) * SUPER] for j in range(3)]
        m = jnp.max(jnp.maximum(jnp.maximum(sc[0], sc[1]), sc[2]), axis=-1, keepdims=True)
        ps = [jnp.exp(sj - m) for sj in sc]
        den = jnp.sum(ps[0] + ps[1] + ps[2], axis=-1, keepdims=True)
        acc = _dot(ps[0].astype(BF16), v_refs[0][:, sl])
        for j in (1, 2):
            acc = acc + _dot(ps[j].astype(BF16), v_refs[j][:, sl])
        o = acc / den
        on = o * lax.rsqrt(jnp.mean(o * o, axis=-1, keepdims=True) + EPS)
        o_ref[:, sl] = (on * ms_ref[:, sl]).astype(BF16)


def _band_attention(z, bias, layer, ms, batch, seq):
    t = z.shape[0]
    ns = seq // SUPER
    qc, kc, vc = Z_ATT // ATT_W, Z_ATT // ATT_W + 1, Z_ATT // ATT_W + 2

    def kv_spec(col, back):
        return pl.BlockSpec((SUPER, ATT_W), lambda b, s: (b * ns + jnp.maximum(s - back, 0), col))

    return pl.pallas_call(
        _att_kernel,
        grid=(batch, ns),
        in_specs=[
            pl.BlockSpec((SUPER, ATT_W), lambda b, s: (b * ns + s, qc)),
            kv_spec(kc, 2), kv_spec(kc, 1), kv_spec(kc, 0),
            kv_spec(vc, 2), kv_spec(vc, 1), kv_spec(vc, 0),
            pl.BlockSpec((None, None, ATT_HEADS, SUPER, BAND),
                         lambda b, s: (layer, jnp.minimum(s, BAND_KINDS - 1), 0, 0, 0)),
            pl.BlockSpec((1, ATT_W), lambda b, s: (0, 0)),
        ],
        out_specs=pl.BlockSpec((SUPER, ATT_W), lambda b, s: (b * ns + s, 0)),
        out_shape=jax.ShapeDtypeStruct((t, ATT_W), BF16),
        compiler_params=_cparams(2),
        name="band_attention",
    )(z, z, z, z, z, z, z, bias, ms.reshape(1, ATT_W))


def _out_kernel(x_ref, a_ref, b_ref, c_ref, w_ref, o_ref):
    o_ref[...] = (x_ref[...] + _dot(a_ref[...], w_ref[:RET_W, :])
                  + _dot(b_ref[...], w_ref[RET_W:RET_W + GLA_V, :])
                  + _dot(c_ref[...], w_ref[RET_W + GLA_V:, :]))


def _out_proj(x, o_ret, o_gla, o_att, w_all, layer):
    t, d = x.shape
    tm = min(t, 512)
    row = lambda w: pl.BlockSpec((tm, w), lambda i: (i, 0))
    return pl.pallas_call(
        _out_kernel,
        grid=(t // tm,),
        in_specs=[row(d), row(RET_W), row(GLA_V), row(ATT_W),
                  pl.BlockSpec((None, w_all.shape[1], d), lambda i: (layer, 0, 0))],
        out_specs=row(d),
        out_shape=jax.ShapeDtypeStruct((t, d), F32),
        compiler_params=_cparams(1),
        name="out_proj",
    )(x, o_ret, o_gla, o_att, w_all)


def _store_token_tiles(ref, packed):
    tm = packed.shape[0]
    for s in range(ROW_SUB):
        ref[pl.ds(s, tm, stride=ROW_SUB), :] = packed[:, s * LANES:(s + 1) * LANES]


def _load_token_tiles(ref, tm):
    return jnp.concatenate([ref[pl.ds(s, tm, stride=ROW_SUB), :] for s in range(ROW_SUB)], axis=1)


def _tile_rows(ref, row, n_rows=1):
    start = row * ROW_SUB
    if not isinstance(start, int):
        start = pl.multiple_of(start, ROW_SUB)
    return ref.at[pl.ds(start, n_rows * ROW_SUB), :]


def _route_kernel(x_ref, g_ref, wr_ref, br_ref, hp_ref, meta_ref, metat_ref, cnt_ref, run_ref):
    i = pl.program_id(0)
    tm = x_ref.shape[0]
    half = x_ref.shape[1] // 2

    @pl.when(i == 0)
    def _():
        run_ref[...] = jnp.zeros_like(run_ref)

    h = _rms(x_ref[...], g_ref[...])
    _store_token_tiles(hp_ref, _pack_bf16_pair(h[:, :half], h[:, half:]))
    lg = _dot(h.astype(BF16), wr_ref[...]) + br_ref[...]
    lane = lax.broadcasted_iota(I32, (tm, LANES), 1)
    neg = jnp.float32(-1e30)
    big = jnp.int32(LANES)
    gl = jnp.where(lane < N_GROUPS, lg, neg)
    gm = jnp.max(gl, axis=-1, keepdims=True)
    p_group = 1.0 / jnp.sum(jnp.exp(gl - gm), axis=-1, keepdims=True)
    g_idx = jnp.min(jnp.where(gl == gm, lane, big), axis=-1, keepdims=True)
    lo = ROUTE_LANE0 + EXPERTS_PER_GROUP * g_idx
    el = jnp.where((lane >= lo) & (lane < lo + EXPERTS_PER_GROUP), lg, neg)
    m1 = jnp.max(el, axis=-1, keepdims=True)
    i1 = jnp.min(jnp.where(el == m1, lane, big), axis=-1, keepdims=True)
    el2 = jnp.where(lane == i1, neg, el)
    m2 = jnp.max(el2, axis=-1, keepdims=True)
    i2 = jnp.min(jnp.where(el2 == m2, lane, big), axis=-1, keepdims=True)
    e2 = jnp.exp(m2 - m1)
    c1 = p_group / (1.0 + e2)
    c2 = p_group * e2 / (1.0 + e2)
    oh1 = jnp.where(lane == i1, 1.0, 0.0)
    oh2 = jnp.where(lane == i2, 1.0, 0.0)
    oh = oh1 + oh2
    r = lax.broadcasted_iota(I32, (tm, tm), 0)
    c = lax.broadcasted_iota(I32, (tm, tm), 1)
    stril = jnp.where(c < r, 1.0, 0.0).astype(BF16)
    before = _dot(stril, oh.astype(BF16)) + run_ref[0:1, :]
    rank1 = jnp.sum(before * oh1, axis=-1, keepdims=True)
    rank2 = jnp.sum(before * oh2, axis=-1, keepdims=True)
    run_ref[0:1, :] = run_ref[0:1, :] + jnp.sum(oh, axis=0, keepdims=True)
    cnt_ref[...] = run_ref[...]
    e1f = (i1 - ROUTE_LANE0).astype(F32)
    e2f = (i2 - ROUTE_LANE0).astype(F32)
    meta = jnp.where(lane == 0, e1f, 0.0)
    meta = jnp.where(lane == 1, e2f, meta)
    meta = jnp.where(lane == 2, rank1, meta)
    meta = jnp.where(lane == 3, rank2, meta)
    meta = jnp.where(lane == 4, c1, meta)
    meta = jnp.where(lane == 5, c2, meta)
    meta_ref[...] = meta
    metat_ref[0] = meta.T[0:8, :]


def _route(x, g, wr_all, br_all, layer):
    t, d = x.shape
    tm = min(t, TOK_TILE)
    nt = t // tm
    return pl.pallas_call(
        _route_kernel,
        grid=(nt,),
        in_specs=[
            pl.BlockSpec((tm, d), lambda i: (i, 0)),
            pl.BlockSpec((1, d), lambda i: (0, 0)),
            pl.BlockSpec((None, d, LANES), lambda i: (layer, 0, 0)),
            pl.BlockSpec((None, 1, LANES), lambda i: (layer, 0, 0)),
        ],
        out_specs=[
            pl.BlockSpec((tm * ROW_SUB, LANES), lambda i: (i, 0)),
            pl.BlockSpec((tm, LANES), lambda i: (i, 0)),
            pl.BlockSpec((1, 8, tm), lambda i: (i, 0, 0)),
            pl.BlockSpec((8, LANES), lambda i: (0, 0)),
        ],
        out_shape=[
            jax.ShapeDtypeStruct((t * ROW_SUB, LANES), U32),
            jax.ShapeDtypeStruct((t, LANES), F32),
            jax.ShapeDtypeStruct((nt, 8, tm), F32),
            jax.ShapeDtypeStruct((8, LANES), F32),
        ],
        scratch_shapes=[pltpu.VMEM((8, LANES), F32)],
        compiler_params=_cparams(1),
        name="router",
    )(x, g.reshape(1, d), wr_all, br_all)


_PAD_BITS = tuple(reversed(range(MOE_TILE.bit_length() - 1)))


def _dispatch_kernel(ps_ref, pn_ref, nu_ref, pos_ref, hp_ref, xs_ref, zero_ref, sem_ref, zsem_ref):
    i = pl.program_id(0)
    tm = pos_ref.shape[2] // 2
    zrows = zero_ref.shape[0] // ROW_SUB
    n_tiles = xs_ref.shape[0] // (ROW_SUB * MOE_TILE)

    def pad_copies(start):
        def zero_copy(off, k):
            cp = pltpu.make_async_copy(_tile_rows(zero_ref, 0, k), _tile_rows(xs_ref, off, k), zsem_ref.at[0])
            if start:
                cp.start()
            else:
                cp.wait()

        def body(e, carry):
            n = pn_ref[e]
            base = ps_ref[e]
            for bit in _PAD_BITS:
                k = 1 << bit

                @pl.when((n & k) != 0)
                def _():
                    zero_copy(base + ((n >> (bit + 1)) << (bit + 1)), k)
            return carry

        lax.fori_loop(0, N_EXPERTS, body, 0)

        def tail(tile, carry):
            for part in range(MOE_TILE // zrows):
                zero_copy(tile * MOE_TILE + part * zrows, zrows)
            return carry

        lax.fori_loop(nu_ref[0], n_tiles, tail, 0)

    @pl.when(i == 0)
    def _():
        zero_ref[...] = jnp.zeros_like(zero_ref)
        pad_copies(True)
        pad_copies(False)

    def row_copy(j):
        return pltpu.make_async_copy(_tile_rows(hp_ref, j % tm), _tile_rows(xs_ref, pos_ref[0, 0, j]),
                                     sem_ref.at[0])

    for j in range(2 * tm):
        row_copy(j).start(priority=j % 2)
    for j in range(2 * tm):
        row_copy(j).wait()


def _dispatch(hp, pos_tiles, pad_start, pad_n, n_used, n_rows):
    nt = pos_tiles.shape[0]
    tm = pos_tiles.shape[2] // 2
    grid_spec = pltpu.PrefetchScalarGridSpec(
        num_scalar_prefetch=3,
        grid=(nt,),
        in_specs=[
            pl.BlockSpec((1, 1, 2 * tm), lambda i, ps, pn, nu: (i, 0, 0), memory_space=pltpu.SMEM),
            pl.BlockSpec((tm * ROW_SUB, LANES), lambda i, ps, pn, nu: (i, 0)),
        ],
        out_specs=pl.BlockSpec(memory_space=pl.ANY),
        scratch_shapes=[
            pltpu.VMEM((MOE_TILE // 2 * ROW_SUB, LANES), U32),
            pltpu.SemaphoreType.DMA((1,)),
            pltpu.SemaphoreType.DMA((1,)),
        ],
    )
    return pl.pallas_call(
        _dispatch_kernel,
        grid_spec=grid_spec,
        out_shape=jax.ShapeDtypeStruct((n_rows * ROW_SUB, LANES), U32),
        compiler_params=_cparams(1),
        name="dispatch",
    )(pad_start, pad_n, n_used, pos_tiles, hp)


def _moe_kernel(te_ref, nu_ref, xs_ref, wg_ref, wu_ref, wd_ref, y_ref, wgb_ref, wub_ref, wdb_ref):
    i = pl.program_id(0)
    n_used = nu_ref[0]
    tm = xs_ref.shape[0] // ROW_SUB
    half = ROW_SUB * LANES
    e = te_ref[i]
    e_prev = te_ref[jnp.maximum(i - 1, 0)]

    @pl.when((i == 0) | (e != e_prev))
    def _():
        wgb_ref[...] = wg_ref[...].astype(BF16)
        wub_ref[...] = wu_ref[...].astype(BF16)
        wdb_ref[...] = wd_ref[...].astype(BF16)

    @pl.when(i < n_used)
    def _():
        lo, hi = _unpack_bf16_pair(_load_token_tiles(xs_ref, tm))
        xl = lo.astype(BF16)
        xh = hi.astype(BF16)
        a = _dot(xl, wgb_ref[:half, :]) + _dot(xh, wgb_ref[half:, :])
        u = _dot(xl, wub_ref[:half, :]) + _dot(xh, wub_ref[half:, :])
        hid = (_silu(a) * u).astype(BF16)
        y = _dot(hid, wdb_ref[...])
        _store_token_tiles(y_ref, _pack_bf16_pair(y[:, :half], y[:, half:]))

    @pl.when(i >= n_used)
    def _():
        y_ref[...] = jnp.zeros_like(y_ref)


def _moe(xs, tile_expert, n_used, wg, wu, wd, layer):
    nt = tile_expert.shape[0]
    tm = xs.shape[0] // ROW_SUB // nt
    d, ff = wg.shape[-2], wg.shape[-1]

    def w_spec(shape):
        return pl.BlockSpec((None, None, None) + shape,
                            lambda i, te, nu: (layer, te[i] // EXPERTS_PER_GROUP, te[i] % EXPERTS_PER_GROUP, 0, 0))

    grid_spec = pltpu.PrefetchScalarGridSpec(
        num_scalar_prefetch=2,
        grid=(nt,),
        in_specs=[
            pl.BlockSpec((tm * ROW_SUB, LANES), lambda i, te, nu: (jnp.minimum(i, nu[0] - 1), 0)),
            w_spec((d, ff)), w_spec((d, ff)), w_spec((ff, d)),
        ],
        out_specs=pl.BlockSpec((tm * ROW_SUB, LANES), lambda i, te, nu: (i, 0)),
        scratch_shapes=[
            pltpu.VMEM((d, ff), BF16),
            pltpu.VMEM((d, ff), BF16),
            pltpu.VMEM((ff, d), BF16),
        ],
    )
    return pl.pallas_call(
        _moe_kernel,
        grid_spec=grid_spec,
        out_shape=jax.ShapeDtypeStruct(xs.shape, U32),
        compiler_params=_cparams(1),
        name="expert_mlp",
    )(tile_expert, n_used, xs, wg, wu, wd)


def _ple_kernel(pos_ref, x_ref, meta_ref, y_ref, p_ref, g_ref, wg_ref, wp_ref, gf_ref, o_ref,
                buf_ref, sem_ref, *, final):
    i = pl.program_id(0)
    n = pl.num_programs(0) - 1
    tm = x_ref.shape[0]
    half = x_ref.shape[1] // 2

    for par in range(2):
        @pl.when((i < n) & (i % 2 == par))
        def _():
            for j in range(2 * tm):
                pltpu.make_async_copy(_tile_rows(y_ref, pos_ref[0, 0, j]), _tile_rows(buf_ref.at[par], j),
                                      sem_ref.at[par]).start(priority=j % 2)

    @pl.when(i > 0)
    def _():
        slot = (i + 1) % 2
        for j in range(2 * tm):
            pltpu.make_async_copy(_tile_rows(y_ref, 0), _tile_rows(buf_ref.at[slot], j), sem_ref.at[slot]).wait()
        rows = _load_token_tiles(buf_ref.at[slot], 2 * tm)
        meta = meta_ref[...]
        c1 = meta[:, 4:5]
        c2 = meta[:, 5:6]
        lo1, hi1 = _unpack_bf16_pair(rows[:tm])
        lo2, hi2 = _unpack_bf16_pair(rows[tm:])
        xl = x_ref[:, :half] + c1 * lo1 + c2 * lo2
        xh = x_ref[:, half:] + c1 * hi1 + c2 * hi2
        ms = (jnp.sum(xl * xl, axis=-1, keepdims=True) + jnp.sum(xh * xh, axis=-1, keepdims=True)) / (2 * half)
        inv = lax.rsqrt(ms + EPS)
        hl = (xl * inv * g_ref[:, :half]).astype(BF16)
        hh = (xh * inv * g_ref[:, half:]).astype(BF16)
        gate = _dot(hl, wg_ref[:half, :]) + _dot(hh, wg_ref[half:, :])
        gate = 1.0 / (1.0 + jnp.exp(-gate))
        pp = _dot(p_ref[...].astype(BF16), wp_ref[...])
        ol = xl + gate[:, :half] * pp[:, :half]
        oh = xh + gate[:, half:] * pp[:, half:]
        if final:
            ms2 = (jnp.sum(ol * ol, axis=-1, keepdims=True) + jnp.sum(oh * oh, axis=-1, keepdims=True)) / (2 * half)
            inv2 = lax.rsqrt(ms2 + EPS)
            ol = ol * inv2 * gf_ref[:, :half]
            oh = oh * inv2 * gf_ref[:, half:]
        o_ref[:, :half] = ol
        o_ref[:, half:] = oh


def _combine_ple(x, meta, pos_tiles, ys, p_all, g_ple, w_gate_all, w_proj_all, layer, g_final, final):
    t, d = x.shape
    nt = pos_tiles.shape[0]
    tm = pos_tiles.shape[2] // 2
    pd = p_all.shape[2]
    const = lambda shape: pl.BlockSpec(shape, lambda i: (0, 0))
    prev = lambda w: pl.BlockSpec((tm, w), lambda i: (jnp.maximum(i - 1, 0), 0))
    return pl.pallas_call(
        functools.partial(_ple_kernel, final=final),
        grid=(nt + 1,),
        in_specs=[
            pl.BlockSpec((1, 1, 2 * tm), lambda i: (jnp.minimum(i, nt - 1), 0, 0), memory_space=pltpu.SMEM),
            prev(d), prev(LANES),
            pl.BlockSpec(memory_space=pl.ANY),
            pl.BlockSpec((None, tm, pd), lambda i: (layer, jnp.maximum(i - 1, 0), 0)),
            const((1, d)),
            pl.BlockSpec((None, d, d), lambda i: (layer, 0, 0)),
            pl.BlockSpec((None, pd, d), lambda i: (layer, 0, 0)),
            const((1, d)),
        ],
        out_specs=prev(d),
        out_shape=jax.ShapeDtypeStruct((t, d), F32),
        scratch_shapes=[pltpu.VMEM((2, 2 * tm * ROW_SUB, LANES), U32), pltpu.SemaphoreType.DMA((2,))],
        compiler_params=_cparams(1),
        name="combine_ple",
    )(pos_tiles, x, meta, ys, p_all, g_ple.reshape(1, d), w_gate_all, w_proj_all, g_final.reshape(1, d))


def _prep_w_in(w):
    ga0 = 4 * RET_W + 2 * GLA_QK + 2 * GLA_V
    w = w.astype(BF16)
    pad = jnp.zeros(w.shape[:-1] + (Z_W - Z_GA - GLA_RANK,), BF16)
    return jnp.concatenate([w[..., :ga0], w[..., ga0 + GLA_RANK:], w[..., ga0:ga0 + GLA_RANK], pad], axis=-1)


def _pad_lanes(parts, width, dtype):
    cat = jnp.concatenate(parts, axis=-1)
    pad = jnp.zeros(cat.shape[:-1] + (width - cat.shape[-1],), cat.dtype)
    return jnp.concatenate([cat, pad], axis=-1).astype(dtype)


def _routing_tables(meta_t, counts, n_tok, tile):
    eid = meta_t[:, 0:2, :].astype(I32)
    rank = meta_t[:, 2:4, :].astype(I32)
    cnt = counts[0, ROUTE_LANE0:ROUTE_LANE0 + N_EXPERTS].astype(I32)
    padded = ((cnt + tile - 1) // tile) * tile
    ends = jnp.cumsum(padded)
    offs = ends - padded
    base = jnp.zeros_like(eid)
    for e in range(N_EXPERTS):
        base = jnp.where(eid == e, offs[e], base)
    pos = base + rank
    n_rows = 2 * n_tok + N_EXPERTS * tile
    nt = n_rows // tile
    tile_start = jnp.arange(nt, dtype=I32) * tile
    tile_expert = jnp.minimum(jnp.sum((tile_start[:, None] >= ends[None, :]).astype(I32), axis=1),
                              N_EXPERTS - 1)
    n_used = (ends[-1] // tile).reshape(1)
    return pos, tile_expert, n_used, offs + cnt, padded - cnt, n_rows


def kernel(x, p, g_mix, w_in, gla_w_alpha, gla_b_alpha, rel_bias, mix_scale, w_out, g_ffn,
           w_router_group, b_router_group, w_router_expert, b_router_expert,
           w_expert_gate, w_expert_up, w_expert_down, g_ple, w_ple_gate, w_ple_proj, g_final):
    batch, seq, d = x.shape
    depth = g_mix.shape[0]
    t = batch * seq
    assert seq % SUPER == 0 and d == 2 * ROW_SUB * LANES
    tok_tile = min(t, TOK_TILE)
    ple_tile = min(t, PLE_TILE)
    xf = x.reshape(t, d)
    cos2, sin2 = _rotary_tables(seq)
    w_in_r = _prep_w_in(w_in)
    w_out_b = w_out.astype(BF16)
    wa_all = jnp.concatenate(
        [gla_w_alpha, jnp.zeros((depth, LANES - GLA_RANK, GLA_QK), gla_w_alpha.dtype)], axis=1).astype(BF16)
    wr_all = _pad_lanes([w_router_group, w_router_expert.reshape(depth, d, N_EXPERTS)], LANES, BF16)
    br_all = _pad_lanes([b_router_group, b_router_expert.reshape(depth, N_EXPERTS)], LANES, F32)
    br_all = br_all.reshape(depth, 1, LANES)
    p_all = p.reshape(depth, t, p.shape[-1])
    w_pg_b = w_ple_gate.astype(BF16)
    w_pp_b = w_ple_proj.astype(BF16)
    bias_all = _att_bias(rel_bias)
    for i in range(depth):
        z = _in_proj(xf, g_mix[i], w_in_r, i)
        ms = mix_scale[i]
        o_ret = _retention(z, cos2, sin2, ms[:RET_W], batch, seq)
        o_gla = _gla(z, wa_all, i, gla_b_alpha[i], ms[RET_W:RET_W + GLA_V], batch, seq)
        o_att = _band_attention(z, bias_all, i, ms[RET_W + GLA_V:], batch, seq)
        x1 = _out_proj(xf, o_ret, o_gla, o_att, w_out_b, i)
        hp, meta, meta_t, counts = _route(x1, g_ffn[i], wr_all, br_all, i)
        pos, tile_expert, n_used, pad_start, pad_n, n_rows = _routing_tables(meta_t, counts, t, MOE_TILE)
        xs = _dispatch(hp, pos.reshape(t // tok_tile, 1, 2 * tok_tile), pad_start, pad_n, n_used, n_rows)
        ys = _moe(xs, tile_expert, n_used, w_expert_gate, w_expert_up, w_expert_down, i)
        r = tok_tile // ple_tile
        pos_ple = pos.reshape(t // tok_tile, 2, r, ple_tile).transpose(0, 2, 1, 3).reshape(t // ple_tile, 1, 2 * ple_tile)
        xf = _combine_ple(x1, meta, pos_ple, ys, p_all, g_ple[i], w_pg_b, w_pp_b, i, g_final,
                          final=(i == depth - 1))
    return xf.reshape(batch, seq, d)
```

```python
import functools
import math

import numpy as np
import jax
import jax.numpy as jnp
from jax import lax
from jax.experimental import pallas as pl
from jax.experimental.pallas import tpu as pltpu

F32 = jnp.float32
BF16 = jnp.bfloat16
U32 = jnp.uint32
I32 = jnp.int32

CHUNK = 64
HEAD_DIM = 128
RET_HEADS = 6
GLA_HEADS = 4
GLA_DK = 64
GLA_DV = 128
GLA_RANK = 16
GLA_TAU = 16.0
ATT_HEADS = 6
ATT_BAND_CHUNKS = 8
MAX_REL = 128
N_GROUPS = 4
EXPERTS_PER_GROUP = 8
N_EXPERTS = N_GROUPS * EXPERTS_PER_GROUP
EXPERT_FF = 256
EPS = 1e-6

RET_W = RET_HEADS * HEAD_DIM
GLA_QK = GLA_HEADS * GLA_DK
GLA_V = GLA_HEADS * GLA_DV
ATT_W = ATT_HEADS * HEAD_DIM

LANES = 128
V7X_VMEM_LIMIT = 56 * 1024 * 1024

Z_RET = 0
Z_GQ = 4 * RET_W
Z_GK = Z_GQ + GLA_QK
Z_GV = Z_GK + GLA_QK
Z_GG = Z_GV + GLA_V
Z_ATT = Z_GG + GLA_V
Z_GA = Z_ATT + 3 * ATT_W
Z_W = 7 * 1024

SUPER = 4 * CHUNK
ROUTE_LANE0 = N_GROUPS
MOE_TILE = 256
TOK_TILE = 512
PLE_TILE = 256
ROW_SUB = 8


def _cparams(n_axes):
    return pltpu.CompilerParams(
        dimension_semantics=("arbitrary",) * n_axes,
        vmem_limit_bytes=V7X_VMEM_LIMIT,
    )


def _dot(a, b):
    return jnp.dot(a, b, preferred_element_type=F32)


def _dot_nt(a, b):
    return lax.dot_general(a, b, (((1,), (1,)), ((), ())), preferred_element_type=F32)


def _dot_tn(a, b):
    return lax.dot_general(a, b, (((0,), (0,)), ((), ())), preferred_element_type=F32)


def _rms(x, g):
    return x * lax.rsqrt(jnp.mean(x * x, axis=-1, keepdims=True) + EPS) * g


def _silu(x):
    return x / (1.0 + jnp.exp(-x))


def _pack_bf16_pair(lo, hi):
    lo_b = lax.bitcast_convert_type(lo.astype(BF16).astype(F32), U32)
    hi_b = lax.bitcast_convert_type(hi.astype(BF16).astype(F32), U32)
    return (lo_b >> 16) | (hi_b & jnp.uint32(0xFFFF0000))


def _unpack_bf16_pair(w):
    lo = lax.bitcast_convert_type(w << 16, F32)
    hi = lax.bitcast_convert_type(w & jnp.uint32(0xFFFF0000), F32)
    return lo, hi


def _rot_kernel(inv_ref, cos_ref, sin_ref):
    rows = cos_ref.shape[0]
    pos = (lax.broadcasted_iota(I32, (rows, LANES), 0) + pl.program_id(0) * rows).astype(F32)
    lane = lax.broadcasted_iota(I32, (rows, LANES), 1)
    ang = pos * inv_ref[...]
    cos_ref[...] = jnp.cos(ang)
    s = jnp.sin(ang)
    sin_ref[...] = jnp.where(lane < HEAD_DIM // 2, -s, s)


def _rotary_tables(seq):
    half = HEAD_DIM // 2
    inv = np.float32(1.0) / (np.float32(10000.0) ** (np.arange(half, dtype=np.float32) / np.float32(half)))
    inv2 = jnp.asarray(np.concatenate([inv, inv]).reshape(1, LANES).astype(np.float32))
    rows = min(seq, 1024)
    return pl.pallas_call(
        _rot_kernel,
        grid=(seq // rows,),
        in_specs=[pl.BlockSpec((1, LANES), lambda i: (0, 0))],
        out_specs=[pl.BlockSpec((rows, LANES), lambda i: (i, 0))] * 2,
        out_shape=[jax.ShapeDtypeStruct((seq, LANES), F32)] * 2,
        compiler_params=_cparams(1),
        name="rotary_tables",
    )(inv2)


def _in_kernel(x_ref, g_ref, w_ref, z_ref, h_ref):
    @pl.when(pl.program_id(1) == 0)
    def _():
        h_ref[...] = _rms(x_ref[...], g_ref[...]).astype(BF16)

    z_ref[...] = _dot(h_ref[...], w_ref[...]).astype(BF16)


def _in_proj(x, g, w_all, layer):
    t, d = x.shape
    n = w_all.shape[2]
    tm = min(t, 1024)
    tn = min(n, 1024)
    return pl.pallas_call(
        _in_kernel,
        grid=(t // tm, n // tn),
        in_specs=[
            pl.BlockSpec((tm, d), lambda i, j: (i, 0)),
            pl.BlockSpec((1, d), lambda i, j: (0, 0)),
            pl.BlockSpec((None, d, tn), lambda i, j: (layer, 0, j)),
        ],
        out_specs=pl.BlockSpec((tm, tn), lambda i, j: (i, j)),
        out_shape=jax.ShapeDtypeStruct((t, n), BF16),
        scratch_shapes=[pltpu.VMEM((tm, d), BF16)],
        compiler_params=_cparams(2),
        name="in_proj",
    )(x, g.reshape(1, d), w_all)


def _ret_log_gamma(h):
    return math.log(1.0 - 2.0 ** (-5.0 - h))


def _ret_kernel(q_ref, k_ref, v_ref, g_ref, cos_ref, sin_ref, ms_ref, o_ref,
                state_ref, dmat_ref, xi_ref, zeta_ref):
    s = pl.program_id(1)

    @pl.when(s == 0)
    def _():
        state_ref[...] = jnp.zeros_like(state_ref)
        row = lax.broadcasted_iota(I32, (SUPER, SUPER), 0)
        col = lax.broadcasted_iota(I32, (SUPER, SUPER), 1)
        dist = jnp.abs(row - col).astype(F32)
        keep = (col <= row) | ((row // CHUNK) == (col // CHUNK))
        t = lax.broadcasted_iota(I32, (SUPER, HEAD_DIM), 0).astype(F32)
        for h in range(RET_HEADS):
            lg = _ret_log_gamma(h)
            dmat_ref[h] = jnp.where(keep, jnp.exp(lg * dist), 0.0)
            xi_ref[h] = jnp.exp(lg * (t + 1.0))
            zeta_ref[h] = jnp.exp(lg * (SUPER - 1.0 - t))

    cos = cos_ref[...]
    sin = sin_ref[...]
    scale = HEAD_DIM ** -0.5
    for h in range(RET_HEADS):
        sl = slice(h * HEAD_DIM, (h + 1) * HEAD_DIM)
        q = q_ref[:, sl].astype(F32)
        k = k_ref[:, sl].astype(F32)
        vb = v_ref[:, sl]
        qr = q * cos + pltpu.roll(q, HEAD_DIM // 2, 1) * sin
        kr = (k * cos + pltpu.roll(k, HEAD_DIM // 2, 1) * sin) * scale
        qb = qr.astype(BF16)
        kb = kr.astype(BF16)
        sc = _dot_nt(qb, kb) * dmat_ref[h]
        intra = _dot(sc.astype(BF16), vb)
        st = state_ref[h]
        cross = _dot(qb, st.astype(BF16)) * xi_ref[h]
        kz = (kr * zeta_ref[h]).astype(BF16)
        upd = _dot_tn(kz, vb)
        state_ref[h] = math.exp(_ret_log_gamma(h) * SUPER) * st + upd
        o = intra + cross
        mu = jnp.mean(o, axis=-1, keepdims=True)
        oc = o - mu
        var = jnp.mean(oc * oc, axis=-1, keepdims=True)
        on = oc * lax.rsqrt(var + EPS)
        gate = _silu(g_ref[:, sl].astype(F32))
        o_ref[:, sl] = (on * gate * ms_ref[:, sl]).astype(BF16)


def _retention(z, cos2, sin2, ms, batch, seq):
    t = z.shape[0]
    ns = seq // SUPER
    zspec = lambda c: pl.BlockSpec((SUPER, RET_W), lambda b, s, c=c: (b * ns + s, c))
    return pl.pallas_call(
        _ret_kernel,
        grid=(batch, ns),
        in_specs=[
            zspec(0), zspec(1), zspec(2), zspec(3),
            pl.BlockSpec((SUPER, LANES), lambda b, s: (s, 0)),
            pl.BlockSpec((SUPER, LANES), lambda b, s: (s, 0)),
            pl.BlockSpec((1, RET_W), lambda b, s: (0, 0)),
        ],
        out_specs=pl.BlockSpec((SUPER, RET_W), lambda b, s: (b * ns + s, 0)),
        out_shape=jax.ShapeDtypeStruct((t, RET_W), BF16),
        scratch_shapes=[
            pltpu.VMEM((RET_HEADS, HEAD_DIM, HEAD_DIM), F32),
            pltpu.VMEM((RET_HEADS, SUPER, SUPER), F32),
            pltpu.VMEM((RET_HEADS, SUPER, HEAD_DIM), F32),
            pltpu.VMEM((RET_HEADS, SUPER, HEAD_DIM), F32),
        ],
        compiler_params=_cparams(2),
        name="retention",
    )(z, z, z, z, cos2, sin2, ms.reshape(1, RET_W))


def _gla_kernel(q_ref, k_ref, v_ref, g_ref, a_ref, wa_ref, ba_ref, ms_ref, o_ref, state_ref):
    s = pl.program_id(0)
    nb = q_ref.shape[0]

    @pl.when(s == 0)
    def _():
        state_ref[...] = jnp.zeros_like(state_ref)

    r = lax.broadcasted_iota(I32, (CHUNK, CHUNK), 0)
    c = lax.broadcasted_iota(I32, (CHUNK, CHUNK), 1)
    causal = c <= r
    tril = jnp.where(causal, 1.0, 0.0).astype(BF16)
    ones = jnp.ones((CHUNK, GLA_DV), BF16)
    scale = GLA_DK ** -0.5
    las = []
    for b in range(nb):
        pre = _dot(a_ref[b], wa_ref[...]) + ba_ref[...]
        las.append((jnp.minimum(pre, 0.0) - jnp.log(1.0 + jnp.exp(-jnp.abs(pre)))) * (1.0 / GLA_TAU))
    for ci in range(SUPER // CHUNK):
        rows = slice(ci * CHUNK, (ci + 1) * CHUNK)
        for b in range(nb):
            la_c = las[b][rows, :]
            hi = la_c.astype(BF16)
            lo = (la_c - hi.astype(F32)).astype(BF16)
            bcs = _dot(tril, hi) + _dot(tril, lo)
            gl = _dot_tn(hi, ones) + _dot_tn(lo, ones)
            blast = bcs[CHUNK - 1:CHUNK, :]
            eb = jnp.exp(bcs)
            enb = jnp.exp(-bcs)
            ekb = jnp.exp(blast - bcs)
            qc = q_ref[b, rows, :].astype(F32) * scale * eb
            kc = k_ref[b, rows, :].astype(F32)
            kin = (kc * enb).astype(BF16)
            kup = (kc * ekb).astype(BF16)
            qin = qc.astype(BF16)
            for h in range(GLA_HEADS):
                ks = slice(h * GLA_DK, (h + 1) * GLA_DK)
                vs = slice(h * GLA_DV, (h + 1) * GLA_DV)
                vb = v_ref[b, rows, vs]
                a = jnp.where(causal, _dot_nt(qin[:, ks], kin[:, ks]), 0.0)
                intra = _dot(a.astype(BF16), vb)
                st = state_ref[b, h]
                cross = _dot(qin[:, ks], st.astype(BF16))
                upd = _dot_tn(kup[:, ks], vb)
                state_ref[b, h] = jnp.exp(gl[ks, :]) * st + upd
                o = intra + cross
                on = o * lax.rsqrt(jnp.mean(o * o, axis=-1, keepdims=True) + EPS)
                gate = _silu(g_ref[b, rows, vs].astype(F32))
                o_ref[b, rows, vs] = (on * gate * ms_ref[:, vs]).astype(BF16)


def _gla(z, wa_all, layer, ba, ms, batch, seq):
    t = z.shape[0]
    ns = seq // SUPER
    z3 = z.reshape(batch, seq, z.shape[1])
    zspec = lambda width, off: pl.BlockSpec((batch, SUPER, width), lambda s: (0, s, off // width))
    out = pl.pallas_call(
        _gla_kernel,
        grid=(ns,),
        in_specs=[
            zspec(GLA_QK, Z_GQ), zspec(GLA_QK, Z_GK), zspec(GLA_V, Z_GV), zspec(GLA_V, Z_GG),
            zspec(LANES, Z_GA),
            pl.BlockSpec((None, LANES, GLA_QK), lambda s: (layer, 0, 0)),
            pl.BlockSpec((1, GLA_QK), lambda s: (0, 0)),
            pl.BlockSpec((1, GLA_V), lambda s: (0, 0)),
        ],
        out_specs=pl.BlockSpec((batch, SUPER, GLA_V), lambda s: (0, s, 0)),
        out_shape=jax.ShapeDtypeStruct((batch, seq, GLA_V), BF16),
        scratch_shapes=[pltpu.VMEM((batch, GLA_HEADS, GLA_DK, GLA_DV), F32)],
        compiler_params=_cparams(1),
        name="gla",
    )(z3, z3, z3, z3, z3, wa_all, ba.reshape(1, GLA_QK), ms.reshape(1, GLA_V))
    return out.reshape(t, GLA_V)


BAND = 3 * SUPER


BAND_KINDS = 3
ROLL_W = 1024
NEG_INF = -1e30


def _bias_kernel(rb_ref, o_ref):
    l = pl.program_id(0)
    h = pl.program_id(1)
    m = lax.broadcasted_iota(I32, (8, ROLL_W), 1)
    d = jnp.where(m < BAND, m, m - ROLL_W)
    idx = jnp.clip(2 * SUPER - d, -MAX_REL, MAX_REL) + MAX_REL

    def body(j, acc):
        return jnp.where(idx == j, rb_ref[l, h, j], acc)

    diag = lax.fori_loop(0, 2 * MAX_REL + 1, body, jnp.zeros((8, ROLL_W), F32))
    full = pltpu.roll(jnp.broadcast_to(diag[0:1, :], (SUPER, ROLL_W)), 0, 1, stride=1, stride_axis=0)
    bias = full[:, :BAND]
    row = lax.broadcasted_iota(I32, (SUPER, BAND), 0)
    col = lax.broadcasted_iota(I32, (SUPER, BAND), 1)
    dist = 2 * (SUPER // CHUNK) + row // CHUNK - col // CHUNK
    in_band = (dist >= 0) & (dist <= ATT_BAND_CHUNKS)
    for kind in range(BAND_KINDS):
        first_valid_col = (BAND_KINDS - 1 - kind) * SUPER
        o_ref[kind, 0] = jnp.where(in_band & (col >= first_valid_col), bias, NEG_INF)


def _att_bias(rel_bias):
    depth = rel_bias.shape[0]
    return pl.pallas_call(
        _bias_kernel,
        grid=(depth, ATT_HEADS),
        in_specs=[pl.BlockSpec(memory_space=pltpu.SMEM)],
        out_specs=pl.BlockSpec((None, BAND_KINDS, 1, SUPER, BAND), lambda l, h: (l, 0, h, 0, 0)),
        out_shape=jax.ShapeDtypeStruct((depth, BAND_KINDS, ATT_HEADS, SUPER, BAND), F32),
        compiler_params=_cparams(2),
        name="att_bias",
    )(rel_bias)


def _att_kernel(q_ref, k0_ref, k1_ref, k2_ref, v0_ref, v1_ref, v2_ref, bias_ref, ms_ref, o_ref):
    k_refs = (k0_ref, k1_ref, k2_ref)
    v_refs = (v0_ref, v1_ref, v2_ref)
    scale = HEAD_DIM ** -0.5
    for h in range(ATT_HEADS):
        sl = slice(h * HEAD_DIM, (h + 1) * HEAD_DIM)
        qb = (q_ref[:, sl].astype(F32) * scale).astype(BF16)
        sc = [_dot_nt(qb, k_refs[j][:, sl]) + bias_ref[h, :, j * SUPER:(j + 1) * SUPER] for j in range(3)]
        m = jnp.max(jnp.maximum(jnp.maximum(sc[0], sc[1]), sc[2]), axis=-1, keepdims=True)
        ps = [jnp.exp(sj - m) for sj in sc]
        den = jnp.sum(ps[0] + ps[1] + ps[2], axis=-1, keepdims=True)
        acc = _dot(ps[0].astype(BF16), v_refs[0][:, sl])
        for j in (1, 2):
            acc = acc + _dot(ps[j].astype(BF16), v_refs[j][:, sl])
        o = acc / den
        on = o * lax.rsqrt(jnp.mean(o * o, axis=-1, keepdims=True) + EPS)
        o_ref[:, sl] = (on * ms_ref[:, sl]).astype(BF16)


def _band_attention(z, bias, layer, ms, batch, seq):
    t = z.shape[0]
    ns = seq // SUPER
    qc, kc, vc = Z_ATT // ATT_W, Z_ATT // ATT_W + 1, Z_ATT // ATT_W + 2

    def kv_spec(col, back):
        return pl.BlockSpec((SUPER, ATT_W), lambda b, s: (b * ns + jnp.maximum(s - back, 0), col))

    return pl.pallas_call(
        _att_kernel,
        grid=(batch, ns),
        in_specs=[
            pl.BlockSpec((SUPER, ATT_W), lambda b, s: (b * ns + s, qc)),
            kv_spec(kc, 2), kv_spec(kc, 1), kv_spec(kc, 0),
            kv_spec(vc, 2), kv_spec(vc, 1), kv_spec(vc, 0),
            pl.BlockSpec((None, None, ATT_HEADS, SUPER, BAND),
                         lambda b, s: (layer, jnp.minimum(s, BAND_KINDS - 1), 0, 0, 0)),
            pl.BlockSpec((1, ATT_W), lambda b, s: (0, 0)),
        ],
        out_specs=pl.BlockSpec((SUPER, ATT_W), lambda b, s: (b * ns + s, 0)),
        out_shape=jax.ShapeDtypeStruct((t, ATT_W), BF16),
        compiler_params=_cparams(2),
        name="band_attention",
    )(z, z, z, z, z, z, z, bias, ms.reshape(1, ATT_W))


def _store_token_tiles(ref, packed):
    tm = packed.shape[0]
    for s in range(ROW_SUB):
        ref[pl.ds(s, tm, stride=ROW_SUB), :] = packed[:, s * LANES:(s + 1) * LANES]


def _load_token_tiles(ref, tm):
    return jnp.concatenate([ref[pl.ds(s, tm, stride=ROW_SUB), :] for s in range(ROW_SUB)], axis=1)


def _tile_rows(ref, row, n_rows=1):
    start = row * ROW_SUB
    if not isinstance(start, int):
        start = pl.multiple_of(start, ROW_SUB)
    return ref.at[pl.ds(start, n_rows * ROW_SUB), :]


def _route_kernel(x_ref, a_ref, b_ref, c_ref, w_ref, g_ref, wr_ref, br_ref,
                  x1_ref, hp_ref, meta_ref, metat_ref, cnt_ref, stage_ref, run_ref):
    i = pl.program_id(0)
    tm = x_ref.shape[0]
    half = x_ref.shape[1] // 2

    @pl.when(i == 0)
    def _():
        run_ref[...] = jnp.zeros_like(run_ref)
        stage_ref[...] = jnp.zeros_like(stage_ref)

    xr = stage_ref[...]
    live = jnp.where(i > 0, 1.0, 0.0)

    h = _rms(xr, g_ref[...])
    _store_token_tiles(hp_ref, _pack_bf16_pair(h[:, :half], h[:, half:]))
    lg = _dot(h.astype(BF16), wr_ref[...]) + br_ref[...]
    lane = lax.broadcasted_iota(I32, (tm, LANES), 1)
    neg = jnp.float32(-1e30)
    big = jnp.int32(LANES)
    gl = jnp.where(lane < N_GROUPS, lg, neg)
    gm = jnp.max(gl, axis=-1, keepdims=True)
    p_group = 1.0 / jnp.sum(jnp.exp(gl - gm), axis=-1, keepdims=True)
    g_idx = jnp.min(jnp.where(gl == gm, lane, big), axis=-1, keepdims=True)
    lo = ROUTE_LANE0 + EXPERTS_PER_GROUP * g_idx
    el = jnp.where((lane >= lo) & (lane < lo + EXPERTS_PER_GROUP), lg, neg)
    m1 = jnp.max(el, axis=-1, keepdims=True)
    i1 = jnp.min(jnp.where(el == m1, lane, big), axis=-1, keepdims=True)
    el2 = jnp.where(lane == i1, neg, el)
    m2 = jnp.max(el2, axis=-1, keepdims=True)
    i2 = jnp.min(jnp.where(el2 == m2, lane, big), axis=-1, keepdims=True)
    e2 = jnp.exp(m2 - m1)
    c1 = p_group / (1.0 + e2)
    c2 = p_group * e2 / (1.0 + e2)
    oh1 = jnp.where(lane == i1, 1.0, 0.0)
    oh2 = jnp.where(lane == i2, 1.0, 0.0)
    oh = (oh1 + oh2) * live
    r = lax.broadcasted_iota(I32, (tm, tm), 0)
    c = lax.broadcasted_iota(I32, (tm, tm), 1)
    stril = jnp.where(c < r, 1.0, 0.0).astype(BF16)
    before = _dot(stril, oh.astype(BF16)) + run_ref[0:1, :]
    rank1 = jnp.sum(before * oh1, axis=-1, keepdims=True)
    rank2 = jnp.sum(before * oh2, axis=-1, keepdims=True)
    run_ref[0:1, :] = run_ref[0:1, :] + jnp.sum(oh, axis=0, keepdims=True)
    cnt_ref[...] = run_ref[...]
    e1f = (i1 - ROUTE_LANE0).astype(F32)
    e2f = (i2 - ROUTE_LANE0).astype(F32)
    meta = jnp.where(lane == 0, e1f, 0.0)
    meta = jnp.where(lane == 1, e2f, meta)
    meta = jnp.where(lane == 2, rank1, meta)
    meta = jnp.where(lane == 3, rank2, meta)
    meta = jnp.where(lane == 4, c1, meta)
    meta = jnp.where(lane == 5, c2, meta)
    meta_ref[...] = meta
    metat_ref[0] = meta.T[0:8, :]

    x1 = (x_ref[...] + _dot(a_ref[...], w_ref[:RET_W, :]) + _dot(b_ref[...], w_ref[RET_W:RET_W + GLA_V, :])
          + _dot(c_ref[...], w_ref[RET_W + GLA_V:, :]))
    x1_ref[...] = x1
    stage_ref[...] = x1


def _out_route(x, o_ret, o_gla, o_att, w_all, g, wr_all, br_all, layer):
    t, d = x.shape
    tm = min(t, TOK_TILE)
    nt = t // tm
    cur = lambda w: pl.BlockSpec((tm, w), lambda i: (jnp.minimum(i, nt - 1), 0))
    prev = lambda i: jnp.maximum(i - 1, 0)
    return pl.pallas_call(
        _route_kernel,
        grid=(nt + 1,),
        in_specs=[
            cur(d), cur(RET_W), cur(GLA_V), cur(ATT_W),
            pl.BlockSpec((None, w_all.shape[1], d), lambda i: (layer, 0, 0), pipeline_mode=pl.Buffered(1)),
            pl.BlockSpec((1, d), lambda i: (0, 0)),
            pl.BlockSpec((None, d, LANES), lambda i: (layer, 0, 0)),
            pl.BlockSpec((None, 1, LANES), lambda i: (layer, 0, 0)),
        ],
        out_specs=[
            cur(d),
            pl.BlockSpec((tm * ROW_SUB, LANES), lambda i: (prev(i), 0)),
            pl.BlockSpec((tm, LANES), lambda i: (prev(i), 0)),
            pl.BlockSpec((1, 8, tm), lambda i: (prev(i), 0, 0)),
            pl.BlockSpec((8, LANES), lambda i: (0, 0)),
        ],
        out_shape=[
            jax.ShapeDtypeStruct((t, d), F32),
            jax.ShapeDtypeStruct((t * ROW_SUB, LANES), U32),
            jax.ShapeDtypeStruct((t, LANES), F32),
            jax.ShapeDtypeStruct((nt, 8, tm), F32),
            jax.ShapeDtypeStruct((8, LANES), F32),
        ],
        scratch_shapes=[pltpu.VMEM((tm, d), F32), pltpu.VMEM((8, LANES), F32)],
        compiler_params=_cparams(1),
        name="out_proj_router",
    )(x, o_ret, o_gla, o_att, w_all, g.reshape(1, d), wr_all, br_all)


_PAD_BITS = tuple(reversed(range(MOE_TILE.bit_length() - 1)))


def _dispatch_kernel(ps_ref, pn_ref, nu_ref, pos_ref, hp_ref, xs_ref, zero_ref, sem_ref, zsem_ref):
    i = pl.program_id(0)
    tm = pos_ref.shape[2] // 2
    zrows = zero_ref.shape[0] // ROW_SUB
    n_tiles = xs_ref.shape[0] // (ROW_SUB * MOE_TILE)

    def pad_copies(start):
        def zero_copy(off, k):
            cp = pltpu.make_async_copy(_tile_rows(zero_ref, 0, k), _tile_rows(xs_ref, off, k), zsem_ref.at[0])
            if start:
                cp.start()
            else:
                cp.wait()

        def body(e, carry):
            n = pn_ref[e]
            base = ps_ref[e]
            for bit in _PAD_BITS:
                k = 1 << bit

                @pl.when((n & k) != 0)
                def _():
                    zero_copy(base + ((n >> (bit + 1)) << (bit + 1)), k)
            return carry

        lax.fori_loop(0, N_EXPERTS, body, 0)

        def tail(tile, carry):
            for part in range(MOE_TILE // zrows):
                zero_copy(tile * MOE_TILE + part * zrows, zrows)
            return carry

        lax.fori_loop(nu_ref[0], n_tiles, tail, 0)

    @pl.when(i == 0)
    def _():
        zero_ref[...] = jnp.zeros_like(zero_ref)
        pad_copies(True)
        pad_copies(False)

    def row_copy(j):
        return pltpu.make_async_copy(_tile_rows(hp_ref, j % tm), _tile_rows(xs_ref, pos_ref[0, 0, j]),
                                     sem_ref.at[0])

    for j in range(2 * tm):
        row_copy(j).start(priority=j % 2)
    for j in range(2 * tm):
        row_copy(j).wait()


def _dispatch(hp, pos_tiles, pad_start, pad_n, n_used, n_rows):
    nt = pos_tiles.shape[0]
    tm = pos_tiles.shape[2] // 2
    grid_spec = pltpu.PrefetchScalarGridSpec(
        num_scalar_prefetch=3,
        grid=(nt,),
        in_specs=[
            pl.BlockSpec((1, 1, 2 * tm), lambda i, ps, pn, nu: (i, 0, 0), memory_space=pltpu.SMEM),
            pl.BlockSpec((tm * ROW_SUB, LANES), lambda i, ps, pn, nu: (i, 0)),
        ],
        out_specs=pl.BlockSpec(memory_space=pl.ANY),
        scratch_shapes=[
            pltpu.VMEM((MOE_TILE // 2 * ROW_SUB, LANES), U32),
            pltpu.SemaphoreType.DMA((1,)),
            pltpu.SemaphoreType.DMA((1,)),
        ],
    )
    return pl.pallas_call(
        _dispatch_kernel,
        grid_spec=grid_spec,
        out_shape=jax.ShapeDtypeStruct((n_rows * ROW_SUB, LANES), U32),
        compiler_params=_cparams(1),
        name="dispatch",
    )(pad_start, pad_n, n_used, pos_tiles, hp)


def _moe_kernel(te_ref, nu_ref, xs_ref, wg_ref, wu_ref, wd_ref, y_ref, wgb_ref, wub_ref, wdb_ref):
    i = pl.program_id(0)
    n_used = nu_ref[0]
    tm = xs_ref.shape[0] // ROW_SUB
    half = ROW_SUB * LANES
    e = te_ref[i]
    e_prev = te_ref[jnp.maximum(i - 1, 0)]

    @pl.when((i == 0) | (e != e_prev))
    def _():
        wgb_ref[...] = wg_ref[...].astype(BF16)
        wub_ref[...] = wu_ref[...].astype(BF16)
        wdb_ref[...] = wd_ref[...].astype(BF16)

    @pl.when(i < n_used)
    def _():
        lo, hi = _unpack_bf16_pair(_load_token_tiles(xs_ref, tm))
        xl = lo.astype(BF16)
        xh = hi.astype(BF16)
        a = _dot(xl, wgb_ref[:half, :]) + _dot(xh, wgb_ref[half:, :])
        u = _dot(xl, wub_ref[:half, :]) + _dot(xh, wub_ref[half:, :])
        hid = (_silu(a) * u).astype(BF16)
        y = _dot(hid, wdb_ref[...])
        _store_token_tiles(y_ref, _pack_bf16_pair(y[:, :half], y[:, half:]))

    @pl.when(i >= n_used)
    def _():
        y_ref[...] = jnp.zeros_like(y_ref)


def _moe(xs, tile_expert, n_used, wg, wu, wd, layer):
    nt = tile_expert.shape[0]
    tm = xs.shape[0] // ROW_SUB // nt
    d, ff = wg.shape[-2], wg.shape[-1]

    def w_spec(shape):
        return pl.BlockSpec((None, None, None) + shape,
                            lambda i, te, nu: (layer, te[i] // EXPERTS_PER_GROUP, te[i] % EXPERTS_PER_GROUP, 0, 0))

    grid_spec = pltpu.PrefetchScalarGridSpec(
        num_scalar_prefetch=2,
        grid=(nt,),
        in_specs=[
            pl.BlockSpec((tm * ROW_SUB, LANES), lambda i, te, nu: (jnp.minimum(i, nu[0] - 1), 0)),
            w_spec((d, ff)), w_spec((d, ff)), w_spec((ff, d)),
        ],
        out_specs=pl.BlockSpec((tm * ROW_SUB, LANES), lambda i, te, nu: (i, 0)),
        scratch_shapes=[
            pltpu.VMEM((d, ff), BF16),
            pltpu.VMEM((d, ff), BF16),
            pltpu.VMEM((ff, d), BF16),
        ],
    )
    return pl.pallas_call(
        _moe_kernel,
        grid_spec=grid_spec,
        out_shape=jax.ShapeDtypeStruct(xs.shape, U32),
        compiler_params=_cparams(1),
        name="expert_mlp",
    )(tile_expert, n_used, xs, wg, wu, wd)


def _ple_kernel(pos_ref, x_ref, meta_ref, y_ref, p_ref, g_ref, wg_ref, wp_ref, gf_ref, o_ref,
                buf_ref, sem_ref, *, final):
    i = pl.program_id(0)
    n = pl.num_programs(0) - 1
    tm = x_ref.shape[0]
    half = x_ref.shape[1] // 2

    for par in range(2):
        @pl.when((i < n) & (i % 2 == par))
        def _():
            for j in range(2 * tm):
                pltpu.make_async_copy(_tile_rows(y_ref, pos_ref[0, 0, j]), _tile_rows(buf_ref.at[par], j),
                                      sem_ref.at[par]).start(priority=j % 2)

    @pl.when(i > 0)
    def _():
        slot = (i + 1) % 2
        for j in range(2 * tm):
            pltpu.make_async_copy(_tile_rows(y_ref, 0), _tile_rows(buf_ref.at[slot], j), sem_ref.at[slot]).wait()
        rows = _load_token_tiles(buf_ref.at[slot], 2 * tm)
        meta = meta_ref[...]
        c1 = meta[:, 4:5]
        c2 = meta[:, 5:6]
        lo1, hi1 = _unpack_bf16_pair(rows[:tm])
        lo2, hi2 = _unpack_bf16_pair(rows[tm:])
        xl = x_ref[:, :half] + c1 * lo1 + c2 * lo2
        xh = x_ref[:, half:] + c1 * hi1 + c2 * hi2
        ms = (jnp.sum(xl * xl, axis=-1, keepdims=True) + jnp.sum(xh * xh, axis=-1, keepdims=True)) / (2 * half)
        inv = lax.rsqrt(ms + EPS)
        hl = (xl * inv * g_ref[:, :half]).astype(BF16)
        hh = (xh * inv * g_ref[:, half:]).astype(BF16)
        gate = _dot(hl, wg_ref[:half, :]) + _dot(hh, wg_ref[half:, :])
        gate = 1.0 / (1.0 + jnp.exp(-gate))
        pp = _dot(p_ref[...].astype(BF16), wp_ref[...])
        ol = xl + gate[:, :half] * pp[:, :half]
        oh = xh + gate[:, half:] * pp[:, half:]
        if final:
            ms2 = (jnp.sum(ol * ol, axis=-1, keepdims=True) + jnp.sum(oh * oh, axis=-1, keepdims=True)) / (2 * half)
            inv2 = lax.rsqrt(ms2 + EPS)
            ol = ol * inv2 * gf_ref[:, :half]
            oh = oh * inv2 * gf_ref[:, half:]
        o_ref[:, :half] = ol
        o_ref[:, half:] = oh


def _combine_ple(x, meta, pos_tiles, ys, p_all, g_ple, w_gate_all, w_proj_all, layer, g_final, final):
    t, d = x.shape
    nt = pos_tiles.shape[0]
    tm = pos_tiles.shape[2] // 2
    pd = p_all.shape[2]
    const = lambda shape: pl.BlockSpec(shape, lambda i: (0, 0))
    prev = lambda w: pl.BlockSpec((tm, w), lambda i: (jnp.maximum(i - 1, 0), 0))
    return pl.pallas_call(
        functools.partial(_ple_kernel, final=final),
        grid=(nt + 1,),
        in_specs=[
            pl.BlockSpec((1, 1, 2 * tm), lambda i: (jnp.minimum(i, nt - 1), 0, 0), memory_space=pltpu.SMEM),
            prev(d), prev(LANES),
            pl.BlockSpec(memory_space=pl.ANY),
            pl.BlockSpec((None, tm, pd), lambda i: (layer, jnp.maximum(i - 1, 0), 0)),
            const((1, d)),
            pl.BlockSpec((None, d, d), lambda i: (layer, 0, 0)),
            pl.BlockSpec((None, pd, d), lambda i: (layer, 0, 0)),
            const((1, d)),
        ],
        out_specs=prev(d),
        out_shape=jax.ShapeDtypeStruct((t, d), F32),
        scratch_shapes=[pltpu.VMEM((2, 2 * tm * ROW_SUB, LANES), U32), pltpu.SemaphoreType.DMA((2,))],
        compiler_params=_cparams(1),
        name="combine_ple",
    )(pos_tiles, x, meta, ys, p_all, g_ple.reshape(1, d), w_gate_all, w_proj_all, g_final.reshape(1, d))


W_GA0 = 4 * RET_W + 2 * GLA_QK + 2 * GLA_V


def _prep_kernel(w_ref, o_ref):
    rows = w_ref.shape[0]
    o_ref[:, :W_GA0] = w_ref[:, :W_GA0].astype(BF16)
    o_ref[:, W_GA0:Z_GA] = w_ref[:, W_GA0 + GLA_RANK:].astype(BF16)
    tail = jnp.concatenate([w_ref[:, W_GA0:W_GA0 + GLA_RANK],
                            jnp.zeros((rows, Z_W - Z_GA - GLA_RANK), F32)], axis=1)
    o_ref[:, Z_GA:] = tail.astype(BF16)


def _prep_w_in(w):
    depth, d, n = w.shape
    rows = min(d, 256)
    return pl.pallas_call(
        _prep_kernel,
        grid=(depth, d // rows),
        in_specs=[pl.BlockSpec((None, rows, n), lambda l, i: (l, i, 0))],
        out_specs=pl.BlockSpec((None, rows, Z_W), lambda l, i: (l, i, 0)),
        out_shape=jax.ShapeDtypeStruct((depth, d, Z_W), BF16),
        compiler_params=_cparams(2),
        name="w_in_layout",
    )(w)


def _pad_lanes(parts, width, dtype):
    cat = jnp.concatenate(parts, axis=-1)
    pad = jnp.zeros(cat.shape[:-1] + (width - cat.shape[-1],), cat.dtype)
    return jnp.concatenate([cat, pad], axis=-1).astype(dtype)


def _routing_tables(meta_t, counts, n_tok, tile):
    eid = meta_t[:, 0:2, :].astype(I32)
    rank = meta_t[:, 2:4, :].astype(I32)
    cnt = counts[0, ROUTE_LANE0:ROUTE_LANE0 + N_EXPERTS].astype(I32)
    padded = ((cnt + tile - 1) // tile) * tile
    ends = jnp.cumsum(padded)
    offs = ends - padded
    base = jnp.zeros_like(eid)
    for e in range(N_EXPERTS):
        base = jnp.where(eid == e, offs[e], base)
    pos = base + rank
    n_rows = 2 * n_tok + N_EXPERTS * tile
    nt = n_rows // tile
    tile_start = jnp.arange(nt, dtype=I32) * tile
    tile_expert = jnp.minimum(jnp.sum((tile_start[:, None] >= ends[None, :]).astype(I32), axis=1),
                              N_EXPERTS - 1)
    n_used = (ends[-1] // tile).reshape(1)
    return pos, tile_expert, n_used, offs + cnt, padded - cnt, n_rows


def kernel(x, p, g_mix, w_in, gla_w_alpha, gla_b_alpha, rel_bias, mix_scale, w_out, g_ffn,
           w_router_group, b_router_group, w_router_expert, b_router_expert,
           w_expert_gate, w_expert_up, w_expert_down, g_ple, w_ple_gate, w_ple_proj, g_final):
    batch, seq, d = x.shape
    depth = g_mix.shape[0]
    t = batch * seq
    assert seq % SUPER == 0 and d == 2 * ROW_SUB * LANES
    tok_tile = min(t, TOK_TILE)
    ple_tile = min(t, PLE_TILE)
    xf = x.reshape(t, d)
    cos2, sin2 = _rotary_tables(seq)
    w_in_r = _prep_w_in(w_in)
    w_out_b = w_out.astype(BF16)
    wa_all = jnp.concatenate(
        [gla_w_alpha, jnp.zeros((depth, LANES - GLA_RANK, GLA_QK), gla_w_alpha.dtype)], axis=1).astype(BF16)
    wr_all = _pad_lanes([w_router_group, w_router_expert.reshape(depth, d, N_EXPERTS)], LANES, BF16)
    br_all = _pad_lanes([b_router_group, b_router_expert.reshape(depth, N_EXPERTS)], LANES, F32)
    br_all = br_all.reshape(depth, 1, LANES)
    p_all = p.reshape(depth, t, p.shape[-1])
    w_pg_b = w_ple_gate.astype(BF16)
    w_pp_b = w_ple_proj.astype(BF16)
    bias_all = _att_bias(rel_bias)
    for i in range(depth):
        z = _in_proj(xf, g_mix[i], w_in_r, i)
        ms = mix_scale[i]
        o_ret = _retention(z, cos2, sin2, ms[:RET_W], batch, seq)
        o_gla = _gla(z, wa_all, i, gla_b_alpha[i], ms[RET_W:RET_W + GLA_V], batch, seq)
        o_att = _band_attention(z, bias_all, i, ms[RET_W + GLA_V:], batch, seq)
        x1, hp, meta, meta_t, counts = _out_route(xf, o_ret, o_gla, o_att, w_out_b, g_ffn[i], wr_all, br_all, i)
        pos, tile_expert, n_used, pad_start, pad_n, n_rows = _routing_tables(meta_t, counts, t, MOE_TILE)
        xs = _dispatch(hp, pos.reshape(t // tok_tile, 1, 2 * tok_tile), pad_start, pad_n, n_used, n_rows)
        ys = _moe(xs, tile_expert, n_used, w_expert_gate, w_expert_up, w_expert_down, i)
        r = tok_tile // ple_tile
        pos_ple = pos.reshape(t // tok_tile, 2, r, ple_tile).transpose(0, 2, 1, 3).reshape(t // ple_tile, 1, 2 * ple_tile)
        xf = _combine_ple(x1, meta, pos_ple, ys, p_all, g_ple[i], w_pg_b, w_pp_b, i, g_final,
                          final=(i == depth - 1))
    return xf.reshape(batch, seq, d)
```

```python
import functools
import math

import numpy as np
import jax
import jax.numpy as jnp
from jax import lax
from jax.experimental import pallas as pl
from jax.experimental.pallas import tpu as pltpu

F32 = jnp.float32
BF16 = jnp.bfloat16
U32 = jnp.uint32
I32 = jnp.int32

CHUNK = 64
HEAD_DIM = 128
RET_HEADS = 6
GLA_HEADS = 4
GLA_DK = 64
GLA_DV = 128
GLA_RANK = 16
GLA_TAU = 16.0
ATT_HEADS = 6
ATT_BAND_CHUNKS = 8
MAX_REL = 128
N_GROUPS = 4
EXPERTS_PER_GROUP = 8
N_EXPERTS = N_GROUPS * EXPERTS_PER_GROUP
EXPERT_FF = 256
EPS = 1e-6

RET_W = RET_HEADS * HEAD_DIM
GLA_QK = GLA_HEADS * GLA_DK
GLA_V = GLA_HEADS * GLA_DV
ATT_W = ATT_HEADS * HEAD_DIM

LANES = 128
V7X_VMEM_LIMIT = 56 * 1024 * 1024

Z_RET = 0
Z_GQ = 4 * RET_W
Z_GK = Z_GQ + GLA_QK
Z_GV = Z_GK + GLA_QK
Z_GG = Z_GV + GLA_V
Z_ATT = Z_GG + GLA_V
Z_GA = Z_ATT + 3 * ATT_W
Z_W = 7 * 1024

SUPER = 4 * CHUNK
ROUTE_LANE0 = N_GROUPS
MOE_TILE = 512
TOK_TILE = 512
PLE_TILE = 256
ROW_SUB = 8


def _cparams(n_axes):
    return pltpu.CompilerParams(
        dimension_semantics=("arbitrary",) * n_axes,
        vmem_limit_bytes=V7X_VMEM_LIMIT,
    )


def _dot(a, b):
    return jnp.dot(a, b, preferred_element_type=F32)


def _dot_nt(a, b):
    return lax.dot_general(a, b, (((1,), (1,)), ((), ())), preferred_element_type=F32)


def _dot_tn(a, b):
    return lax.dot_general(a, b, (((0,), (0,)), ((), ())), preferred_element_type=F32)


def _rms(x, g):
    return x * lax.rsqrt(jnp.mean(x * x, axis=-1, keepdims=True) + EPS) * g


def _silu(x):
    return x / (1.0 + jnp.exp(-x))


def _pack_bf16_pair(lo, hi):
    lo_b = lax.bitcast_convert_type(lo.astype(BF16).astype(F32), U32)
    hi_b = lax.bitcast_convert_type(hi.astype(BF16).astype(F32), U32)
    return (lo_b >> 16) | (hi_b & jnp.uint32(0xFFFF0000))


def _unpack_bf16_pair(w):
    lo = lax.bitcast_convert_type(w << 16, F32)
    hi = lax.bitcast_convert_type(w & jnp.uint32(0xFFFF0000), F32)
    return lo, hi


def _rot_kernel(inv_ref, cos_ref, sin_ref):
    rows = cos_ref.shape[0]
    pos = (lax.broadcasted_iota(I32, (rows, LANES), 0) + pl.program_id(0) * rows).astype(F32)
    lane = lax.broadcasted_iota(I32, (rows, LANES), 1)
    ang = pos * inv_ref[...]
    cos_ref[...] = jnp.cos(ang)
    s = jnp.sin(ang)
    sin_ref[...] = jnp.where(lane < HEAD_DIM // 2, -s, s)


def _rotary_tables(seq):
    half = HEAD_DIM // 2
    inv = np.float32(1.0) / (np.float32(10000.0) ** (np.arange(half, dtype=np.float32) / np.float32(half)))
    inv2 = jnp.asarray(np.concatenate([inv, inv]).reshape(1, LANES).astype(np.float32))
    rows = min(seq, 1024)
    return pl.pallas_call(
        _rot_kernel,
        grid=(seq // rows,),
        in_specs=[pl.BlockSpec((1, LANES), lambda i: (0, 0))],
        out_specs=[pl.BlockSpec((rows, LANES), lambda i: (i, 0))] * 2,
        out_shape=[jax.ShapeDtypeStruct((seq, LANES), F32)] * 2,
        compiler_params=_cparams(1),
        name="rotary_tables",
    )(inv2)


def _norm_kernel(x_ref, g_ref, h_ref):
    h_ref[...] = _rms(x_ref[...], g_ref[...]).astype(BF16)


def _norm_cast(x, g):
    t, d = x.shape
    tm = min(t, 512)
    return pl.pallas_call(
        _norm_kernel,
        grid=(t // tm,),
        in_specs=[pl.BlockSpec((tm, d), lambda i: (i, 0)), pl.BlockSpec((1, d), lambda i: (0, 0))],
        out_specs=pl.BlockSpec((tm, d), lambda i: (i, 0)),
        out_shape=jax.ShapeDtypeStruct((t, d), BF16),
        compiler_params=_cparams(1),
        name="norm_cast",
    )(x, g.reshape(1, d))


def _in_kernel(h_ref, w_ref, z_ref):
    z_ref[...] = _dot(h_ref[...], w_ref[...]).astype(BF16)


def _in_proj(h, w_all, layer):
    t, d = h.shape
    n = w_all.shape[2]
    tm = min(t, 2048)
    tn = min(n, 1024)
    return pl.pallas_call(
        _in_kernel,
        grid=(t // tm, n // tn),
        in_specs=[
            pl.BlockSpec((tm, d), lambda i, j: (i, 0)),
            pl.BlockSpec((None, d, tn), lambda i, j: (layer, 0, j)),
        ],
        out_specs=pl.BlockSpec((tm, tn), lambda i, j: (i, j)),
        out_shape=jax.ShapeDtypeStruct((t, n), BF16),
        compiler_params=_cparams(2),
        name="in_proj",
    )(h, w_all)


def _ret_log_gamma(h):
    return math.log(1.0 - 2.0 ** (-5.0 - h))


def _ret_kernel(q_ref, k_ref, v_ref, g_ref, cos_ref, sin_ref, ms_ref, o_ref,
                state_ref, dmat_ref, xi_ref, zeta_ref):
    s = pl.program_id(1)

    @pl.when(s == 0)
    def _():
        state_ref[...] = jnp.zeros_like(state_ref)
        row = lax.broadcasted_iota(I32, (SUPER, SUPER), 0)
        col = lax.broadcasted_iota(I32, (SUPER, SUPER), 1)
        dist = jnp.abs(row - col).astype(F32)
        keep = (col <= row) | ((row // CHUNK) == (col // CHUNK))
        t = lax.broadcasted_iota(I32, (SUPER, HEAD_DIM), 0).astype(F32)
        for h in range(RET_HEADS):
            lg = _ret_log_gamma(h)
            dmat_ref[h] = jnp.where(keep, jnp.exp(lg * dist), 0.0)
            xi_ref[h] = jnp.exp(lg * (t + 1.0))
            zeta_ref[h] = jnp.exp(lg * (SUPER - 1.0 - t))

    cos = cos_ref[...]
    sin = sin_ref[...]
    scale = HEAD_DIM ** -0.5
    for h in range(RET_HEADS):
        sl = slice(h * HEAD_DIM, (h + 1) * HEAD_DIM)
        q = q_ref[:, sl].astype(F32)
        k = k_ref[:, sl].astype(F32)
        vb = v_ref[:, sl]
        qr = q * cos + pltpu.roll(q, HEAD_DIM // 2, 1) * sin
        kr = (k * cos + pltpu.roll(k, HEAD_DIM // 2, 1) * sin) * scale
        qb = qr.astype(BF16)
        kb = kr.astype(BF16)
        sc = _dot_nt(qb, kb) * dmat_ref[h]
        intra = _dot(sc.astype(BF16), vb)
        st = state_ref[h]
        cross = _dot(qb, st.astype(BF16)) * xi_ref[h]
        kz = (kr * zeta_ref[h]).astype(BF16)
        upd = _dot_tn(kz, vb)
        state_ref[h] = math.exp(_ret_log_gamma(h) * SUPER) * st + upd
        o = intra + cross
        mu = jnp.mean(o, axis=-1, keepdims=True)
        oc = o - mu
        var = jnp.mean(oc * oc, axis=-1, keepdims=True)
        on = oc * lax.rsqrt(var + EPS)
        gate = _silu(g_ref[:, sl].astype(F32))
        o_ref[:, sl] = (on * gate * ms_ref[:, sl]).astype(BF16)


def _retention(z, cos2, sin2, ms, batch, seq):
    t = z.shape[0]
    ns = seq // SUPER
    zspec = lambda c: pl.BlockSpec((SUPER, RET_W), lambda b, s, c=c: (b * ns + s, c))
    return pl.pallas_call(
        _ret_kernel,
        grid=(batch, ns),
        in_specs=[
            zspec(0), zspec(1), zspec(2), zspec(3),
            pl.BlockSpec((SUPER, LANES), lambda b, s: (s, 0)),
            pl.BlockSpec((SUPER, LANES), lambda b, s: (s, 0)),
            pl.BlockSpec((1, RET_W), lambda b, s: (0, 0)),
        ],
        out_specs=pl.BlockSpec((SUPER, RET_W), lambda b, s: (b * ns + s, 0)),
        out_shape=jax.ShapeDtypeStruct((t, RET_W), BF16),
        scratch_shapes=[
            pltpu.VMEM((RET_HEADS, HEAD_DIM, HEAD_DIM), F32),
            pltpu.VMEM((RET_HEADS, SUPER, SUPER), F32),
            pltpu.VMEM((RET_HEADS, SUPER, HEAD_DIM), F32),
            pltpu.VMEM((RET_HEADS, SUPER, HEAD_DIM), F32),
        ],
        compiler_params=_cparams(2),
        name="retention",
    )(z, z, z, z, cos2, sin2, ms.reshape(1, RET_W))


def _gla_kernel(q_ref, k_ref, v_ref, g_ref, a_ref, wa_ref, ba_ref, ms_ref, o_ref, state_ref):
    s = pl.program_id(0)
    nb = q_ref.shape[0]

    @pl.when(s == 0)
    def _():
        state_ref[...] = jnp.zeros_like(state_ref)

    r = lax.broadcasted_iota(I32, (CHUNK, CHUNK), 0)
    c = lax.broadcasted_iota(I32, (CHUNK, CHUNK), 1)
    causal = c <= r
    tril = jnp.where(causal, 1.0, 0.0).astype(BF16)
    ones = jnp.ones((CHUNK, GLA_DV), BF16)
    scale = GLA_DK ** -0.5
    las = []
    for b in range(nb):
        pre = _dot(a_ref[b], wa_ref[...]) + ba_ref[...]
        las.append((jnp.minimum(pre, 0.0) - jnp.log(1.0 + jnp.exp(-jnp.abs(pre)))) * (1.0 / GLA_TAU))
    for ci in range(SUPER // CHUNK):
        rows = slice(ci * CHUNK, (ci + 1) * CHUNK)
        for b in range(nb):
            la_c = las[b][rows, :]
            hi = la_c.astype(BF16)
            lo = (la_c - hi.astype(F32)).astype(BF16)
            bcs = _dot(tril, hi) + _dot(tril, lo)
            gl = _dot_tn(hi, ones) + _dot_tn(lo, ones)
            blast = bcs[CHUNK - 1:CHUNK, :]
            eb = jnp.exp(bcs)
            enb = jnp.exp(-bcs)
            ekb = jnp.exp(blast - bcs)
            qc = q_ref[b, rows, :].astype(F32) * scale * eb
            kc = k_ref[b, rows, :].astype(F32)
            kin = (kc * enb).astype(BF16)
            kup = (kc * ekb).astype(BF16)
            qin = qc.astype(BF16)
            for h in range(GLA_HEADS):
                ks = slice(h * GLA_DK, (h + 1) * GLA_DK)
                vs = slice(h * GLA_DV, (h + 1) * GLA_DV)
                vb = v_ref[b, rows, vs]
                a = jnp.where(causal, _dot_nt(qin[:, ks], kin[:, ks]), 0.0)
                intra = _dot(a.astype(BF16), vb)
                st = state_ref[b, h]
                cross = _dot(qin[:, ks], st.astype(BF16))
                upd = _dot_tn(kup[:, ks], vb)
                state_ref[b, h] = jnp.exp(gl[ks, :]) * st + upd
                o = intra + cross
                on = o * lax.rsqrt(jnp.mean(o * o, axis=-1, keepdims=True) + EPS)
                gate = _silu(g_ref[b, rows, vs].astype(F32))
                o_ref[b, rows, vs] = (on * gate * ms_ref[:, vs]).astype(BF16)


def _gla(z, wa_all, layer, ba, ms, batch, seq):
    t = z.shape[0]
    ns = seq // SUPER
    z3 = z.reshape(batch, seq, z.shape[1])
    zspec = lambda width, off: pl.BlockSpec((batch, SUPER, width), lambda s: (0, s, off // width))
    out = pl.pallas_call(
        _gla_kernel,
        grid=(ns,),
        in_specs=[
            zspec(GLA_QK, Z_GQ), zspec(GLA_QK, Z_GK), zspec(GLA_V, Z_GV), zspec(GLA_V, Z_GG),
            zspec(LANES, Z_GA),
            pl.BlockSpec((None, LANES, GLA_QK), lambda s: (layer, 0, 0)),
            pl.BlockSpec((1, GLA_QK), lambda s: (0, 0)),
            pl.BlockSpec((1, GLA_V), lambda s: (0, 0)),
        ],
        out_specs=pl.BlockSpec((batch, SUPER, GLA_V), lambda s: (0, s, 0)),
        out_shape=jax.ShapeDtypeStruct((batch, seq, GLA_V), BF16),
        scratch_shapes=[pltpu.VMEM((batch, GLA_HEADS, GLA_DK, GLA_DV), F32)],
        compiler_params=_cparams(1),
        name="gla",
    )(z3, z3, z3, z3, z3, wa_all, ba.reshape(1, GLA_QK), ms.reshape(1, GLA_V))
    return out.reshape(t, GLA_V)


BAND = 3 * SUPER


BAND_KINDS = 3
ROLL_W = 1024
NEG_INF = -1e30


def _bias_kernel(rb_ref, o_ref):
    l = pl.program_id(0)
    h = pl.program_id(1)
    m = lax.broadcasted_iota(I32, (8, ROLL_W), 1)
    d = jnp.where(m < BAND, m, m - ROLL_W)
    idx = jnp.clip(2 * SUPER - d, -MAX_REL, MAX_REL) + MAX_REL

    def body(j, acc):
        return jnp.where(idx == j, rb_ref[l, h, j], acc)

    diag = lax.fori_loop(0, 2 * MAX_REL + 1, body, jnp.zeros((8, ROLL_W), F32))
    full = pltpu.roll(jnp.broadcast_to(diag[0:1, :], (SUPER, ROLL_W)), 0, 1, stride=1, stride_axis=0)
    bias = full[:, :BAND]
    row = lax.broadcasted_iota(I32, (SUPER, BAND), 0)
    col = lax.broadcasted_iota(I32, (SUPER, BAND), 1)
    dist = 2 * (SUPER // CHUNK) + row // CHUNK - col // CHUNK
    in_band = (dist >= 0) & (dist <= ATT_BAND_CHUNKS)
    for kind in range(BAND_KINDS):
        first_valid_col = (BAND_KINDS - 1 - kind) * SUPER
        o_ref[kind, 0] = jnp.where(in_band & (col >= first_valid_col), bias, NEG_INF)


def _att_bias(rel_bias):
    depth = rel_bias.shape[0]
    return pl.pallas_call(
        _bias_kernel,
        grid=(depth, ATT_HEADS),
        in_specs=[pl.BlockSpec(memory_space=pltpu.SMEM)],
        out_specs=pl.BlockSpec((None, BAND_KINDS, 1, SUPER, BAND), lambda l, h: (l, 0, h, 0, 0)),
        out_shape=jax.ShapeDtypeStruct((depth, BAND_KINDS, ATT_HEADS, SUPER, BAND), F32),
        compiler_params=_cparams(2),
        name="att_bias",
    )(rel_bias)


def _att_kernel(q_ref, k0_ref, k1_ref, k2_ref, v0_ref, v1_ref, v2_ref, bias_ref, ms_ref, o_ref):
    k_refs = (k0_ref, k1_ref, k2_ref)
    v_refs = (v0_ref, v1_ref, v2_ref)
    scale = HEAD_DIM ** -0.5
    for h in range(ATT_HEADS):
        sl = slice(h * HEAD_DIM, (h + 1) * HEAD_DIM)
        qb = (q_ref[:, sl].astype(F32) * scale).astype(BF16)
        sc = [_dot_nt(qb, k_refs[j][:, sl]) + bias_ref[h, :, j * SUPER:(j + 1) * SUPER] for j in range(3)]
        m = jnp.max(jnp.maximum(jnp.maximum(sc[0], sc[1]), sc[2]), axis=-1, keepdims=True)
        ps = [jnp.exp(sj - m) for sj in sc]
        den = jnp.sum(ps[0] + ps[1] + ps[2], axis=-1, keepdims=True)
        acc = _dot(ps[0].astype(BF16), v_refs[0][:, sl])
        for j in (1, 2):
            acc = acc + _dot(ps[j].astype(BF16), v_refs[j][:, sl])
        o = acc / den
        on = o * lax.rsqrt(jnp.mean(o * o, axis=-1, keepdims=True) + EPS)
        o_ref[:, sl] = (on * ms_ref[:, sl]).astype(BF16)


def _band_attention(z, bias, layer, ms, batch, seq):
    t = z.shape[0]
    ns = seq // SUPER
    qc, kc, vc = Z_ATT // ATT_W, Z_ATT // ATT_W + 1, Z_ATT // ATT_W + 2

    def kv_spec(col, back):
        return pl.BlockSpec((SUPER, ATT_W), lambda b, s: (b * ns + jnp.maximum(s - back, 0), col))

    return pl.pallas_call(
        _att_kernel,
        grid=(batch, ns),
        in_specs=[
            pl.BlockSpec((SUPER, ATT_W), lambda b, s: (b * ns + s, qc)),
            kv_spec(kc, 2), kv_spec(kc, 1), kv_spec(kc, 0),
            kv_spec(vc, 2), kv_spec(vc, 1), kv_spec(vc, 0),
            pl.BlockSpec((None, None, ATT_HEADS, SUPER, BAND),
                         lambda b, s: (layer, jnp.minimum(s, BAND_KINDS - 1), 0, 0, 0)),
            pl.BlockSpec((1, ATT_W), lambda b, s: (0, 0)),
        ],
        out_specs=pl.BlockSpec((SUPER, ATT_W), lambda b, s: (b * ns + s, 0)),
        out_shape=jax.ShapeDtypeStruct((t, ATT_W), BF16),
        compiler_params=_cparams(2),
        name="band_attention",
    )(z, z, z, z, z, z, z, bias, ms.reshape(1, ATT_W))


def _store_token_tiles(ref, packed):
    tm = packed.shape[0]
    for s in range(ROW_SUB):
        ref[pl.ds(s, tm, stride=ROW_SUB), :] = packed[:, s * LANES:(s + 1) * LANES]


def _load_token_tiles(ref, tm):
    return jnp.concatenate([ref[pl.ds(s, tm, stride=ROW_SUB), :] for s in range(ROW_SUB)], axis=1)


def _tile_rows(ref, row, n_rows=1):
    start = row * ROW_SUB
    if not isinstance(start, int):
        start = pl.multiple_of(start, ROW_SUB)
    return ref.at[pl.ds(start, n_rows * ROW_SUB), :]


def _route_kernel(x_ref, a_ref, b_ref, c_ref, w_ref, g_ref, wr_ref, br_ref,
                  x1_ref, hp_ref, meta_ref, metat_ref, cnt_ref, stage_ref, run_ref):
    i = pl.program_id(0)
    tm = x_ref.shape[0]
    half = x_ref.shape[1] // 2

    @pl.when(i == 0)
    def _():
        run_ref[...] = jnp.zeros_like(run_ref)
        stage_ref[...] = jnp.zeros_like(stage_ref)

    xr = stage_ref[...]
    live = jnp.where(i > 0, 1.0, 0.0)

    h = _rms(xr, g_ref[...])
    _store_token_tiles(hp_ref, _pack_bf16_pair(h[:, :half], h[:, half:]))
    lg = _dot(h.astype(BF16), wr_ref[...]) + br_ref[...]
    lane = lax.broadcasted_iota(I32, (tm, LANES), 1)
    neg = jnp.float32(-1e30)
    big = jnp.int32(LANES)
    gl = jnp.where(lane < N_GROUPS, lg, neg)
    gm = jnp.max(gl, axis=-1, keepdims=True)
    p_group = 1.0 / jnp.sum(jnp.exp(gl - gm), axis=-1, keepdims=True)
    g_idx = jnp.min(jnp.where(gl == gm, lane, big), axis=-1, keepdims=True)
    lo = ROUTE_LANE0 + EXPERTS_PER_GROUP * g_idx
    el = jnp.where((lane >= lo) & (lane < lo + EXPERTS_PER_GROUP), lg, neg)
    m1 = jnp.max(el, axis=-1, keepdims=True)
    i1 = jnp.min(jnp.where(el == m1, lane, big), axis=-1, keepdims=True)
    el2 = jnp.where(lane == i1, neg, el)
    m2 = jnp.max(el2, axis=-1, keepdims=True)
    i2 = jnp.min(jnp.where(el2 == m2, lane, big), axis=-1, keepdims=True)
    e2 = jnp.exp(m2 - m1)
    c1 = p_group / (1.0 + e2)
    c2 = p_group * e2 / (1.0 + e2)
    oh1 = jnp.where(lane == i1, 1.0, 0.0)
    oh2 = jnp.where(lane == i2, 1.0, 0.0)
    oh = (oh1 + oh2) * live
    r = lax.broadcasted_iota(I32, (tm, tm), 0)
    c = lax.broadcasted_iota(I32, (tm, tm), 1)
    stril = jnp.where(c < r, 1.0, 0.0).astype(BF16)
    before = _dot(stril, oh.astype(BF16)) + run_ref[0:1, :]
    rank1 = jnp.sum(before * oh1, axis=-1, keepdims=True)
    rank2 = jnp.sum(before * oh2, axis=-1, keepdims=True)
    run_ref[0:1, :] = run_ref[0:1, :] + jnp.sum(oh, axis=0, keepdims=True)
    cnt_ref[...] = run_ref[...]
    e1f = (i1 - ROUTE_LANE0).astype(F32)
    e2f = (i2 - ROUTE_LANE0).astype(F32)
    meta = jnp.where(lane == 0, e1f, 0.0)
    meta = jnp.where(lane == 1, e2f, meta)
    meta = jnp.where(lane == 2, rank1, meta)
    meta = jnp.where(lane == 3, rank2, meta)
    meta = jnp.where(lane == 4, c1, meta)
    meta = jnp.where(lane == 5, c2, meta)
    meta_ref[...] = meta
    metat_ref[0] = meta.T[0:8, :]

    x1 = (x_ref[...] + _dot(a_ref[...], w_ref[:RET_W, :]) + _dot(b_ref[...], w_ref[RET_W:RET_W + GLA_V, :])
          + _dot(c_ref[...], w_ref[RET_W + GLA_V:, :]))
    x1_ref[...] = x1
    stage_ref[...] = x1


def _out_route(x, o_ret, o_gla, o_att, w_all, g, wr_all, br_all, layer):
    t, d = x.shape
    tm = min(t, TOK_TILE)
    nt = t // tm
    cur = lambda w: pl.BlockSpec((tm, w), lambda i: (jnp.minimum(i, nt - 1), 0))
    prev = lambda i: jnp.maximum(i - 1, 0)
    return pl.pallas_call(
        _route_kernel,
        grid=(nt + 1,),
        in_specs=[
            cur(d), cur(RET_W), cur(GLA_V), cur(ATT_W),
            pl.BlockSpec((None, w_all.shape[1], d), lambda i: (layer, 0, 0), pipeline_mode=pl.Buffered(1)),
            pl.BlockSpec((1, d), lambda i: (0, 0)),
            pl.BlockSpec((None, d, LANES), lambda i: (layer, 0, 0)),
            pl.BlockSpec((None, 1, LANES), lambda i: (layer, 0, 0)),
        ],
        out_specs=[
            cur(d),
            pl.BlockSpec((tm * ROW_SUB, LANES), lambda i: (prev(i), 0)),
            pl.BlockSpec((tm, LANES), lambda i: (prev(i), 0)),
            pl.BlockSpec((1, 8, tm), lambda i: (prev(i), 0, 0)),
            pl.BlockSpec((8, LANES), lambda i: (0, 0)),
        ],
        out_shape=[
            jax.ShapeDtypeStruct((t, d), F32),
            jax.ShapeDtypeStruct((t * ROW_SUB, LANES), U32),
            jax.ShapeDtypeStruct((t, LANES), F32),
            jax.ShapeDtypeStruct((nt, 8, tm), F32),
            jax.ShapeDtypeStruct((8, LANES), F32),
        ],
        scratch_shapes=[pltpu.VMEM((tm, d), F32), pltpu.VMEM((8, LANES), F32)],
        compiler_params=_cparams(1),
        name="out_proj_router",
    )(x, o_ret, o_gla, o_att, w_all, g.reshape(1, d), wr_all, br_all)


_PAD_BITS = tuple(reversed(range(MOE_TILE.bit_length() - 1)))


def _dispatch_kernel(ps_ref, pn_ref, nu_ref, pos_ref, hp_ref, xs_ref, zero_ref, sem_ref, zsem_ref):
    i = pl.program_id(0)
    tm = pos_ref.shape[2] // 2
    zrows = zero_ref.shape[0] // ROW_SUB
    n_tiles = xs_ref.shape[0] // (ROW_SUB * MOE_TILE)

    def pad_copies(start):
        def zero_copy(off, k):
            cp = pltpu.make_async_copy(_tile_rows(zero_ref, 0, k), _tile_rows(xs_ref, off, k), zsem_ref.at[0])
            if start:
                cp.start()
            else:
                cp.wait()

        def body(e, carry):
            n = pn_ref[e]
            base = ps_ref[e]
            for bit in _PAD_BITS:
                k = 1 << bit

                @pl.when((n & k) != 0)
                def _():
                    zero_copy(base + ((n >> (bit + 1)) << (bit + 1)), k)
            return carry

        lax.fori_loop(0, N_EXPERTS, body, 0)

        def tail(tile, carry):
            for part in range(MOE_TILE // zrows):
                zero_copy(tile * MOE_TILE + part * zrows, zrows)
            return carry

        lax.fori_loop(nu_ref[0], n_tiles, tail, 0)

    @pl.when(i == 0)
    def _():
        zero_ref[...] = jnp.zeros_like(zero_ref)
        pad_copies(True)
        pad_copies(False)

    def row_copy(j):
        return pltpu.make_async_copy(_tile_rows(hp_ref, j % tm), _tile_rows(xs_ref, pos_ref[0, 0, j]),
                                     sem_ref.at[0])

    for j in range(2 * tm):
        row_copy(j).start(priority=j % 2)
    for j in range(2 * tm):
        row_copy(j).wait()


def _dispatch(hp, pos_tiles, pad_start, pad_n, n_used, n_rows):
    nt = pos_tiles.shape[0]
    tm = pos_tiles.shape[2] // 2
    grid_spec = pltpu.PrefetchScalarGridSpec(
        num_scalar_prefetch=3,
        grid=(nt,),
        in_specs=[
            pl.BlockSpec((1, 1, 2 * tm), lambda i, ps, pn, nu: (i, 0, 0), memory_space=pltpu.SMEM),
            pl.BlockSpec((tm * ROW_SUB, LANES), lambda i, ps, pn, nu: (i, 0)),
        ],
        out_specs=pl.BlockSpec(memory_space=pl.ANY),
        scratch_shapes=[
            pltpu.VMEM((MOE_TILE // 2 * ROW_SUB, LANES), U32),
            pltpu.SemaphoreType.DMA((1,)),
            pltpu.SemaphoreType.DMA((1,)),
        ],
    )
    return pl.pallas_call(
        _dispatch_kernel,
        grid_spec=grid_spec,
        out_shape=jax.ShapeDtypeStruct((n_rows * ROW_SUB, LANES), U32),
        compiler_params=_cparams(1),
        name="dispatch",
    )(pad_start, pad_n, n_used, pos_tiles, hp)


def _moe_kernel(te_ref, nu_ref, xs_ref, wg_ref, wu_ref, wd_ref, y_ref, wgb_ref, wub_ref, wdb_ref):
    i = pl.program_id(0)
    n_used = nu_ref[0]
    tm = xs_ref.shape[0] // ROW_SUB
    half = ROW_SUB * LANES
    e = te_ref[i]
    e_prev = te_ref[jnp.maximum(i - 1, 0)]

    @pl.when((i == 0) | (e != e_prev))
    def _():
        wgb_ref[...] = wg_ref[...].astype(BF16)
        wub_ref[...] = wu_ref[...].astype(BF16)
        wdb_ref[...] = wd_ref[...].astype(BF16)

    @pl.when(i < n_used)
    def _():
        lo, hi = _unpack_bf16_pair(_load_token_tiles(xs_ref, tm))
        xl = lo.astype(BF16)
        xh = hi.astype(BF16)
        a = _dot(xl, wgb_ref[:half, :]) + _dot(xh, wgb_ref[half:, :])
        u = _dot(xl, wub_ref[:half, :]) + _dot(xh, wub_ref[half:, :])
        hid = (_silu(a) * u).astype(BF16)
        y = _dot(hid, wdb_ref[...])
        _store_token_tiles(y_ref, _pack_bf16_pair(y[:, :half], y[:, half:]))

    @pl.when(i >= n_used)
    def _():
        y_ref[...] = jnp.zeros_like(y_ref)


def _moe(xs, tile_expert, n_used, wg, wu, wd, layer):
    nt = tile_expert.shape[0]
    tm = xs.shape[0] // ROW_SUB // nt
    d, ff = wg.shape[-2], wg.shape[-1]

    def w_spec(shape):
        return pl.BlockSpec((None, None, None) + shape,
                            lambda i, te, nu: (layer, te[i] // EXPERTS_PER_GROUP, te[i] % EXPERTS_PER_GROUP, 0, 0))

    grid_spec = pltpu.PrefetchScalarGridSpec(
        num_scalar_prefetch=2,
        grid=(nt,),
        in_specs=[
            pl.BlockSpec((tm * ROW_SUB, LANES), lambda i, te, nu: (jnp.minimum(i, nu[0] - 1), 0)),
            w_spec((d, ff)), w_spec((d, ff)), w_spec((ff, d)),
        ],
        out_specs=pl.BlockSpec((tm * ROW_SUB, LANES), lambda i, te, nu: (i, 0)),
        scratch_shapes=[
            pltpu.VMEM((d, ff), BF16),
            pltpu.VMEM((d, ff), BF16),
            pltpu.VMEM((ff, d), BF16),
        ],
    )
    return pl.pallas_call(
        _moe_kernel,
        grid_spec=grid_spec,
        out_shape=jax.ShapeDtypeStruct(xs.shape, U32),
        compiler_params=_cparams(1),
        name="expert_mlp",
    )(tile_expert, n_used, xs, wg, wu, wd)


def _ple_kernel(pos_ref, x_ref, meta_ref, y_ref, p_ref, g_ref, wg_ref, wp_ref, gn_ref, *rest, final):
    if final:
        o_ref, buf_ref, sem_ref = rest
    else:
        o_ref, hn_ref, buf_ref, sem_ref = rest
    i = pl.program_id(0)
    n = pl.num_programs(0) - 1
    tm = x_ref.shape[0]
    half = x_ref.shape[1] // 2

    for par in range(2):
        @pl.when((i < n) & (i % 2 == par))
        def _():
            for j in range(2 * tm):
                pltpu.make_async_copy(_tile_rows(y_ref, pos_ref[0, 0, j]), _tile_rows(buf_ref.at[par], j),
                                      sem_ref.at[par]).start(priority=j % 2)

    @pl.when(i > 0)
    def _():
        slot = (i + 1) % 2
        for j in range(2 * tm):
            pltpu.make_async_copy(_tile_rows(y_ref, 0), _tile_rows(buf_ref.at[slot], j), sem_ref.at[slot]).wait()
        rows = _load_token_tiles(buf_ref.at[slot], 2 * tm)
        meta = meta_ref[...]
        c1 = meta[:, 4:5]
        c2 = meta[:, 5:6]
        lo1, hi1 = _unpack_bf16_pair(rows[:tm])
        lo2, hi2 = _unpack_bf16_pair(rows[tm:])
        xl = x_ref[:, :half] + c1 * lo1 + c2 * lo2
        xh = x_ref[:, half:] + c1 * hi1 + c2 * hi2
        ms = (jnp.sum(xl * xl, axis=-1, keepdims=True) + jnp.sum(xh * xh, axis=-1, keepdims=True)) / (2 * half)
        inv = lax.rsqrt(ms + EPS)
        hl = (xl * inv * g_ref[:, :half]).astype(BF16)
        hh = (xh * inv * g_ref[:, half:]).astype(BF16)
        gate = _dot(hl, wg_ref[:half, :]) + _dot(hh, wg_ref[half:, :])
        gate = 1.0 / (1.0 + jnp.exp(-gate))
        pp = _dot(p_ref[...].astype(BF16), wp_ref[...])
        ol = xl + gate[:, :half] * pp[:, :half]
        oh = xh + gate[:, half:] * pp[:, half:]
        ms2 = (jnp.sum(ol * ol, axis=-1, keepdims=True) + jnp.sum(oh * oh, axis=-1, keepdims=True)) / (2 * half)
        inv2 = lax.rsqrt(ms2 + EPS)
        nl = ol * inv2 * gn_ref[:, :half]
        nh = oh * inv2 * gn_ref[:, half:]
        if final:
            o_ref[:, :half] = nl
            o_ref[:, half:] = nh
        else:
            o_ref[:, :half] = ol
            o_ref[:, half:] = oh
            hn_ref[:, :half] = nl.astype(BF16)
            hn_ref[:, half:] = nh.astype(BF16)


def _combine_ple(x, meta, pos_tiles, ys, p_all, g_ple, w_gate_all, w_proj_all, layer, g_next, final):
    t, d = x.shape
    nt = pos_tiles.shape[0]
    tm = pos_tiles.shape[2] // 2
    pd = p_all.shape[2]
    const = lambda shape: pl.BlockSpec(shape, lambda i: (0, 0))
    prev = lambda w: pl.BlockSpec((tm, w), lambda i: (jnp.maximum(i - 1, 0), 0))
    return pl.pallas_call(
        functools.partial(_ple_kernel, final=final),
        grid=(nt + 1,),
        in_specs=[
            pl.BlockSpec((1, 1, 2 * tm), lambda i: (jnp.minimum(i, nt - 1), 0, 0), memory_space=pltpu.SMEM),
            prev(d), prev(LANES),
            pl.BlockSpec(memory_space=pl.ANY),
            pl.BlockSpec((None, tm, pd), lambda i: (layer, jnp.maximum(i - 1, 0), 0)),
            const((1, d)),
            pl.BlockSpec((None, d, d), lambda i: (layer, 0, 0)),
            pl.BlockSpec((None, pd, d), lambda i: (layer, 0, 0)),
            const((1, d)),
        ],
        out_specs=[prev(d)] if final else [prev(d), prev(d)],
        out_shape=[jax.ShapeDtypeStruct((t, d), F32)] + ([] if final else [jax.ShapeDtypeStruct((t, d), BF16)]),
        scratch_shapes=[pltpu.VMEM((2, 2 * tm * ROW_SUB, LANES), U32), pltpu.SemaphoreType.DMA((2,))],
        compiler_params=_cparams(1),
        name="combine_ple",
    )(pos_tiles, x, meta, ys, p_all, g_ple.reshape(1, d), w_gate_all, w_proj_all, g_next.reshape(1, d))


W_GA0 = 4 * RET_W + 2 * GLA_QK + 2 * GLA_V


PREP_COLS = 256
PREP_ALIGNED = W_GA0 // PREP_COLS
PREP_SHIFTED = (Z_GA - W_GA0) // PREP_COLS


def _prep_kernel(wt_ref, o_ref):
    j = pl.program_id(1)
    t = wt_ref[0].T

    @pl.when(j < PREP_ALIGNED + PREP_SHIFTED)
    def _():
        o_ref[...] = t.astype(BF16)

    @pl.when(j >= PREP_ALIGNED + PREP_SHIFTED)
    def _():
        lane = lax.broadcasted_iota(I32, t.shape, 1)
        o_ref[...] = jnp.where(lane < GLA_RANK, t, 0.0).astype(BF16)


def _prep_w_in(w):
    depth, d, n = w.shape
    wt = jnp.transpose(w, (0, 2, 1))

    def src_row(j):
        shifted = (W_GA0 + GLA_RANK) // 8 + (j - PREP_ALIGNED) * (PREP_COLS // 8)
        r8 = jnp.where(j < PREP_ALIGNED, j * (PREP_COLS // 8),
                       jnp.where(j < PREP_ALIGNED + PREP_SHIFTED, shifted, W_GA0 // 8))
        return r8 * 8

    return pl.pallas_call(
        _prep_kernel,
        grid=(depth, Z_W // PREP_COLS),
        in_specs=[pl.BlockSpec((pl.Element(1), pl.Element(PREP_COLS), pl.Element(d)),
                               lambda l, j: (l, src_row(j), 0))],
        out_specs=pl.BlockSpec((None, d, PREP_COLS), lambda l, j: (l, 0, j)),
        out_shape=jax.ShapeDtypeStruct((depth, d, Z_W), BF16),
        compiler_params=_cparams(2),
        name="w_in_layout",
    )(wt)


def _pad_lanes(parts, width, dtype):
    cat = jnp.concatenate(parts, axis=-1)
    pad = jnp.zeros(cat.shape[:-1] + (width - cat.shape[-1],), cat.dtype)
    return jnp.concatenate([cat, pad], axis=-1).astype(dtype)


def _routing_tables(meta_t, counts, n_tok, tile):
    eid = meta_t[:, 0:2, :].astype(I32)
    rank = meta_t[:, 2:4, :].astype(I32)
    cnt = counts[0, ROUTE_LANE0:ROUTE_LANE0 + N_EXPERTS].astype(I32)
    padded = ((cnt + tile - 1) // tile) * tile
    ends = jnp.cumsum(padded)
    offs = ends - padded
    base = jnp.zeros_like(eid)
    for e in range(N_EXPERTS):
        base = jnp.where(eid == e, offs[e], base)
    pos = base + rank
    n_rows = 2 * n_tok + N_EXPERTS * tile
    nt = n_rows // tile
    tile_start = jnp.arange(nt, dtype=I32) * tile
    tile_expert = jnp.minimum(jnp.sum((tile_start[:, None] >= ends[None, :]).astype(I32), axis=1),
                              N_EXPERTS - 1)
    n_used = (ends[-1] // tile).reshape(1)
    return pos, tile_expert, n_used, offs + cnt, padded - cnt, n_rows


def kernel(x, p, g_mix, w_in, gla_w_alpha, gla_b_alpha, rel_bias, mix_scale, w_out, g_ffn,
           w_router_group, b_router_group, w_router_expert, b_router_expert,
           w_expert_gate, w_expert_up, w_expert_down, g_ple, w_ple_gate, w_ple_proj, g_final):
    batch, seq, d = x.shape
    depth = g_mix.shape[0]
    t = batch * seq
    assert seq % SUPER == 0 and d == 2 * ROW_SUB * LANES
    tok_tile = min(t, TOK_TILE)
    ple_tile = min(t, PLE_TILE)
    xf = x.reshape(t, d)
    cos2, sin2 = _rotary_tables(seq)
    w_in_r = _prep_w_in(w_in)
    w_out_b = w_out.astype(BF16)
    wa_all = jnp.concatenate(
        [gla_w_alpha, jnp.zeros((depth, LANES - GLA_RANK, GLA_QK), gla_w_alpha.dtype)], axis=1).astype(BF16)
    wr_all = _pad_lanes([w_router_group, w_router_expert.reshape(depth, d, N_EXPERTS)], LANES, BF16)
    br_all = _pad_lanes([b_router_group, b_router_expert.reshape(depth, N_EXPERTS)], LANES, F32)
    br_all = br_all.reshape(depth, 1, LANES)
    p_all = p.reshape(depth, t, p.shape[-1])
    w_pg_b = w_ple_gate.astype(BF16)
    w_pp_b = w_ple_proj.astype(BF16)
    bias_all = _att_bias(rel_bias)
    h = _norm_cast(xf, g_mix[0])
    for i in range(depth):
        z = _in_proj(h, w_in_r, i)
        ms = mix_scale[i]
        o_ret = _retention(z, cos2, sin2, ms[:RET_W], batch, seq)
        o_gla = _gla(z, wa_all, i, gla_b_alpha[i], ms[RET_W:RET_W + GLA_V], batch, seq)
        o_att = _band_attention(z, bias_all, i, ms[RET_W + GLA_V:], batch, seq)
        x1, hp, meta, meta_t, counts = _out_route(xf, o_ret, o_gla, o_att, w_out_b, g_ffn[i], wr_all, br_all, i)
        pos, tile_expert, n_used, pad_start, pad_n, n_rows = _routing_tables(meta_t, counts, t, MOE_TILE)
        xs = _dispatch(hp, pos.reshape(t // tok_tile, 1, 2 * tok_tile), pad_start, pad_n, n_used, n_rows)
        ys = _moe(xs, tile_expert, n_used, w_expert_gate, w_expert_up, w_expert_down, i)
        r = tok_tile // ple_tile
        pos_ple = pos.reshape(t // tok_tile, 2, r, ple_tile).transpose(0, 2, 1, 3).reshape(t // ple_tile, 1, 2 * ple_tile)
        final = i == depth - 1
        outs = _combine_ple(x1, meta, pos_ple, ys, p_all, g_ple[i], w_pg_b, w_pp_b, i,
                            g_final if final else g_mix[i + 1], final=final)
        xf = outs[0]
        if not final:
            h = outs[1]
    return xf.reshape(batch, seq, d)
```

```python
import functools
import math

import numpy as np
import jax
import jax.numpy as jnp
from jax import lax
from jax.experimental import pallas as pl
from jax.experimental.pallas import tpu as pltpu

F32 = jnp.float32
BF16 = jnp.bfloat16
U32 = jnp.uint32
I32 = jnp.int32

CHUNK = 64
HEAD_DIM = 128
RET_HEADS = 6
GLA_HEADS = 4
GLA_DK = 64
GLA_DV = 128
GLA_RANK = 16
GLA_TAU = 16.0
ATT_HEADS = 6
ATT_BAND_CHUNKS = 8
MAX_REL = 128
N_GROUPS = 4
EXPERTS_PER_GROUP = 8
N_EXPERTS = N_GROUPS * EXPERTS_PER_GROUP
EXPERT_FF = 256
EPS = 1e-6

RET_W = RET_HEADS * HEAD_DIM
GLA_QK = GLA_HEADS * GLA_DK
GLA_V = GLA_HEADS * GLA_DV
ATT_W = ATT_HEADS * HEAD_DIM

LANES = 128
V7X_VMEM_LIMIT = 56 * 1024 * 1024

Z_RET = 0
Z_GQ = 4 * RET_W
Z_GK = Z_GQ + GLA_QK
Z_GV = Z_GK + GLA_QK
Z_GG = Z_GV + GLA_V
Z_ATT = Z_GG + GLA_V
Z_GA = Z_ATT + 3 * ATT_W
Z_W = 7 * 1024

SUPER = 4 * CHUNK
ROUTE_LANE0 = N_GROUPS
MOE_TILE = 512
TOK_TILE = 512
PLE_TILE = 256
ROW_SUB = 8


def _cparams(n_axes):
    return pltpu.CompilerParams(
        dimension_semantics=("arbitrary",) * n_axes,
        vmem_limit_bytes=V7X_VMEM_LIMIT,
    )


def _dot(a, b):
    return jnp.dot(a, b, preferred_element_type=F32)


def _dot_nt(a, b):
    return lax.dot_general(a, b, (((1,), (1,)), ((), ())), preferred_element_type=F32)


def _dot_tn(a, b):
    return lax.dot_general(a, b, (((0,), (0,)), ((), ())), preferred_element_type=F32)


def _rms(x, g):
    return x * lax.rsqrt(jnp.mean(x * x, axis=-1, keepdims=True) + EPS) * g


def _silu(x):
    return x / (1.0 + jnp.exp(-x))


def _pack_bf16_pair(lo, hi):
    lo_b = lax.bitcast_convert_type(lo.astype(BF16).astype(F32), U32)
    hi_b = lax.bitcast_convert_type(hi.astype(BF16).astype(F32), U32)
    return (lo_b >> 16) | (hi_b & jnp.uint32(0xFFFF0000))


def _unpack_bf16_pair(w):
    lo = lax.bitcast_convert_type(w << 16, F32)
    hi = lax.bitcast_convert_type(w & jnp.uint32(0xFFFF0000), F32)
    return lo, hi


def _rot_kernel(inv_ref, cos_ref, sin_ref):
    rows = cos_ref.shape[0]
    pos = (lax.broadcasted_iota(I32, (rows, LANES), 0) + pl.program_id(0) * rows).astype(F32)
    lane = lax.broadcasted_iota(I32, (rows, LANES), 1)
    ang = pos * inv_ref[...]
    cos_ref[...] = jnp.cos(ang)
    s = jnp.sin(ang)
    sin_ref[...] = jnp.where(lane < HEAD_DIM // 2, -s, s)


def _rotary_tables(seq):
    half = HEAD_DIM // 2
    inv = np.float32(1.0) / (np.float32(10000.0) ** (np.arange(half, dtype=np.float32) / np.float32(half)))
    inv2 = jnp.asarray(np.concatenate([inv, inv]).reshape(1, LANES).astype(np.float32))
    rows = min(seq, 1024)
    return pl.pallas_call(
        _rot_kernel,
        grid=(seq // rows,),
        in_specs=[pl.BlockSpec((1, LANES), lambda i: (0, 0))],
        out_specs=[pl.BlockSpec((rows, LANES), lambda i: (i, 0))] * 2,
        out_shape=[jax.ShapeDtypeStruct((seq, LANES), F32)] * 2,
        compiler_params=_cparams(1),
        name="rotary_tables",
    )(inv2)


def _norm_kernel(x_ref, g_ref, h_ref):
    h_ref[...] = _rms(x_ref[...], g_ref[...]).astype(BF16)


def _norm_cast(x, g):
    t, d = x.shape
    tm = min(t, 512)
    return pl.pallas_call(
        _norm_kernel,
        grid=(t // tm,),
        in_specs=[pl.BlockSpec((tm, d), lambda i: (i, 0)), pl.BlockSpec((1, d), lambda i: (0, 0))],
        out_specs=pl.BlockSpec((tm, d), lambda i: (i, 0)),
        out_shape=jax.ShapeDtypeStruct((t, d), BF16),
        compiler_params=_cparams(1),
        name="norm_cast",
    )(x, g.reshape(1, d))


def _in_kernel(h_ref, w_ref, z_ref):
    z_ref[...] = _dot(h_ref[...], w_ref[...]).astype(BF16)


def _in_proj(h, w_all, layer):
    t, d = h.shape
    n = w_all.shape[2]
    tm = min(t, 2048)
    tn = min(n, 1024)
    return pl.pallas_call(
        _in_kernel,
        grid=(t // tm, n // tn),
        in_specs=[
            pl.BlockSpec((tm, d), lambda i, j: (i, 0)),
            pl.BlockSpec((None, d, tn), lambda i, j: (layer, 0, j)),
        ],
        out_specs=pl.BlockSpec((tm, tn), lambda i, j: (i, j)),
        out_shape=jax.ShapeDtypeStruct((t, n), BF16),
        compiler_params=_cparams(2),
        name="in_proj",
    )(h, w_all)


def _ret_log_gamma(h):
    return math.log(1.0 - 2.0 ** (-5.0 - h))


def _ret_init(state_ref, dmat_ref, xi_ref, zeta_ref):
    state_ref[...] = jnp.zeros_like(state_ref)
    row = lax.broadcasted_iota(I32, (SUPER, SUPER), 0)
    col = lax.broadcasted_iota(I32, (SUPER, SUPER), 1)
    dist = jnp.abs(row - col).astype(F32)
    keep = (col <= row) | ((row // CHUNK) == (col // CHUNK))
    t = lax.broadcasted_iota(I32, (SUPER, HEAD_DIM), 0).astype(F32)
    for h in range(RET_HEADS):
        lg = _ret_log_gamma(h)
        dmat_ref[h] = jnp.where(keep, jnp.exp(lg * dist), 0.0)
        xi_ref[h] = jnp.exp(lg * (t + 1.0))
        zeta_ref[h] = jnp.exp(lg * (SUPER - 1.0 - t))


def _ret_head(h, q_ref, k_ref, v_ref, g_ref, cos, sin, ms_ref, o_ref, state_ref, dmat_ref, xi_ref, zeta_ref):
    sl = slice(h * HEAD_DIM, (h + 1) * HEAD_DIM)
    scale = HEAD_DIM ** -0.5
    q = q_ref[:, sl].astype(F32)
    k = k_ref[:, sl].astype(F32)
    vb = v_ref[:, sl]
    qr = q * cos + pltpu.roll(q, HEAD_DIM // 2, 1) * sin
    kr = (k * cos + pltpu.roll(k, HEAD_DIM // 2, 1) * sin) * scale
    qb = qr.astype(BF16)
    kb = kr.astype(BF16)
    sc = _dot_nt(qb, kb) * dmat_ref[h]
    intra = _dot(sc.astype(BF16), vb)
    st = state_ref[h]
    cross = _dot(qb, st.astype(BF16)) * xi_ref[h]
    kz = (kr * zeta_ref[h]).astype(BF16)
    upd = _dot_tn(kz, vb)
    state_ref[h] = math.exp(_ret_log_gamma(h) * SUPER) * st + upd
    o = intra + cross
    mu = jnp.mean(o, axis=-1, keepdims=True)
    oc = o - mu
    var = jnp.mean(oc * oc, axis=-1, keepdims=True)
    on = oc * lax.rsqrt(var + EPS)
    gate = _silu(g_ref[:, sl].astype(F32))
    o_ref[:, sl] = (on * gate * ms_ref[:, sl]).astype(BF16)


def _gla_log_alpha(a_ref, wa_ref, ba_ref):
    pre = _dot(a_ref[...], wa_ref[...]) + ba_ref[...]
    return (jnp.minimum(pre, 0.0) - jnp.log(1.0 + jnp.exp(-jnp.abs(pre)))) * (1.0 / GLA_TAU)


def _gla_chunk(ci, la, q_ref, k_ref, v_ref, g_ref, ms_ref, o_ref, state_ref):
    r = lax.broadcasted_iota(I32, (CHUNK, CHUNK), 0)
    c = lax.broadcasted_iota(I32, (CHUNK, CHUNK), 1)
    causal = c <= r
    tril = jnp.where(causal, 1.0, 0.0).astype(BF16)
    ones = jnp.ones((CHUNK, GLA_DV), BF16)
    scale = GLA_DK ** -0.5
    rows = slice(ci * CHUNK, (ci + 1) * CHUNK)
    la_c = la[rows, :]
    hi = la_c.astype(BF16)
    lo = (la_c - hi.astype(F32)).astype(BF16)
    bcs = _dot(tril, hi) + _dot(tril, lo)
    gl = _dot_tn(hi, ones) + _dot_tn(lo, ones)
    blast = bcs[CHUNK - 1:CHUNK, :]
    eb = jnp.exp(bcs)
    enb = jnp.exp(-bcs)
    ekb = jnp.exp(blast - bcs)
    qc = q_ref[rows, :].astype(F32) * scale * eb
    kc = k_ref[rows, :].astype(F32)
    kin = (kc * enb).astype(BF16)
    kup = (kc * ekb).astype(BF16)
    qin = qc.astype(BF16)
    for h in range(GLA_HEADS):
        ks = slice(h * GLA_DK, (h + 1) * GLA_DK)
        vs = slice(h * GLA_DV, (h + 1) * GLA_DV)
        vb = v_ref[rows, vs]
        a = jnp.where(causal, _dot_nt(qin[:, ks], kin[:, ks]), 0.0)
        intra = _dot(a.astype(BF16), vb)
        st = state_ref[h]
        cross = _dot(qin[:, ks], st.astype(BF16))
        upd = _dot_tn(kup[:, ks], vb)
        state_ref[h] = jnp.exp(gl[ks, :]) * st + upd
        o = intra + cross
        on = o * lax.rsqrt(jnp.mean(o * o, axis=-1, keepdims=True) + EPS)
        gate = _silu(g_ref[rows, vs].astype(F32))
        o_ref[rows, vs] = (on * gate * ms_ref[:, vs]).astype(BF16)


BAND = 3 * SUPER


BAND_KINDS = 3
ROLL_W = 1024
NEG_INF = -1e30


def _bias_kernel(rb_ref, o_ref):
    l = pl.program_id(0)
    h = pl.program_id(1)
    m = lax.broadcasted_iota(I32, (8, ROLL_W), 1)
    d = jnp.where(m < BAND, m, m - ROLL_W)
    idx = jnp.clip(2 * SUPER - d, -MAX_REL, MAX_REL) + MAX_REL

    def body(j, acc):
        return jnp.where(idx == j, rb_ref[l, h, j], acc)

    diag = lax.fori_loop(0, 2 * MAX_REL + 1, body, jnp.zeros((8, ROLL_W), F32))
    full = pltpu.roll(jnp.broadcast_to(diag[0:1, :], (SUPER, ROLL_W)), 0, 1, stride=1, stride_axis=0)
    bias = full[:, :BAND]
    row = lax.broadcasted_iota(I32, (SUPER, BAND), 0)
    col = lax.broadcasted_iota(I32, (SUPER, BAND), 1)
    dist = 2 * (SUPER // CHUNK) + row // CHUNK - col // CHUNK
    in_band = (dist >= 0) & (dist <= ATT_BAND_CHUNKS)
    for kind in range(BAND_KINDS):
        first_valid_col = (BAND_KINDS - 1 - kind) * SUPER
        o_ref[kind, 0] = jnp.where(in_band & (col >= first_valid_col), bias, NEG_INF)


def _att_bias(rel_bias):
    depth = rel_bias.shape[0]
    return pl.pallas_call(
        _bias_kernel,
        grid=(depth, ATT_HEADS),
        in_specs=[pl.BlockSpec(memory_space=pltpu.SMEM)],
        out_specs=pl.BlockSpec((None, BAND_KINDS, 1, SUPER, BAND), lambda l, h: (l, 0, h, 0, 0)),
        out_shape=jax.ShapeDtypeStruct((depth, BAND_KINDS, ATT_HEADS, SUPER, BAND), F32),
        compiler_params=_cparams(2),
        name="att_bias",
    )(rel_bias)


def _att_head(h, q_ref, k_refs, v_refs, bias_ref, ms_ref, o_ref):
    sl = slice(h * HEAD_DIM, (h + 1) * HEAD_DIM)
    scale = HEAD_DIM ** -0.5
    qb = (q_ref[:, sl].astype(F32) * scale).astype(BF16)
    sc = [_dot_nt(qb, k_refs[j][:, sl]) + bias_ref[h, :, j * SUPER:(j + 1) * SUPER] for j in range(3)]
    m = jnp.max(jnp.maximum(jnp.maximum(sc[0], sc[1]), sc[2]), axis=-1, keepdims=True)
    ps = [jnp.exp(sj - m) for sj in sc]
    den = jnp.sum(ps[0] + ps[1] + ps[2], axis=-1, keepdims=True)
    acc = _dot(ps[0].astype(BF16), v_refs[0][:, sl])
    for j in (1, 2):
        acc = acc + _dot(ps[j].astype(BF16), v_refs[j][:, sl])
    o = acc / den
    on = o * lax.rsqrt(jnp.mean(o * o, axis=-1, keepdims=True) + EPS)
    o_ref[:, sl] = (on * ms_ref[:, sl]).astype(BF16)


def _mixer_kernel(rq_ref, rk_ref, rv_ref, rg_ref, cos_ref, sin_ref, rms_ref,
                  gq_ref, gk_ref, gv_ref, gg_ref, ga_ref, wa_ref, ba_ref, gms_ref,
                  aq_ref, k0_ref, k1_ref, k2_ref, v0_ref, v1_ref, v2_ref, bias_ref, ams_ref,
                  oret_ref, ogla_ref, oatt_ref,
                  rstate_ref, dmat_ref, xi_ref, zeta_ref, gstate_ref):
    @pl.when(pl.program_id(1) == 0)
    def _():
        _ret_init(rstate_ref, dmat_ref, xi_ref, zeta_ref)
        gstate_ref[...] = jnp.zeros_like(gstate_ref)

    cos = cos_ref[...]
    sin = sin_ref[...]
    la = _gla_log_alpha(ga_ref, wa_ref, ba_ref)
    k_refs = (k0_ref, k1_ref, k2_ref)
    v_refs = (v0_ref, v1_ref, v2_ref)
    n_chunks = SUPER // CHUNK
    for h in range(max(RET_HEADS, ATT_HEADS)):
        if h < RET_HEADS:
            _ret_head(h, rq_ref, rk_ref, rv_ref, rg_ref, cos, sin, rms_ref, oret_ref,
                      rstate_ref, dmat_ref, xi_ref, zeta_ref)
        if h < ATT_HEADS:
            _att_head(h, aq_ref, k_refs, v_refs, bias_ref, ams_ref, oatt_ref)
        if h < n_chunks:
            _gla_chunk(h, la, gq_ref, gk_ref, gv_ref, gg_ref, gms_ref, ogla_ref, gstate_ref)


def _mixers(z, cos2, sin2, wa_all, ba, bias, ms, layer, batch, seq):
    t = z.shape[0]
    ns = seq // SUPER
    row = lambda b, s: b * ns + s
    zspec = lambda width, off: pl.BlockSpec((SUPER, width), lambda b, s: (row(b, s), off // width))
    const = lambda width: pl.BlockSpec((1, width), lambda b, s: (0, 0))

    def kv_spec(off, back):
        return pl.BlockSpec((SUPER, ATT_W), lambda b, s: (b * ns + jnp.maximum(s - back, 0), off // ATT_W))

    ak, av = Z_ATT + ATT_W, Z_ATT + 2 * ATT_W
    return pl.pallas_call(
        _mixer_kernel,
        grid=(batch, ns),
        in_specs=[
            zspec(RET_W, 0), zspec(RET_W, RET_W), zspec(RET_W, 2 * RET_W), zspec(RET_W, 3 * RET_W),
            pl.BlockSpec((SUPER, LANES), lambda b, s: (s, 0)),
            pl.BlockSpec((SUPER, LANES), lambda b, s: (s, 0)),
            const(RET_W),
            zspec(GLA_QK, Z_GQ), zspec(GLA_QK, Z_GK), zspec(GLA_V, Z_GV), zspec(GLA_V, Z_GG),
            zspec(LANES, Z_GA),
            pl.BlockSpec((None, LANES, GLA_QK), lambda b, s: (layer, 0, 0)),
            const(GLA_QK), const(GLA_V),
            zspec(ATT_W, Z_ATT),
            kv_spec(ak, 2), kv_spec(ak, 1), kv_spec(ak, 0),
            kv_spec(av, 2), kv_spec(av, 1), kv_spec(av, 0),
            pl.BlockSpec((None, None, ATT_HEADS, SUPER, BAND),
                         lambda b, s: (layer, jnp.minimum(s, BAND_KINDS - 1), 0, 0, 0)),
            const(ATT_W),
        ],
        out_specs=[
            pl.BlockSpec((SUPER, RET_W), lambda b, s: (row(b, s), 0)),
            pl.BlockSpec((SUPER, GLA_V), lambda b, s: (row(b, s), 0)),
            pl.BlockSpec((SUPER, ATT_W), lambda b, s: (row(b, s), 0)),
        ],
        out_shape=[
            jax.ShapeDtypeStruct((t, RET_W), BF16),
            jax.ShapeDtypeStruct((t, GLA_V), BF16),
            jax.ShapeDtypeStruct((t, ATT_W), BF16),
        ],
        scratch_shapes=[
            pltpu.VMEM((RET_HEADS, HEAD_DIM, HEAD_DIM), F32),
            pltpu.VMEM((RET_HEADS, SUPER, SUPER), F32),
            pltpu.VMEM((RET_HEADS, SUPER, HEAD_DIM), F32),
            pltpu.VMEM((RET_HEADS, SUPER, HEAD_DIM), F32),
            pltpu.VMEM((GLA_HEADS, GLA_DK, GLA_DV), F32),
        ],
        compiler_params=_cparams(2),
        name="mixers",
    )(z, z, z, z, cos2, sin2, ms[:RET_W].reshape(1, RET_W),
      z, z, z, z, z, wa_all, ba.reshape(1, GLA_QK), ms[RET_W:RET_W + GLA_V].reshape(1, GLA_V),
      z, z, z, z, z, z, z, bias, ms[RET_W + GLA_V:].reshape(1, ATT_W))


def _store_token_tiles(ref, packed):
    tm = packed.shape[0]
    for s in range(ROW_SUB):
        ref[pl.ds(s, tm, stride=ROW_SUB), :] = packed[:, s * LANES:(s + 1) * LANES]


def _load_token_tiles(ref, tm):
    return jnp.concatenate([ref[pl.ds(s, tm, stride=ROW_SUB), :] for s in range(ROW_SUB)], axis=1)


def _tile_rows(ref, row, n_rows=1):
    start = row * ROW_SUB
    if not isinstance(start, int):
        start = pl.multiple_of(start, ROW_SUB)
    return ref.at[pl.ds(start, n_rows * ROW_SUB), :]


def _route_kernel(x_ref, a_ref, b_ref, c_ref, w_ref, g_ref, wr_ref, br_ref,
                  x1_ref, hp_ref, meta_ref, metat_ref, cnt_ref, stage_ref, run_ref):
    i = pl.program_id(0)
    tm = x_ref.shape[0]
    half = x_ref.shape[1] // 2

    @pl.when(i == 0)
    def _():
        run_ref[...] = jnp.zeros_like(run_ref)
        stage_ref[...] = jnp.zeros_like(stage_ref)

    xr = stage_ref[...]
    live = jnp.where(i > 0, 1.0, 0.0)

    h = _rms(xr, g_ref[...])
    _store_token_tiles(hp_ref, _pack_bf16_pair(h[:, :half], h[:, half:]))
    lg = _dot(h.astype(BF16), wr_ref[...]) + br_ref[...]
    lane = lax.broadcasted_iota(I32, (tm, LANES), 1)
    neg = jnp.float32(-1e30)
    big = jnp.int32(LANES)
    gl = jnp.where(lane < N_GROUPS, lg, neg)
    gm = jnp.max(gl, axis=-1, keepdims=True)
    p_group = 1.0 / jnp.sum(jnp.exp(gl - gm), axis=-1, keepdims=True)
    g_idx = jnp.min(jnp.where(gl == gm, lane, big), axis=-1, keepdims=True)
    lo = ROUTE_LANE0 + EXPERTS_PER_GROUP * g_idx
    el = jnp.where((lane >= lo) & (lane < lo + EXPERTS_PER_GROUP), lg, neg)
    m1 = jnp.max(el, axis=-1, keepdims=True)
    i1 = jnp.min(jnp.where(el == m1, lane, big), axis=-1, keepdims=True)
    el2 = jnp.where(lane == i1, neg, el)
    m2 = jnp.max(el2, axis=-1, keepdims=True)
    i2 = jnp.min(jnp.where(el2 == m2, lane, big), axis=-1, keepdims=True)
    e2 = jnp.exp(m2 - m1)
    c1 = p_group / (1.0 + e2)
    c2 = p_group * e2 / (1.0 + e2)
    oh1 = jnp.where(lane == i1, 1.0, 0.0)
    oh2 = jnp.where(lane == i2, 1.0, 0.0)
    oh = (oh1 + oh2) * live
    r = lax.broadcasted_iota(I32, (tm, tm), 0)
    c = lax.broadcasted_iota(I32, (tm, tm), 1)
    stril = jnp.where(c < r, 1.0, 0.0).astype(BF16)
    before = _dot(stril, oh.astype(BF16)) + run_ref[0:1, :]
    rank1 = jnp.sum(before * oh1, axis=-1, keepdims=True)
    rank2 = jnp.sum(before * oh2, axis=-1, keepdims=True)
    run_ref[0:1, :] = run_ref[0:1, :] + jnp.sum(oh, axis=0, keepdims=True)
    cnt_ref[...] = run_ref[...]
    e1f = (i1 - ROUTE_LANE0).astype(F32)
    e2f = (i2 - ROUTE_LANE0).astype(F32)
    meta = jnp.where(lane == 0, e1f, 0.0)
    meta = jnp.where(lane == 1, e2f, meta)
    meta = jnp.where(lane == 2, rank1, meta)
    meta = jnp.where(lane == 3, rank2, meta)
    meta = jnp.where(lane == 4, c1, meta)
    meta = jnp.where(lane == 5, c2, meta)
    meta_ref[...] = meta
    metat_ref[0] = meta.T[0:8, :]

    x1 = (x_ref[...] + _dot(a_ref[...], w_ref[:RET_W, :]) + _dot(b_ref[...], w_ref[RET_W:RET_W + GLA_V, :])
          + _dot(c_ref[...], w_ref[RET_W + GLA_V:, :]))
    x1_ref[...] = x1
    stage_ref[...] = x1


def _out_route(x, o_ret, o_gla, o_att, w_all, g, wr_all, br_all, layer):
    t, d = x.shape
    tm = min(t, TOK_TILE)
    nt = t // tm
    cur = lambda w: pl.BlockSpec((tm, w), lambda i: (jnp.minimum(i, nt - 1), 0))
    prev = lambda i: jnp.maximum(i - 1, 0)
    return pl.pallas_call(
        _route_kernel,
        grid=(nt + 1,),
        in_specs=[
            cur(d), cur(RET_W), cur(GLA_V), cur(ATT_W),
            pl.BlockSpec((None, w_all.shape[1], d), lambda i: (layer, 0, 0), pipeline_mode=pl.Buffered(1)),
            pl.BlockSpec((1, d), lambda i: (0, 0)),
            pl.BlockSpec((None, d, LANES), lambda i: (layer, 0, 0)),
            pl.BlockSpec((None, 1, LANES), lambda i: (layer, 0, 0)),
        ],
        out_specs=[
            cur(d),
            pl.BlockSpec((tm * ROW_SUB, LANES), lambda i: (prev(i), 0)),
            pl.BlockSpec((tm, LANES), lambda i: (prev(i), 0)),
            pl.BlockSpec((1, 8, tm), lambda i: (prev(i), 0, 0)),
            pl.BlockSpec((8, LANES), lambda i: (0, 0)),
        ],
        out_shape=[
            jax.ShapeDtypeStruct((t, d), F32),
            jax.ShapeDtypeStruct((t * ROW_SUB, LANES), U32),
            jax.ShapeDtypeStruct((t, LANES), F32),
            jax.ShapeDtypeStruct((nt, 8, tm), F32),
            jax.ShapeDtypeStruct((8, LANES), F32),
        ],
        scratch_shapes=[pltpu.VMEM((tm, d), F32), pltpu.VMEM((8, LANES), F32)],
        compiler_params=_cparams(1),
        name="out_proj_router",
    )(x, o_ret, o_gla, o_att, w_all, g.reshape(1, d), wr_all, br_all)


_PAD_BITS = tuple(reversed(range(MOE_TILE.bit_length() - 1)))


def _dispatch_kernel(ps_ref, pn_ref, nu_ref, pos_ref, hp_ref, xs_ref, zero_ref, sem_ref, zsem_ref):
    i = pl.program_id(0)
    tm = pos_ref.shape[2] // 2
    zrows = zero_ref.shape[0] // ROW_SUB
    n_tiles = xs_ref.shape[0] // (ROW_SUB * MOE_TILE)

    def pad_copies(start):
        def zero_copy(off, k):
            cp = pltpu.make_async_copy(_tile_rows(zero_ref, 0, k), _tile_rows(xs_ref, off, k), zsem_ref.at[0])
            if start:
                cp.start()
            else:
                cp.wait()

        def body(e, carry):
            n = pn_ref[e]
            base = ps_ref[e]
            for bit in _PAD_BITS:
                k = 1 << bit

                @pl.when((n & k) != 0)
                def _():
                    zero_copy(base + ((n >> (bit + 1)) << (bit + 1)), k)
            return carry

        lax.fori_loop(0, N_EXPERTS, body, 0)

        def tail(tile, carry):
            for part in range(MOE_TILE // zrows):
                zero_copy(tile * MOE_TILE + part * zrows, zrows)
            return carry

        lax.fori_loop(nu_ref[0], n_tiles, tail, 0)

    @pl.when(i == 0)
    def _():
        zero_ref[...] = jnp.zeros_like(zero_ref)
        pad_copies(True)
        pad_copies(False)

    def row_copy(j):
        return pltpu.make_async_copy(_tile_rows(hp_ref, j % tm), _tile_rows(xs_ref, pos_ref[0, 0, j]),
                                     sem_ref.at[0])

    for j in range(2 * tm):
        row_copy(j).start(priority=j % 2)
    for j in range(2 * tm):
        row_copy(j).wait()


def _dispatch(hp, pos_tiles, pad_start, pad_n, n_used, n_rows):
    nt = pos_tiles.shape[0]
    tm = pos_tiles.shape[2] // 2
    grid_spec = pltpu.PrefetchScalarGridSpec(
        num_scalar_prefetch=3,
        grid=(nt,),
        in_specs=[
            pl.BlockSpec((1, 1, 2 * tm), lambda i, ps, pn, nu: (i, 0, 0), memory_space=pltpu.SMEM),
            pl.BlockSpec((tm * ROW_SUB, LANES), lambda i, ps, pn, nu: (i, 0)),
        ],
        out_specs=pl.BlockSpec(memory_space=pl.ANY),
        scratch_shapes=[
            pltpu.VMEM((MOE_TILE // 2 * ROW_SUB, LANES), U32),
            pltpu.SemaphoreType.DMA((1,)),
            pltpu.SemaphoreType.DMA((1,)),
        ],
    )
    return pl.pallas_call(
        _dispatch_kernel,
        grid_spec=grid_spec,
        out_shape=jax.ShapeDtypeStruct((n_rows * ROW_SUB, LANES), U32),
        compiler_params=_cparams(1),
        name="dispatch",
    )(pad_start, pad_n, n_used, pos_tiles, hp)


def _moe_kernel(te_ref, nu_ref, xs_ref, wg_ref, wu_ref, wd_ref, y_ref, wgb_ref, wub_ref, wdb_ref):
    i = pl.program_id(0)
    n_used = nu_ref[0]
    tm = xs_ref.shape[0] // ROW_SUB
    half = ROW_SUB * LANES
    e = te_ref[i]
    e_prev = te_ref[jnp.maximum(i - 1, 0)]

    @pl.when((i == 0) | (e != e_prev))
    def _():
        wgb_ref[...] = wg_ref[...].astype(BF16)
        wub_ref[...] = wu_ref[...].astype(BF16)
        wdb_ref[...] = wd_ref[...].astype(BF16)

    @pl.when(i < n_used)
    def _():
        lo, hi = _unpack_bf16_pair(_load_token_tiles(xs_ref, tm))
        xl = lo.astype(BF16)
        xh = hi.astype(BF16)
        a = _dot(xl, wgb_ref[:half, :]) + _dot(xh, wgb_ref[half:, :])
        u = _dot(xl, wub_ref[:half, :]) + _dot(xh, wub_ref[half:, :])
        hid = (_silu(a) * u).astype(BF16)
        y = _dot(hid, wdb_ref[...])
        _store_token_tiles(y_ref, _pack_bf16_pair(y[:, :half], y[:, half:]))

    @pl.when(i >= n_used)
    def _():
        y_ref[...] = jnp.zeros_like(y_ref)


def _moe(xs, tile_expert, n_used, wg, wu, wd, layer):
    nt = tile_expert.shape[0]
    tm = xs.shape[0] // ROW_SUB // nt
    d, ff = wg.shape[-2], wg.shape[-1]

    def w_spec(shape):
        return pl.BlockSpec((None, None, None) + shape,
                            lambda i, te, nu: (layer, te[i] // EXPERTS_PER_GROUP, te[i] % EXPERTS_PER_GROUP, 0, 0))

    grid_spec = pltpu.PrefetchScalarGridSpec(
        num_scalar_prefetch=2,
        grid=(nt,),
        in_specs=[
            pl.BlockSpec((tm * ROW_SUB, LANES), lambda i, te, nu: (jnp.minimum(i, nu[0] - 1), 0)),
            w_spec((d, ff)), w_spec((d, ff)), w_spec((ff, d)),
        ],
        out_specs=pl.BlockSpec((tm * ROW_SUB, LANES), lambda i, te, nu: (i, 0)),
        scratch_shapes=[
            pltpu.VMEM((d, ff), BF16),
            pltpu.VMEM((d, ff), BF16),
            pltpu.VMEM((ff, d), BF16),
        ],
    )
    return pl.pallas_call(
        _moe_kernel,
        grid_spec=grid_spec,
        out_shape=jax.ShapeDtypeStruct(xs.shape, U32),
        compiler_params=_cparams(1),
        name="expert_mlp",
    )(tile_expert, n_used, xs, wg, wu, wd)


def _ple_kernel(pos_ref, x_ref, meta_ref, y_ref, p_ref, g_ref, wg_ref, wp_ref, gn_ref, *rest, final):
    if final:
        o_ref, buf_ref, sem_ref = rest
    else:
        o_ref, hn_ref, buf_ref, sem_ref = rest
    i = pl.program_id(0)
    n = pl.num_programs(0) - 1
    tm = x_ref.shape[0]
    half = x_ref.shape[1] // 2

    for par in range(2):
        @pl.when((i < n) & (i % 2 == par))
        def _():
            for j in range(2 * tm):
                pltpu.make_async_copy(_tile_rows(y_ref, pos_ref[0, 0, j]), _tile_rows(buf_ref.at[par], j),
                                      sem_ref.at[par]).start(priority=j % 2)

    @pl.when(i > 0)
    def _():
        slot = (i + 1) % 2
        for j in range(2 * tm):
            pltpu.make_async_copy(_tile_rows(y_ref, 0), _tile_rows(buf_ref.at[slot], j), sem_ref.at[slot]).wait()
        rows = _load_token_tiles(buf_ref.at[slot], 2 * tm)
        meta = meta_ref[...]
        c1 = meta[:, 4:5]
        c2 = meta[:, 5:6]
        lo1, hi1 = _unpack_bf16_pair(rows[:tm])
        lo2, hi2 = _unpack_bf16_pair(rows[tm:])
        xl = x_ref[:, :half] + c1 * lo1 + c2 * lo2
        xh = x_ref[:, half:] + c1 * hi1 + c2 * hi2
        ms = (jnp.sum(xl * xl, axis=-1, keepdims=True) + jnp.sum(xh * xh, axis=-1, keepdims=True)) / (2 * half)
        inv = lax.rsqrt(ms + EPS)
        hl = (xl * inv * g_ref[:, :half]).astype(BF16)
        hh = (xh * inv * g_ref[:, half:]).astype(BF16)
        gate = _dot(hl, wg_ref[:half, :]) + _dot(hh, wg_ref[half:, :])
        gate = 1.0 / (1.0 + jnp.exp(-gate))
        pp = _dot(p_ref[...].astype(BF16), wp_ref[...])
        ol = xl + gate[:, :half] * pp[:, :half]
        oh = xh + gate[:, half:] * pp[:, half:]
        ms2 = (jnp.sum(ol * ol, axis=-1, keepdims=True) + jnp.sum(oh * oh, axis=-1, keepdims=True)) / (2 * half)
        inv2 = lax.rsqrt(ms2 + EPS)
        nl = ol * inv2 * gn_ref[:, :half]
        nh = oh * inv2 * gn_ref[:, half:]
        if final:
            o_ref[:, :half] = nl
            o_ref[:, half:] = nh
        else:
            o_ref[:, :half] = ol
            o_ref[:, half:] = oh
            hn_ref[:, :half] = nl.astype(BF16)
            hn_ref[:, half:] = nh.astype(BF16)


def _combine_ple(x, meta, pos_tiles, ys, p_all, g_ple, w_gate_all, w_proj_all, layer, g_next, final):
    t, d = x.shape
    nt = pos_tiles.shape[0]
    tm = pos_tiles.shape[2] // 2
    pd = p_all.shape[2]
    const = lambda shape: pl.BlockSpec(shape, lambda i: (0, 0))
    prev = lambda w: pl.BlockSpec((tm, w), lambda i: (jnp.maximum(i - 1, 0), 0))
    return pl.pallas_call(
        functools.partial(_ple_kernel, final=final),
        grid=(nt + 1,),
        in_specs=[
            pl.BlockSpec((1, 1, 2 * tm), lambda i: (jnp.minimum(i, nt - 1), 0, 0), memory_space=pltpu.SMEM),
            prev(d), prev(LANES),
            pl.BlockSpec(memory_space=pl.ANY),
            pl.BlockSpec((None, tm, pd), lambda i: (layer, jnp.maximum(i - 1, 0), 0)),
            const((1, d)),
            pl.BlockSpec((None, d, d), lambda i: (layer, 0, 0)),
            pl.BlockSpec((None, pd, d), lambda i: (layer, 0, 0)),
            const((1, d)),
        ],
        out_specs=[prev(d)] if final else [prev(d), prev(d)],
        out_shape=[jax.ShapeDtypeStruct((t, d), F32)] + ([] if final else [jax.ShapeDtypeStruct((t, d), BF16)]),
        scratch_shapes=[pltpu.VMEM((2, 2 * tm * ROW_SUB, LANES), U32), pltpu.SemaphoreType.DMA((2,))],
        compiler_params=_cparams(1),
        name="combine_ple",
    )(pos_tiles, x, meta, ys, p_all, g_ple.reshape(1, d), w_gate_all, w_proj_all, g_next.reshape(1, d))


W_GA0 = 4 * RET_W + 2 * GLA_QK + 2 * GLA_V


PREP_COLS = 256
PREP_ALIGNED = W_GA0 // PREP_COLS
PREP_SHIFTED = (Z_GA - W_GA0) // PREP_COLS


def _prep_kernel(wt_ref, o_ref):
    j = pl.program_id(1)
    t = wt_ref[0].T

    @pl.when(j < PREP_ALIGNED + PREP_SHIFTED)
    def _():
        o_ref[...] = t.astype(BF16)

    @pl.when(j >= PREP_ALIGNED + PREP_SHIFTED)
    def _():
        lane = lax.broadcasted_iota(I32, t.shape, 1)
        o_ref[...] = jnp.where(lane < GLA_RANK, t, 0.0).astype(BF16)


def _prep_w_in(w):
    depth, d, n = w.shape
    wt = jnp.transpose(w, (0, 2, 1))

    def src_row(j):
        shifted = (W_GA0 + GLA_RANK) // 8 + (j - PREP_ALIGNED) * (PREP_COLS // 8)
        r8 = jnp.where(j < PREP_ALIGNED, j * (PREP_COLS // 8),
                       jnp.where(j < PREP_ALIGNED + PREP_SHIFTED, shifted, W_GA0 // 8))
        return r8 * 8

    return pl.pallas_call(
        _prep_kernel,
        grid=(depth, Z_W // PREP_COLS),
        in_specs=[pl.BlockSpec((pl.Element(1), pl.Element(PREP_COLS), pl.Element(d)),
                               lambda l, j: (l, src_row(j), 0))],
        out_specs=pl.BlockSpec((None, d, PREP_COLS), lambda l, j: (l, 0, j)),
        out_shape=jax.ShapeDtypeStruct((depth, d, Z_W), BF16),
        compiler_params=_cparams(2),
        name="w_in_layout",
    )(wt)


def _pad_lanes(parts, width, dtype):
    cat = jnp.concatenate(parts, axis=-1)
    pad = jnp.zeros(cat.shape[:-1] + (width - cat.shape[-1],), cat.dtype)
    return jnp.concatenate([cat, pad], axis=-1).astype(dtype)


def _routing_tables(meta_t, counts, n_tok, tile):
    eid = meta_t[:, 0:2, :].astype(I32)
    rank = meta_t[:, 2:4, :].astype(I32)
    cnt = counts[0, ROUTE_LANE0:ROUTE_LANE0 + N_EXPERTS].astype(I32)
    padded = ((cnt + tile - 1) // tile) * tile
    ends = jnp.cumsum(padded)
    offs = ends - padded
    base = jnp.zeros_like(eid)
    for e in range(N_EXPERTS):
        base = jnp.where(eid == e, offs[e], base)
    pos = base + rank
    n_rows = 2 * n_tok + N_EXPERTS * tile
    nt = n_rows // tile
    tile_start = jnp.arange(nt, dtype=I32) * tile
    tile_expert = jnp.minimum(jnp.sum((tile_start[:, None] >= ends[None, :]).astype(I32), axis=1),
                              N_EXPERTS - 1)
    n_used = (ends[-1] // tile).reshape(1)
    return pos, tile_expert, n_used, offs + cnt, padded - cnt, n_rows


def kernel(x, p, g_mix, w_in, gla_w_alpha, gla_b_alpha, rel_bias, mix_scale, w_out, g_ffn,
           w_router_group, b_router_group, w_router_expert, b_router_expert,
           w_expert_gate, w_expert_up, w_expert_down, g_ple, w_ple_gate, w_ple_proj, g_final):
    batch, seq, d = x.shape
    depth = g_mix.shape[0]
    t = batch * seq
    assert seq % SUPER == 0 and d == 2 * ROW_SUB * LANES
    tok_tile = min(t, TOK_TILE)
    ple_tile = min(t, PLE_TILE)
    xf = x.reshape(t, d)
    cos2, sin2 = _rotary_tables(seq)
    w_in_r = _prep_w_in(w_in)
    w_out_b = w_out.astype(BF16)
    wa_all = jnp.concatenate(
        [gla_w_alpha, jnp.zeros((depth, LANES - GLA_RANK, GLA_QK), gla_w_alpha.dtype)], axis=1).astype(BF16)
    wr_all = _pad_lanes([w_router_group, w_router_expert.reshape(depth, d, N_EXPERTS)], LANES, BF16)
    br_all = _pad_lanes([b_router_group, b_router_expert.reshape(depth, N_EXPERTS)], LANES, F32)
    br_all = br_all.reshape(depth, 1, LANES)
    p_all = p.reshape(depth, t, p.shape[-1])
    w_pg_b = w_ple_gate.astype(BF16)
    w_pp_b = w_ple_proj.astype(BF16)
    bias_all = _att_bias(rel_bias)
    h = _norm_cast(xf, g_mix[0])
    for i in range(depth):
        z = _in_proj(h, w_in_r, i)
        ms = mix_scale[i]
        o_ret, o_gla, o_att = _mixers(z, cos2, sin2, wa_all, gla_b_alpha[i], bias_all, ms, i, batch, seq)
        x1, hp, meta, meta_t, counts = _out_route(xf, o_ret, o_gla, o_att, w_out_b, g_ffn[i], wr_all, br_all, i)
        pos, tile_expert, n_used, pad_start, pad_n, n_rows = _routing_tables(meta_t, counts, t, MOE_TILE)
        xs = _dispatch(hp, pos.reshape(t // tok_tile, 1, 2 * tok_tile), pad_start, pad_n, n_used, n_rows)
        ys = _moe(xs, tile_expert, n_used, w_expert_gate, w_expert_up, w_expert_down, i)
        r = tok_tile // ple_tile
        pos_ple = pos.reshape(t // tok_tile, 2, r, ple_tile).transpose(0, 2, 1, 3).reshape(t // ple_tile, 1, 2 * ple_tile)
        final = i == depth - 1
        outs = _combine_ple(x1, meta, pos_ple, ys, p_all, g_ple[i], w_pg_b, w_pp_b, i,
                            g_final if final else g_mix[i + 1], final=final)
        xf = outs[0]
        if not final:
            h = outs[1]
    return xf.reshape(batch, seq, d)
```

```python
import functools
import math

import numpy as np
import jax
import jax.numpy as jnp
from jax import lax
from jax.experimental import pallas as pl
from jax.experimental.pallas import tpu as pltpu

F32 = jnp.float32
BF16 = jnp.bfloat16
U32 = jnp.uint32
I32 = jnp.int32

CHUNK = 64
HEAD_DIM = 128
RET_HEADS = 6
GLA_HEADS = 4
GLA_DK = 64
GLA_DV = 128
GLA_RANK = 16
GLA_TAU = 16.0
ATT_HEADS = 6
ATT_BAND_CHUNKS = 8
MAX_REL = 128
N_GROUPS = 4
EXPERTS_PER_GROUP = 8
N_EXPERTS = N_GROUPS * EXPERTS_PER_GROUP
EXPERT_FF = 256
EPS = 1e-6

RET_W = RET_HEADS * HEAD_DIM
GLA_QK = GLA_HEADS * GLA_DK
GLA_V = GLA_HEADS * GLA_DV
ATT_W = ATT_HEADS * HEAD_DIM

LANES = 128
V7X_VMEM_LIMIT = 56 * 1024 * 1024

Z_RET = 0
Z_GQ = 4 * RET_W
Z_GK = Z_GQ + GLA_QK
Z_GV = Z_GK + GLA_QK
Z_GG = Z_GV + GLA_V
Z_ATT = Z_GG + GLA_V
Z_GA = Z_ATT + 3 * ATT_W
Z_W = 7 * 1024

SUPER = 4 * CHUNK
ROUTE_LANE0 = N_GROUPS
MOE_TILE = 512
TOK_TILE = 512
PLE_TILE = 256
ROW_SUB = 8


def _cparams(n_axes):
    return pltpu.CompilerParams(
        dimension_semantics=("arbitrary",) * n_axes,
        vmem_limit_bytes=V7X_VMEM_LIMIT,
    )


def _dot(a, b):
    return jnp.dot(a, b, preferred_element_type=F32)


def _dot_nt(a, b):
    return lax.dot_general(a, b, (((1,), (1,)), ((), ())), preferred_element_type=F32)


def _dot_tn(a, b):
    return lax.dot_general(a, b, (((0,), (0,)), ((), ())), preferred_element_type=F32)


def _rms(x, g):
    return x * lax.rsqrt(jnp.mean(x * x, axis=-1, keepdims=True) + EPS) * g


def _silu(x):
    return x / (1.0 + jnp.exp(-x))


def _pack_bf16_pair(lo, hi):
    lo_b = lax.bitcast_convert_type(lo.astype(BF16).astype(F32), U32)
    hi_b = lax.bitcast_convert_type(hi.astype(BF16).astype(F32), U32)
    return (lo_b >> 16) | (hi_b & jnp.uint32(0xFFFF0000))


def _unpack_bf16_pair(w):
    lo = lax.bitcast_convert_type(w << 16, F32)
    hi = lax.bitcast_convert_type(w & jnp.uint32(0xFFFF0000), F32)
    return lo, hi


def _rot_kernel(inv_ref, cos_ref, sin_ref):
    rows = cos_ref.shape[0]
    pos = (lax.broadcasted_iota(I32, (rows, LANES), 0) + pl.program_id(0) * rows).astype(F32)
    lane = lax.broadcasted_iota(I32, (rows, LANES), 1)
    ang = pos * inv_ref[...]
    cos_ref[...] = jnp.cos(ang)
    s = jnp.sin(ang)
    sin_ref[...] = jnp.where(lane < HEAD_DIM // 2, -s, s)


def _rotary_tables(seq):
    half = HEAD_DIM // 2
    inv = np.float32(1.0) / (np.float32(10000.0) ** (np.arange(half, dtype=np.float32) / np.float32(half)))
    inv2 = jnp.asarray(np.concatenate([inv, inv]).reshape(1, LANES).astype(np.float32))
    rows = min(seq, 1024)
    return pl.pallas_call(
        _rot_kernel,
        grid=(seq // rows,),
        in_specs=[pl.BlockSpec((1, LANES), lambda i: (0, 0))],
        out_specs=[pl.BlockSpec((rows, LANES), lambda i: (i, 0))] * 2,
        out_shape=[jax.ShapeDtypeStruct((seq, LANES), F32)] * 2,
        compiler_params=_cparams(1),
        name="rotary_tables",
    )(inv2)


def _norm_kernel(x_ref, g_ref, h_ref):
    h_ref[...] = _rms(x_ref[...], g_ref[...]).astype(BF16)


def _norm_cast(x, g):
    t, d = x.shape
    tm = min(t, 512)
    return pl.pallas_call(
        _norm_kernel,
        grid=(t // tm,),
        in_specs=[pl.BlockSpec((tm, d), lambda i: (i, 0)), pl.BlockSpec((1, d), lambda i: (0, 0))],
        out_specs=pl.BlockSpec((tm, d), lambda i: (i, 0)),
        out_shape=jax.ShapeDtypeStruct((t, d), BF16),
        compiler_params=_cparams(1),
        name="norm_cast",
    )(x, g.reshape(1, d))


def _in_kernel(h_ref, w_ref, z_ref):
    z_ref[...] = _dot(h_ref[...], w_ref[...]).astype(BF16)


def _in_proj(h, w_all, layer):
    t, d = h.shape
    n = w_all.shape[2]
    tm = min(t, 2048)
    tn = min(n, 1024)
    return pl.pallas_call(
        _in_kernel,
        grid=(t // tm, n // tn),
        in_specs=[
            pl.BlockSpec((tm, d), lambda i, j: (i, 0)),
            pl.BlockSpec((None, d, tn), lambda i, j: (layer, 0, j)),
        ],
        out_specs=pl.BlockSpec((tm, tn), lambda i, j: (i, j)),
        out_shape=jax.ShapeDtypeStruct((t, n), BF16),
        compiler_params=_cparams(2),
        name="in_proj",
    )(h, w_all)


def _ret_log_gamma(h):
    return math.log(1.0 - 2.0 ** (-5.0 - h))


def _ret_init(state_ref, dmat_ref, xi_ref, zeta_ref):
    state_ref[...] = jnp.zeros_like(state_ref)
    row = lax.broadcasted_iota(I32, (SUPER, SUPER), 0)
    col = lax.broadcasted_iota(I32, (SUPER, SUPER), 1)
    dist = jnp.abs(row - col).astype(F32)
    keep = (col <= row) | ((row // CHUNK) == (col // CHUNK))
    t = lax.broadcasted_iota(I32, (SUPER, HEAD_DIM), 0).astype(F32)
    for h in range(RET_HEADS):
        lg = _ret_log_gamma(h)
        dmat_ref[h] = jnp.where(keep, jnp.exp(lg * dist), 0.0)
        xi_ref[h] = jnp.exp(lg * (t + 1.0))
        zeta_ref[h] = jnp.exp(lg * (SUPER - 1.0 - t))


def _ret_head(h, q_ref, k_ref, v_ref, g_ref, cos, sin, ms_ref, o_ref, state_ref, dmat_ref, xi_ref, zeta_ref):
    sl = slice(h * HEAD_DIM, (h + 1) * HEAD_DIM)
    scale = HEAD_DIM ** -0.5
    q = q_ref[:, sl].astype(F32)
    k = k_ref[:, sl].astype(F32)
    vb = v_ref[:, sl]
    qr = q * cos + pltpu.roll(q, HEAD_DIM // 2, 1) * sin
    kr = (k * cos + pltpu.roll(k, HEAD_DIM // 2, 1) * sin) * scale
    qb = qr.astype(BF16)
    kb = kr.astype(BF16)
    sc = _dot_nt(qb, kb) * dmat_ref[h]
    intra = _dot(sc.astype(BF16), vb)
    st = state_ref[h]
    cross = _dot(qb, st.astype(BF16)) * xi_ref[h]
    kz = (kr * zeta_ref[h]).astype(BF16)
    upd = _dot_tn(kz, vb)
    state_ref[h] = math.exp(_ret_log_gamma(h) * SUPER) * st + upd
    o = intra + cross
    mu = jnp.mean(o, axis=-1, keepdims=True)
    oc = o - mu
    var = jnp.mean(oc * oc, axis=-1, keepdims=True)
    on = oc * lax.rsqrt(var + EPS)
    gate = _silu(g_ref[:, sl].astype(F32))
    o_ref[:, sl] = (on * gate * ms_ref[:, sl]).astype(BF16)


def _gla_log_alpha(a_ref, wa_ref, ba_ref):
    pre = _dot(a_ref[...], wa_ref[...]) + ba_ref[...]
    return (jnp.minimum(pre, 0.0) - jnp.log(1.0 + jnp.exp(-jnp.abs(pre)))) * (1.0 / GLA_TAU)


def _gla_chunk(ci, la, q_ref, k_ref, v_ref, g_ref, ms_ref, o_ref, state_ref):
    r = lax.broadcasted_iota(I32, (CHUNK, CHUNK), 0)
    c = lax.broadcasted_iota(I32, (CHUNK, CHUNK), 1)
    causal = c <= r
    tril = jnp.where(causal, 1.0, 0.0).astype(BF16)
    ones = jnp.ones((CHUNK, GLA_DV), BF16)
    scale = GLA_DK ** -0.5
    rows = slice(ci * CHUNK, (ci + 1) * CHUNK)
    la_c = la[rows, :]
    hi = la_c.astype(BF16)
    lo = (la_c - hi.astype(F32)).astype(BF16)
    bcs = _dot(tril, hi) + _dot(tril, lo)
    gl = _dot_tn(hi, ones) + _dot_tn(lo, ones)
    blast = bcs[CHUNK - 1:CHUNK, :]
    eb = jnp.exp(bcs)
    enb = jnp.exp(-bcs)
    ekb = jnp.exp(blast - bcs)
    qc = q_ref[rows, :].astype(F32) * scale * eb
    kc = k_ref[rows, :].astype(F32)
    kin = (kc * enb).astype(BF16)
    kup = (kc * ekb).astype(BF16)
    qin = qc.astype(BF16)
    for h in range(GLA_HEADS):
        ks = slice(h * GLA_DK, (h + 1) * GLA_DK)
        vs = slice(h * GLA_DV, (h + 1) * GLA_DV)
        vb = v_ref[rows, vs]
        a = jnp.where(causal, _dot_nt(qin[:, ks], kin[:, ks]), 0.0)
        intra = _dot(a.astype(BF16), vb)
        st = state_ref[h]
        cross = _dot(qin[:, ks], st.astype(BF16))
        upd = _dot_tn(kup[:, ks], vb)
        state_ref[h] = jnp.exp(gl[ks, :]) * st + upd
        o = intra + cross
        on = o * lax.rsqrt(jnp.mean(o * o, axis=-1, keepdims=True) + EPS)
        gate = _silu(g_ref[rows, vs].astype(F32))
        o_ref[rows, vs] = (on * gate * ms_ref[:, vs]).astype(BF16)


BAND = 3 * SUPER


BAND_KINDS = 3
ROLL_W = 1024
NEG_INF = -1e30


def _bias_kernel(rb_ref, o_ref):
    l = pl.program_id(0)
    h = pl.program_id(1)
    m = lax.broadcasted_iota(I32, (8, ROLL_W), 1)
    d = jnp.where(m < BAND, m, m - ROLL_W)
    idx = jnp.clip(2 * SUPER - d, -MAX_REL, MAX_REL) + MAX_REL

    def body(j, acc):
        return jnp.where(idx == j, rb_ref[l, h, j], acc)

    diag = lax.fori_loop(0, 2 * MAX_REL + 1, body, jnp.zeros((8, ROLL_W), F32))
    full = pltpu.roll(jnp.broadcast_to(diag[0:1, :], (SUPER, ROLL_W)), 0, 1, stride=1, stride_axis=0)
    bias = full[:, :BAND]
    row = lax.broadcasted_iota(I32, (SUPER, BAND), 0)
    col = lax.broadcasted_iota(I32, (SUPER, BAND), 1)
    dist = 2 * (SUPER // CHUNK) + row // CHUNK - col // CHUNK
    in_band = (dist >= 0) & (dist <= ATT_BAND_CHUNKS)
    for kind in range(BAND_KINDS):
        first_valid_col = (BAND_KINDS - 1 - kind) * SUPER
        o_ref[kind, 0] = jnp.where(in_band & (col >= first_valid_col), bias, NEG_INF)


def _att_bias(rel_bias):
    depth = rel_bias.shape[0]
    return pl.pallas_call(
        _bias_kernel,
        grid=(depth, ATT_HEADS),
        in_specs=[pl.BlockSpec(memory_space=pltpu.SMEM)],
        out_specs=pl.BlockSpec((None, BAND_KINDS, 1, SUPER, BAND), lambda l, h: (l, 0, h, 0, 0)),
        out_shape=jax.ShapeDtypeStruct((depth, BAND_KINDS, ATT_HEADS, SUPER, BAND), F32),
        compiler_params=_cparams(2),
        name="att_bias",
    )(rel_bias)


def _att_head(h, q_ref, k_refs, v_refs, bias_ref, ms_ref, o_ref):
    sl = slice(h * HEAD_DIM, (h + 1) * HEAD_DIM)
    scale = HEAD_DIM ** -0.5
    qb = (q_ref[:, sl].astype(F32) * scale).astype(BF16)
    sc = [_dot_nt(qb, k_refs[j][:, sl]) + bias_ref[h, :, j * SUPER:(j + 1) * SUPER] for j in range(3)]
    m = jnp.max(jnp.maximum(jnp.maximum(sc[0], sc[1]), sc[2]), axis=-1, keepdims=True)
    ps = [jnp.exp(sj - m) for sj in sc]
    den = jnp.sum(ps[0] + ps[1] + ps[2], axis=-1, keepdims=True)
    acc = _dot(ps[0].astype(BF16), v_refs[0][:, sl])
    for j in (1, 2):
        acc = acc + _dot(ps[j].astype(BF16), v_refs[j][:, sl])
    o = acc / den
    on = o * lax.rsqrt(jnp.mean(o * o, axis=-1, keepdims=True) + EPS)
    o_ref[:, sl] = (on * ms_ref[:, sl]).astype(BF16)


def _mixer_kernel(rq_ref, rk_ref, rv_ref, rg_ref, cos_ref, sin_ref, rms_ref,
                  gq_ref, gk_ref, gv_ref, gg_ref, ga_ref, wa_ref, ba_ref, gms_ref,
                  aq_ref, k0_ref, k1_ref, k2_ref, v0_ref, v1_ref, v2_ref, bias_ref, ams_ref,
                  oret_ref, ogla_ref, oatt_ref,
                  rstate_ref, dmat_ref, xi_ref, zeta_ref, gstate_ref):
    @pl.when(pl.program_id(1) == 0)
    def _():
        _ret_init(rstate_ref, dmat_ref, xi_ref, zeta_ref)
        gstate_ref[...] = jnp.zeros_like(gstate_ref)

    cos = cos_ref[...]
    sin = sin_ref[...]
    la = _gla_log_alpha(ga_ref, wa_ref, ba_ref)
    k_refs = (k0_ref, k1_ref, k2_ref)
    v_refs = (v0_ref, v1_ref, v2_ref)
    n_chunks = SUPER // CHUNK
    for h in range(max(RET_HEADS, ATT_HEADS)):
        if h < RET_HEADS:
            _ret_head(h, rq_ref, rk_ref, rv_ref, rg_ref, cos, sin, rms_ref, oret_ref,
                      rstate_ref, dmat_ref, xi_ref, zeta_ref)
        if h < ATT_HEADS:
            _att_head(h, aq_ref, k_refs, v_refs, bias_ref, ams_ref, oatt_ref)
        if h < n_chunks:
            _gla_chunk(h, la, gq_ref, gk_ref, gv_ref, gg_ref, gms_ref, ogla_ref, gstate_ref)


def _mixers(z, cos2, sin2, wa_all, ba, bias, ms, layer, batch, seq):
    t = z.shape[0]
    ns = seq // SUPER
    row = lambda b, s: b * ns + s
    zspec = lambda width, off: pl.BlockSpec((SUPER, width), lambda b, s: (row(b, s), off // width))
    const = lambda width: pl.BlockSpec((1, width), lambda b, s: (0, 0))

    def kv_spec(off, back):
        return pl.BlockSpec((SUPER, ATT_W), lambda b, s: (b * ns + jnp.maximum(s - back, 0), off // ATT_W))

    ak, av = Z_ATT + ATT_W, Z_ATT + 2 * ATT_W
    return pl.pallas_call(
        _mixer_kernel,
        grid=(batch, ns),
        in_specs=[
            zspec(RET_W, 0), zspec(RET_W, RET_W), zspec(RET_W, 2 * RET_W), zspec(RET_W, 3 * RET_W),
            pl.BlockSpec((SUPER, LANES), lambda b, s: (s, 0)),
            pl.BlockSpec((SUPER, LANES), lambda b, s: (s, 0)),
            const(RET_W),
            zspec(GLA_QK, Z_GQ), zspec(GLA_QK, Z_GK), zspec(GLA_V, Z_GV), zspec(GLA_V, Z_GG),
            zspec(LANES, Z_GA),
            pl.BlockSpec((None, LANES, GLA_QK), lambda b, s: (layer, 0, 0)),
            const(GLA_QK), const(GLA_V),
            zspec(ATT_W, Z_ATT),
            kv_spec(ak, 2), kv_spec(ak, 1), kv_spec(ak, 0),
            kv_spec(av, 2), kv_spec(av, 1), kv_spec(av, 0),
            pl.BlockSpec((None, None, ATT_HEADS, SUPER, BAND),
                         lambda b, s: (layer, jnp.minimum(s, BAND_KINDS - 1), 0, 0, 0)),
            const(ATT_W),
        ],
        out_specs=[
            pl.BlockSpec((SUPER, RET_W), lambda b, s: (row(b, s), 0)),
            pl.BlockSpec((SUPER, GLA_V), lambda b, s: (row(b, s), 0)),
            pl.BlockSpec((SUPER, ATT_W), lambda b, s: (row(b, s), 0)),
        ],
        out_shape=[
            jax.ShapeDtypeStruct((t, RET_W), BF16),
            jax.ShapeDtypeStruct((t, GLA_V), BF16),
            jax.ShapeDtypeStruct((t, ATT_W), BF16),
        ],
        scratch_shapes=[
            pltpu.VMEM((RET_HEADS, HEAD_DIM, HEAD_DIM), F32),
            pltpu.VMEM((RET_HEADS, SUPER, SUPER), F32),
            pltpu.VMEM((RET_HEADS, SUPER, HEAD_DIM), F32),
            pltpu.VMEM((RET_HEADS, SUPER, HEAD_DIM), F32),
            pltpu.VMEM((GLA_HEADS, GLA_DK, GLA_DV), F32),
        ],
        compiler_params=_cparams(2),
        name="mixers",
    )(z, z, z, z, cos2, sin2, ms[:RET_W].reshape(1, RET_W),
      z, z, z, z, z, wa_all, ba.reshape(1, GLA_QK), ms[RET_W:RET_W + GLA_V].reshape(1, GLA_V),
      z, z, z, z, z, z, z, bias, ms[RET_W + GLA_V:].reshape(1, ATT_W))


def _store_token_tiles(ref, packed):
    tm = packed.shape[0]
    for s in range(ROW_SUB):
        ref[pl.ds(s, tm, stride=ROW_SUB), :] = packed[:, s * LANES:(s + 1) * LANES]


def _load_token_tiles(ref, tm):
    return jnp.concatenate([ref[pl.ds(s, tm, stride=ROW_SUB), :] for s in range(ROW_SUB)], axis=1)


def _tile_rows(ref, row, n_rows=1):
    start = row * ROW_SUB
    if not isinstance(start, int):
        start = pl.multiple_of(start, ROW_SUB)
    return ref.at[pl.ds(start, n_rows * ROW_SUB), :]


def _route_kernel(x_ref, a_ref, b_ref, c_ref, w_ref, g_ref, wr_ref, br_ref,
                  x1_ref, hp_ref, meta_ref, metat_ref, cnt_ref, stage_ref, run_ref):
    i = pl.program_id(0)
    tm = x_ref.shape[0]
    half = x_ref.shape[1] // 2

    @pl.when(i == 0)
    def _():
        run_ref[...] = jnp.zeros_like(run_ref)
        stage_ref[...] = jnp.zeros_like(stage_ref)

    xr = stage_ref[...]
    live = jnp.where(i > 0, 1.0, 0.0)

    h = _rms(xr, g_ref[...])
    _store_token_tiles(hp_ref, _pack_bf16_pair(h[:, :half], h[:, half:]))
    lg = _dot(h.astype(BF16), wr_ref[...]) + br_ref[...]
    lane = lax.broadcasted_iota(I32, (tm, LANES), 1)
    neg = jnp.float32(-1e30)
    big = jnp.int32(LANES)
    gl = jnp.where(lane < N_GROUPS, lg, neg)
    gm = jnp.max(gl, axis=-1, keepdims=True)
    p_group = 1.0 / jnp.sum(jnp.exp(gl - gm), axis=-1, keepdims=True)
    g_idx = jnp.min(jnp.where(gl == gm, lane, big), axis=-1, keepdims=True)
    lo = ROUTE_LANE0 + EXPERTS_PER_GROUP * g_idx
    el = jnp.where((lane >= lo) & (lane < lo + EXPERTS_PER_GROUP), lg, neg)
    m1 = jnp.max(el, axis=-1, keepdims=True)
    i1 = jnp.min(jnp.where(el == m1, lane, big), axis=-1, keepdims=True)
    el2 = jnp.where(lane == i1, neg, el)
    m2 = jnp.max(el2, axis=-1, keepdims=True)
    i2 = jnp.min(jnp.where(el2 == m2, lane, big), axis=-1, keepdims=True)
    e2 = jnp.exp(m2 - m1)
    c1 = p_group / (1.0 + e2)
    c2 = p_group * e2 / (1.0 + e2)
    oh1 = jnp.where(lane == i1, 1.0, 0.0)
    oh2 = jnp.where(lane == i2, 1.0, 0.0)
    oh = (oh1 + oh2) * live
    r = lax.broadcasted_iota(I32, (tm, tm), 0)
    c = lax.broadcasted_iota(I32, (tm, tm), 1)
    stril = jnp.where(c < r, 1.0, 0.0).astype(BF16)
    before = _dot(stril, oh.astype(BF16)) + run_ref[0:1, :]
    rank1 = jnp.sum(before * oh1, axis=-1, keepdims=True)
    rank2 = jnp.sum(before * oh2, axis=-1, keepdims=True)
    run_ref[0:1, :] = run_ref[0:1, :] + jnp.sum(oh, axis=0, keepdims=True)
    cnt_ref[...] = run_ref[...]
    e1f = (i1 - ROUTE_LANE0).astype(F32)
    e2f = (i2 - ROUTE_LANE0).astype(F32)
    meta = jnp.where(lane == 0, e1f, 0.0)
    meta = jnp.where(lane == 1, e2f, meta)
    meta = jnp.where(lane == 2, rank1, meta)
    meta = jnp.where(lane == 3, rank2, meta)
    meta = jnp.where(lane == 4, c1, meta)
    meta = jnp.where(lane == 5, c2, meta)
    meta_ref[...] = meta
    metat_ref[0] = meta.T[0:8, :]

    x1 = (x_ref[...] + _dot(a_ref[...], w_ref[:RET_W, :]) + _dot(b_ref[...], w_ref[RET_W:RET_W + GLA_V, :])
          + _dot(c_ref[...], w_ref[RET_W + GLA_V:, :]))
    x1_ref[...] = x1
    stage_ref[...] = x1


def _out_route(x, o_ret, o_gla, o_att, w_all, g, wr_all, br_all, layer):
    t, d = x.shape
    tm = min(t, TOK_TILE)
    nt = t // tm
    cur = lambda w: pl.BlockSpec((tm, w), lambda i: (jnp.minimum(i, nt - 1), 0))
    prev = lambda i: jnp.maximum(i - 1, 0)
    return pl.pallas_call(
        _route_kernel,
        grid=(nt + 1,),
        in_specs=[
            cur(d), cur(RET_W), cur(GLA_V), cur(ATT_W),
            pl.BlockSpec((None, w_all.shape[1], d), lambda i: (layer, 0, 0), pipeline_mode=pl.Buffered(1)),
            pl.BlockSpec((1, d), lambda i: (0, 0)),
            pl.BlockSpec((None, d, LANES), lambda i: (layer, 0, 0)),
            pl.BlockSpec((None, 1, LANES), lambda i: (layer, 0, 0)),
        ],
        out_specs=[
            cur(d),
            pl.BlockSpec((tm * ROW_SUB, LANES), lambda i: (prev(i), 0)),
            pl.BlockSpec((tm, LANES), lambda i: (prev(i), 0)),
            pl.BlockSpec((1, 8, tm), lambda i: (prev(i), 0, 0)),
            pl.BlockSpec((8, LANES), lambda i: (0, 0)),
        ],
        out_shape=[
            jax.ShapeDtypeStruct((t, d), F32),
            jax.ShapeDtypeStruct((t * ROW_SUB, LANES), U32),
            jax.ShapeDtypeStruct((t, LANES), F32),
            jax.ShapeDtypeStruct((nt, 8, tm), F32),
            jax.ShapeDtypeStruct((8, LANES), F32),
        ],
        scratch_shapes=[pltpu.VMEM((tm, d), F32), pltpu.VMEM((8, LANES), F32)],
        compiler_params=_cparams(1),
        name="out_proj_router",
    )(x, o_ret, o_gla, o_att, w_all, g.reshape(1, d), wr_all, br_all)


_PAD_BITS = tuple(reversed(range(MOE_TILE.bit_length() - 1)))


def _dispatch_kernel(ps_ref, pn_ref, nu_ref, pos_ref, hp_ref, xs_ref, stage_ref, zero_ref, lsem_ref, sem_ref,
                     zsem_ref):
    i = pl.program_id(0)
    tm = pos_ref.shape[2] // 2
    zrows = zero_ref.shape[0] // ROW_SUB
    n_tiles = xs_ref.shape[0] // (ROW_SUB * MOE_TILE)

    def pad_copies(start):
        def zero_copy(off, k):
            cp = pltpu.make_async_copy(_tile_rows(zero_ref, 0, k), _tile_rows(xs_ref, off, k), zsem_ref.at[0])
            if start:
                cp.start()
            else:
                cp.wait()

        def body(e, carry):
            n = pn_ref[e]
            base = ps_ref[e]
            for bit in _PAD_BITS:
                k = 1 << bit

                @pl.when((n & k) != 0)
                def _():
                    zero_copy(base + ((n >> (bit + 1)) << (bit + 1)), k)
            return carry

        lax.fori_loop(0, N_EXPERTS, body, 0)

        def tail(tile, carry):
            for part in range(MOE_TILE // zrows):
                zero_copy(tile * MOE_TILE + part * zrows, zrows)
            return carry

        lax.fori_loop(nu_ref[0], n_tiles, tail, 0)

    n = pl.num_programs(0)

    def tile_load(tile, slot):
        return pltpu.make_async_copy(_tile_rows(hp_ref, tile * tm, tm), stage_ref.at[slot], lsem_ref.at[slot])

    @pl.when(i == 0)
    def _():
        zero_ref[...] = jnp.zeros_like(zero_ref)
        pad_copies(True)
        pad_copies(False)
        tile_load(0, 0).start()

    @pl.when(i + 1 < n)
    def _():
        tile_load(i + 1, (i + 1) % 3).start()

    tile_load(i, i % 3).wait()
    src = stage_ref.at[i % 3]
    for j in range(2 * tm):
        pltpu.make_async_copy(_tile_rows(src, j % tm), _tile_rows(xs_ref, pos_ref[0, 0, j]),
                              sem_ref.at[i % 2]).start(priority=j % 2)

    def wait_rows(slot):
        for j in range(2 * tm):
            pltpu.make_async_copy(_tile_rows(src, 0), _tile_rows(xs_ref, 0), sem_ref.at[slot]).wait()

    @pl.when(i > 0)
    def _():
        wait_rows((i + 1) % 2)

    @pl.when(i == n - 1)
    def _():
        wait_rows(i % 2)


def _dispatch(hp, pos_tiles, pad_start, pad_n, n_used, n_rows):
    nt = pos_tiles.shape[0]
    tm = pos_tiles.shape[2] // 2
    grid_spec = pltpu.PrefetchScalarGridSpec(
        num_scalar_prefetch=3,
        grid=(nt,),
        in_specs=[
            pl.BlockSpec((1, 1, 2 * tm), lambda i, ps, pn, nu: (i, 0, 0), memory_space=pltpu.SMEM),
            pl.BlockSpec(memory_space=pl.ANY),
        ],
        out_specs=pl.BlockSpec(memory_space=pl.ANY),
        scratch_shapes=[
            pltpu.VMEM((3, tm * ROW_SUB, LANES), U32),
            pltpu.VMEM((MOE_TILE // 2 * ROW_SUB, LANES), U32),
            pltpu.SemaphoreType.DMA((3,)),
            pltpu.SemaphoreType.DMA((2,)),
            pltpu.SemaphoreType.DMA((1,)),
        ],
    )
    return pl.pallas_call(
        _dispatch_kernel,
        grid_spec=grid_spec,
        out_shape=jax.ShapeDtypeStruct((n_rows * ROW_SUB, LANES), U32),
        compiler_params=_cparams(1),
        name="dispatch",
    )(pad_start, pad_n, n_used, pos_tiles, hp)


def _moe_kernel(te_ref, nu_ref, xs_ref, wg_ref, wu_ref, wd_ref, y_ref, wgb_ref, wub_ref, wdb_ref):
    i = pl.program_id(0)
    n_used = nu_ref[0]
    tm = xs_ref.shape[0] // ROW_SUB
    half = ROW_SUB * LANES
    e = te_ref[i]
    e_prev = te_ref[jnp.maximum(i - 1, 0)]

    @pl.when((i == 0) | (e != e_prev))
    def _():
        wgb_ref[...] = wg_ref[...].astype(BF16)
        wub_ref[...] = wu_ref[...].astype(BF16)
        wdb_ref[...] = wd_ref[...].astype(BF16)

    @pl.when(i < n_used)
    def _():
        lo, hi = _unpack_bf16_pair(_load_token_tiles(xs_ref, tm))
        xl = lo.astype(BF16)
        xh = hi.astype(BF16)
        a = _dot(xl, wgb_ref[:half, :]) + _dot(xh, wgb_ref[half:, :])
        u = _dot(xl, wub_ref[:half, :]) + _dot(xh, wub_ref[half:, :])
        hid = (_silu(a) * u).astype(BF16)
        y = _dot(hid, wdb_ref[...])
        _store_token_tiles(y_ref, _pack_bf16_pair(y[:, :half], y[:, half:]))

    @pl.when(i >= n_used)
    def _():
        y_ref[...] = jnp.zeros_like(y_ref)


def _moe(xs, tile_expert, n_used, wg, wu, wd, layer):
    nt = tile_expert.shape[0]
    tm = xs.shape[0] // ROW_SUB // nt
    d, ff = wg.shape[-2], wg.shape[-1]

    def w_spec(shape):
        return pl.BlockSpec((None, None, None) + shape,
                            lambda i, te, nu: (layer, te[i] // EXPERTS_PER_GROUP, te[i] % EXPERTS_PER_GROUP, 0, 0))

    grid_spec = pltpu.PrefetchScalarGridSpec(
        num_scalar_prefetch=2,
        grid=(nt,),
        in_specs=[
            pl.BlockSpec((tm * ROW_SUB, LANES), lambda i, te, nu: (jnp.minimum(i, nu[0] - 1), 0)),
            w_spec((d, ff)), w_spec((d, ff)), w_spec((ff, d)),
        ],
        out_specs=pl.BlockSpec((tm * ROW_SUB, LANES), lambda i, te, nu: (i, 0)),
        scratch_shapes=[
            pltpu.VMEM((d, ff), BF16),
            pltpu.VMEM((d, ff), BF16),
            pltpu.VMEM((ff, d), BF16),
        ],
    )
    return pl.pallas_call(
        _moe_kernel,
        grid_spec=grid_spec,
        out_shape=jax.ShapeDtypeStruct(xs.shape, U32),
        compiler_params=_cparams(1),
        name="expert_mlp",
    )(tile_expert, n_used, xs, wg, wu, wd)


def _ple_kernel(pos_ref, x_ref, meta_ref, y_ref, p_ref, g_ref, wg_ref, wp_ref, gn_ref, *rest, final):
    if final:
        o_ref, buf_ref, sem_ref = rest
    else:
        o_ref, hn_ref, buf_ref, sem_ref = rest
    i = pl.program_id(0)
    n = pl.num_programs(0) - 1
    tm = x_ref.shape[0]
    half = x_ref.shape[1] // 2

    for par in range(2):
        @pl.when((i < n) & (i % 2 == par))
        def _():
            for j in range(2 * tm):
                pltpu.make_async_copy(_tile_rows(y_ref, pos_ref[0, 0, j]), _tile_rows(buf_ref.at[par], j),
                                      sem_ref.at[par]).start(priority=j % 2)

    @pl.when(i > 0)
    def _():
        slot = (i + 1) % 2
        for j in range(2 * tm):
            pltpu.make_async_copy(_tile_rows(y_ref, 0), _tile_rows(buf_ref.at[slot], j), sem_ref.at[slot]).wait()
        rows = _load_token_tiles(buf_ref.at[slot], 2 * tm)
        meta = meta_ref[...]
        c1 = meta[:, 4:5]
        c2 = meta[:, 5:6]
        lo1, hi1 = _unpack_bf16_pair(rows[:tm])
        lo2, hi2 = _unpack_bf16_pair(rows[tm:])
        xl = x_ref[:, :half] + c1 * lo1 + c2 * lo2
        xh = x_ref[:, half:] + c1 * hi1 + c2 * hi2
        ms = (jnp.sum(xl * xl, axis=-1, keepdims=True) + jnp.sum(xh * xh, axis=-1, keepdims=True)) / (2 * half)
        inv = lax.rsqrt(ms + EPS)
        hl = (xl * inv * g_ref[:, :half]).astype(BF16)
        hh = (xh * inv * g_ref[:, half:]).astype(BF16)
        gate = _dot(hl, wg_ref[:half, :]) + _dot(hh, wg_ref[half:, :])
        gate = 1.0 / (1.0 + jnp.exp(-gate))
        pp = _dot(p_ref[...].astype(BF16), wp_ref[...])
        ol = xl + gate[:, :half] * pp[:, :half]
        oh = xh + gate[:, half:] * pp[:, half:]
        ms2 = (jnp.sum(ol * ol, axis=-1, keepdims=True) + jnp.sum(oh * oh, axis=-1, keepdims=True)) / (2 * half)
        inv2 = lax.rsqrt(ms2 + EPS)
        nl = ol * inv2 * gn_ref[:, :half]
        nh = oh * inv2 * gn_ref[:, half:]
        if final:
            o_ref[:, :half] = nl
            o_ref[:, half:] = nh
        else:
            o_ref[:, :half] = ol
            o_ref[:, half:] = oh
            hn_ref[:, :half] = nl.astype(BF16)
            hn_ref[:, half:] = nh.astype(BF16)


def _combine_ple(x, meta, pos_tiles, ys, p_all, g_ple, w_gate_all, w_proj_all, layer, g_next, final):
    t, d = x.shape
    nt = pos_tiles.shape[0]
    tm = pos_tiles.shape[2] // 2
    pd = p_all.shape[2]
    const = lambda shape: pl.BlockSpec(shape, lambda i: (0, 0))
    prev = lambda w: pl.BlockSpec((tm, w), lambda i: (jnp.maximum(i - 1, 0), 0))
    return pl.pallas_call(
        functools.partial(_ple_kernel, final=final),
        grid=(nt + 1,),
        in_specs=[
            pl.BlockSpec((1, 1, 2 * tm), lambda i: (jnp.minimum(i, nt - 1), 0, 0), memory_space=pltpu.SMEM),
            prev(d), prev(LANES),
            pl.BlockSpec(memory_space=pl.ANY),
            pl.BlockSpec((None, tm, pd), lambda i: (layer, jnp.maximum(i - 1, 0), 0)),
            const((1, d)),
            pl.BlockSpec((None, d, d), lambda i: (layer, 0, 0)),
            pl.BlockSpec((None, pd, d), lambda i: (layer, 0, 0)),
            const((1, d)),
        ],
        out_specs=[prev(d)] if final else [prev(d), prev(d)],
        out_shape=[jax.ShapeDtypeStruct((t, d), F32)] + ([] if final else [jax.ShapeDtypeStruct((t, d), BF16)]),
        scratch_shapes=[pltpu.VMEM((2, 2 * tm * ROW_SUB, LANES), U32), pltpu.SemaphoreType.DMA((2,))],
        compiler_params=_cparams(1),
        name="combine_ple",
    )(pos_tiles, x, meta, ys, p_all, g_ple.reshape(1, d), w_gate_all, w_proj_all, g_next.reshape(1, d))


W_GA0 = 4 * RET_W + 2 * GLA_QK + 2 * GLA_V


PREP_COLS = 256
PREP_ALIGNED = W_GA0 // PREP_COLS
PREP_SHIFTED = (Z_GA - W_GA0) // PREP_COLS


def _prep_kernel(wt_ref, o_ref):
    j = pl.program_id(1)
    t = wt_ref[0].T

    @pl.when(j < PREP_ALIGNED + PREP_SHIFTED)
    def _():
        o_ref[...] = t.astype(BF16)

    @pl.when(j >= PREP_ALIGNED + PREP_SHIFTED)
    def _():
        lane = lax.broadcasted_iota(I32, t.shape, 1)
        o_ref[...] = jnp.where(lane < GLA_RANK, t, 0.0).astype(BF16)


def _prep_w_in(w):
    depth, d, n = w.shape
    wt = jnp.transpose(w, (0, 2, 1))

    def src_row(j):
        shifted = (W_GA0 + GLA_RANK) // 8 + (j - PREP_ALIGNED) * (PREP_COLS // 8)
        r8 = jnp.where(j < PREP_ALIGNED, j * (PREP_COLS // 8),
                       jnp.where(j < PREP_ALIGNED + PREP_SHIFTED, shifted, W_GA0 // 8))
        return r8 * 8

    return pl.pallas_call(
        _prep_kernel,
        grid=(depth, Z_W // PREP_COLS),
        in_specs=[pl.BlockSpec((pl.Element(1), pl.Element(PREP_COLS), pl.Element(d)),
                               lambda l, j: (l, src_row(j), 0))],
        out_specs=pl.BlockSpec((None, d, PREP_COLS), lambda l, j: (l, 0, j)),
        out_shape=jax.ShapeDtypeStruct((depth, d, Z_W), BF16),
        compiler_params=_cparams(2),
        name="w_in_layout",
    )(wt)


def _pad_lanes(parts, width, dtype):
    cat = jnp.concatenate(parts, axis=-1)
    pad = jnp.zeros(cat.shape[:-1] + (width - cat.shape[-1],), cat.dtype)
    return jnp.concatenate([cat, pad], axis=-1).astype(dtype)


def _routing_tables(meta_t, counts, n_tok, tile):
    eid = meta_t[:, 0:2, :].astype(I32)
    rank = meta_t[:, 2:4, :].astype(I32)
    cnt = counts[0, ROUTE_LANE0:ROUTE_LANE0 + N_EXPERTS].astype(I32)
    padded = ((cnt + tile - 1) // tile) * tile
    ends = jnp.cumsum(padded)
    offs = ends - padded
    base = jnp.zeros_like(eid)
    for e in range(N_EXPERTS):
        base = jnp.where(eid == e, offs[e], base)
    pos = base + rank
    n_rows = 2 * n_tok + N_EXPERTS * tile
    nt = n_rows // tile
    tile_start = jnp.arange(nt, dtype=I32) * tile
    tile_expert = jnp.minimum(jnp.sum((tile_start[:, None] >= ends[None, :]).astype(I32), axis=1),
                              N_EXPERTS - 1)
    n_used = (ends[-1] // tile).reshape(1)
    return pos, tile_expert, n_used, offs + cnt, padded - cnt, n_rows


def kernel(x, p, g_mix, w_in, gla_w_alpha, gla_b_alpha, rel_bias, mix_scale, w_out, g_ffn,
           w_router_group, b_router_group, w_router_expert, b_router_expert,
           w_expert_gate, w_expert_up, w_expert_down, g_ple, w_ple_gate, w_ple_proj, g_final):
    batch, seq, d = x.shape
    depth = g_mix.shape[0]
    t = batch * seq
    assert seq % SUPER == 0 and d == 2 * ROW_SUB * LANES
    tok_tile = min(t, TOK_TILE)
    ple_tile = min(t, PLE_TILE)
    xf = x.reshape(t, d)
    cos2, sin2 = _rotary_tables(seq)
    w_in_r = _prep_w_in(w_in)
    w_out_b = w_out.astype(BF16)
    wa_all = jnp.concatenate(
        [gla_w_alpha, jnp.zeros((depth, LANES - GLA_RANK, GLA_QK), gla_w_alpha.dtype)], axis=1).astype(BF16)
    wr_all = _pad_lanes([w_router_group, w_router_expert.reshape(depth, d, N_EXPERTS)], LANES, BF16)
    br_all = _pad_lanes([b_router_group, b_router_expert.reshape(depth, N_EXPERTS)], LANES, F32)
    br_all = br_all.reshape(depth, 1, LANES)
    p_all = p.reshape(depth, t, p.shape[-1])
    w_pg_b = w_ple_gate.astype(BF16)
    w_pp_b = w_ple_proj.astype(BF16)
    bias_all = _att_bias(rel_bias)
    h = _norm_cast(xf, g_mix[0])
    for i in range(depth):
        z = _in_proj(h, w_in_r, i)
        ms = mix_scale[i]
        o_ret, o_gla, o_att = _mixers(z, cos2, sin2, wa_all, gla_b_alpha[i], bias_all, ms, i, batch, seq)
        x1, hp, meta, meta_t, counts = _out_route(xf, o_ret, o_gla, o_att, w_out_b, g_ffn[i], wr_all, br_all, i)
        pos, tile_expert, n_used, pad_start, pad_n, n_rows = _routing_tables(meta_t, counts, t, MOE_TILE)
        xs = _dispatch(hp, pos.reshape(t // tok_tile, 1, 2 * tok_tile), pad_start, pad_n, n_used, n_rows)
        ys = _moe(xs, tile_expert, n_used, w_expert_gate, w_expert_up, w_expert_down, i)
        r = tok_tile // ple_tile
        pos_ple = pos.reshape(t // tok_tile, 2, r, ple_tile).transpose(0, 2, 1, 3).reshape(t // ple_tile, 1, 2 * ple_tile)
        final = i == depth - 1
        outs = _combine_ple(x1, meta, pos_ple, ys, p_all, g_ple[i], w_pg_b, w_pp_b, i,
                            g_final if final else g_mix[i + 1], final=final)
        xf = outs[0]
        if not final:
            h = outs[1]
    return xf.reshape(batch, seq, d)
```

```python
import functools
import math

import numpy as np
import jax
import jax.numpy as jnp
from jax import lax
from jax.experimental import pallas as pl
from jax.experimental.pallas import tpu as pltpu

F32 = jnp.float32
BF16 = jnp.bfloat16
U32 = jnp.uint32
I32 = jnp.int32

CHUNK = 64
HEAD_DIM = 128
RET_HEADS = 6
GLA_HEADS = 4
GLA_DK = 64
GLA_DV = 128
GLA_RANK = 16
GLA_TAU = 16.0
ATT_HEADS = 6
ATT_BAND_CHUNKS = 8
MAX_REL = 128
N_GROUPS = 4
EXPERTS_PER_GROUP = 8
N_EXPERTS = N_GROUPS * EXPERTS_PER_GROUP
EXPERT_FF = 256
EPS = 1e-6

RET_W = RET_HEADS * HEAD_DIM
GLA_QK = GLA_HEADS * GLA_DK
GLA_V = GLA_HEADS * GLA_DV
ATT_W = ATT_HEADS * HEAD_DIM

LANES = 128
V7X_VMEM_LIMIT = 56 * 1024 * 1024

Z_RET = 0
Z_GQ = 4 * RET_W
Z_GK = Z_GQ + GLA_QK
Z_GV = Z_GK + GLA_QK
Z_GG = Z_GV + GLA_V
Z_ATT = Z_GG + GLA_V
Z_GA = Z_ATT + 3 * ATT_W
Z_W = 7 * 1024

SUPER = 4 * CHUNK
ROUTE_LANE0 = N_GROUPS
MOE_TILE = 512
TOK_TILE = 512
PLE_TILE = 256
ROW_SUB = 8
PROJ_ROWS = 2048
PROJ_COLS = 1024
NORM_TILE = 512
ROT_ROWS = 1024


def _cparams(n_axes):
    return pltpu.CompilerParams(
        dimension_semantics=("arbitrary",) * n_axes,
        vmem_limit_bytes=V7X_VMEM_LIMIT,
    )


def _dot(a, b):
    return jnp.dot(a, b, preferred_element_type=F32)


def _dot_nt(a, b):
    return lax.dot_general(a, b, (((1,), (1,)), ((), ())), preferred_element_type=F32)


def _dot_tn(a, b):
    return lax.dot_general(a, b, (((0,), (0,)), ((), ())), preferred_element_type=F32)


def _rms(x, g):
    return x * lax.rsqrt(jnp.mean(x * x, axis=-1, keepdims=True) + EPS) * g


def _silu(x):
    return x / (1.0 + jnp.exp(-x))


def _pack_bf16_pair(lo, hi):
    lo_b = lax.bitcast_convert_type(lo.astype(BF16).astype(F32), U32)
    hi_b = lax.bitcast_convert_type(hi.astype(BF16).astype(F32), U32)
    return (lo_b >> 16) | (hi_b & jnp.uint32(0xFFFF0000))


def _unpack_bf16_pair(w):
    lo = lax.bitcast_convert_type(w << 16, F32)
    hi = lax.bitcast_convert_type(w & jnp.uint32(0xFFFF0000), F32)
    return lo, hi


def _rot_kernel(inv_ref, cos_ref, sin_ref):
    rows = cos_ref.shape[0]
    pos = (lax.broadcasted_iota(I32, (rows, LANES), 0) + pl.program_id(0) * rows).astype(F32)
    lane = lax.broadcasted_iota(I32, (rows, LANES), 1)
    ang = pos * inv_ref[...]
    cos_ref[...] = jnp.cos(ang)
    s = jnp.sin(ang)
    sin_ref[...] = jnp.where(lane < HEAD_DIM // 2, -s, s)


def _rotary_tables(seq):
    half = HEAD_DIM // 2
    inv = np.float32(1.0) / (np.float32(10000.0) ** (np.arange(half, dtype=np.float32) / np.float32(half)))
    inv2 = jnp.asarray(np.concatenate([inv, inv]).reshape(1, LANES).astype(np.float32))
    rows = min(seq, ROT_ROWS)
    return pl.pallas_call(
        _rot_kernel,
        grid=(seq // rows,),
        in_specs=[pl.BlockSpec((1, LANES), lambda i: (0, 0))],
        out_specs=[pl.BlockSpec((rows, LANES), lambda i: (i, 0))] * 2,
        out_shape=[jax.ShapeDtypeStruct((seq, LANES), F32)] * 2,
        compiler_params=_cparams(1),
        name="rotary_tables",
    )(inv2)


def _norm_kernel(x_ref, g_ref, h_ref):
    h_ref[...] = _rms(x_ref[...], g_ref[...]).astype(BF16)


def _norm_cast(x, g):
    t, d = x.shape
    tm = min(t, NORM_TILE)
    return pl.pallas_call(
        _norm_kernel,
        grid=(t // tm,),
        in_specs=[pl.BlockSpec((tm, d), lambda i: (i, 0)), pl.BlockSpec((1, d), lambda i: (0, 0))],
        out_specs=pl.BlockSpec((tm, d), lambda i: (i, 0)),
        out_shape=jax.ShapeDtypeStruct((t, d), BF16),
        compiler_params=_cparams(1),
        name="norm_cast",
    )(x, g.reshape(1, d))


def _in_kernel(h_ref, w_ref, z_ref):
    z_ref[...] = _dot(h_ref[...], w_ref[...]).astype(BF16)


def _in_proj(h, w_all, layer):
    t, d = h.shape
    n = w_all.shape[2]
    tm = min(t, PROJ_ROWS)
    tn = min(n, PROJ_COLS)
    return pl.pallas_call(
        _in_kernel,
        grid=(t // tm, n // tn),
        in_specs=[
            pl.BlockSpec((tm, d), lambda i, j: (i, 0)),
            pl.BlockSpec((None, d, tn), lambda i, j: (layer, 0, j)),
        ],
        out_specs=pl.BlockSpec((tm, tn), lambda i, j: (i, j)),
        out_shape=jax.ShapeDtypeStruct((t, n), BF16),
        compiler_params=_cparams(2),
        name="in_proj",
    )(h, w_all)


def _ret_log_gamma(h):
    return math.log(1.0 - 2.0 ** (-5.0 - h))


def _ret_init(state_ref, dmat_ref, xi_ref, zeta_ref):
    state_ref[...] = jnp.zeros_like(state_ref)
    row = lax.broadcasted_iota(I32, (SUPER, SUPER), 0)
    col = lax.broadcasted_iota(I32, (SUPER, SUPER), 1)
    dist = jnp.abs(row - col).astype(F32)
    keep = (col <= row) | ((row // CHUNK) == (col // CHUNK))
    t = lax.broadcasted_iota(I32, (SUPER, HEAD_DIM), 0).astype(F32)
    for h in range(RET_HEADS):
        lg = _ret_log_gamma(h)
        dmat_ref[h] = jnp.where(keep, jnp.exp(lg * dist), 0.0)
        xi_ref[h] = jnp.exp(lg * (t + 1.0))
        zeta_ref[h] = jnp.exp(lg * (SUPER - 1.0 - t))


def _ret_head(h, q_ref, k_ref, v_ref, g_ref, cos, sin, ms_ref, o_ref, state_ref, dmat_ref, xi_ref, zeta_ref):
    sl = slice(h * HEAD_DIM, (h + 1) * HEAD_DIM)
    scale = HEAD_DIM ** -0.5
    q = q_ref[:, sl].astype(F32)
    k = k_ref[:, sl].astype(F32)
    vb = v_ref[:, sl]
    qr = q * cos + pltpu.roll(q, HEAD_DIM // 2, 1) * sin
    kr = (k * cos + pltpu.roll(k, HEAD_DIM // 2, 1) * sin) * scale
    qb = qr.astype(BF16)
    kb = kr.astype(BF16)
    sc = _dot_nt(qb, kb) * dmat_ref[h]
    intra = _dot(sc.astype(BF16), vb)
    st = state_ref[h]
    cross = _dot(qb, st.astype(BF16)) * xi_ref[h]
    kz = (kr * zeta_ref[h]).astype(BF16)
    upd = _dot_tn(kz, vb)
    state_ref[h] = math.exp(_ret_log_gamma(h) * SUPER) * st + upd
    o = intra + cross
    mu = jnp.mean(o, axis=-1, keepdims=True)
    oc = o - mu
    var = jnp.mean(oc * oc, axis=-1, keepdims=True)
    on = oc * lax.rsqrt(var + EPS)
    gate = _silu(g_ref[:, sl].astype(F32))
    o_ref[:, sl] = (on * gate * ms_ref[:, sl]).astype(BF16)


def _gla_log_alpha(a_ref, wa_ref, ba_ref):
    pre = _dot(a_ref[...], wa_ref[...]) + ba_ref[...]
    return (jnp.minimum(pre, 0.0) - jnp.log(1.0 + jnp.exp(-jnp.abs(pre)))) * (1.0 / GLA_TAU)


def _gla_chunk(ci, la, q_ref, k_ref, v_ref, g_ref, ms_ref, o_ref, state_ref):
    r = lax.broadcasted_iota(I32, (CHUNK, CHUNK), 0)
    c = lax.broadcasted_iota(I32, (CHUNK, CHUNK), 1)
    causal = c <= r
    tril = jnp.where(causal, 1.0, 0.0).astype(BF16)
    ones = jnp.ones((CHUNK, GLA_DV), BF16)
    scale = GLA_DK ** -0.5
    rows = slice(ci * CHUNK, (ci + 1) * CHUNK)
    la_c = la[rows, :]
    hi = la_c.astype(BF16)
    lo = (la_c - hi.astype(F32)).astype(BF16)
    bcs = _dot(tril, hi) + _dot(tril, lo)
    gl = _dot_tn(hi, ones) + _dot_tn(lo, ones)
    blast = bcs[CHUNK - 1:CHUNK, :]
    eb = jnp.exp(bcs)
    enb = jnp.exp(-bcs)
    ekb = jnp.exp(blast - bcs)
    qc = q_ref[rows, :].astype(F32) * scale * eb
    kc = k_ref[rows, :].astype(F32)
    kin = (kc * enb).astype(BF16)
    kup = (kc * ekb).astype(BF16)
    qin = qc.astype(BF16)
    for h in range(GLA_HEADS):
        ks = slice(h * GLA_DK, (h + 1) * GLA_DK)
        vs = slice(h * GLA_DV, (h + 1) * GLA_DV)
        vb = v_ref[rows, vs]
        a = jnp.where(causal, _dot_nt(qin[:, ks], kin[:, ks]), 0.0)
        intra = _dot(a.astype(BF16), vb)
        st = state_ref[h]
        cross = _dot(qin[:, ks], st.astype(BF16))
        upd = _dot_tn(kup[:, ks], vb)
        state_ref[h] = jnp.exp(gl[ks, :]) * st + upd
        o = intra + cross
        on = o * lax.rsqrt(jnp.mean(o * o, axis=-1, keepdims=True) + EPS)
        gate = _silu(g_ref[rows, vs].astype(F32))
        o_ref[rows, vs] = (on * gate * ms_ref[:, vs]).astype(BF16)


BAND = 3 * SUPER


BAND_KINDS = 3
ROLL_W = 1024
NEG_INF = -1e30


def _bias_kernel(rb_ref, o_ref):
    l = pl.program_id(0)
    h = pl.program_id(1)
    m = lax.broadcasted_iota(I32, (8, ROLL_W), 1)
    d = jnp.where(m < BAND, m, m - ROLL_W)
    idx = jnp.clip(2 * SUPER - d, -MAX_REL, MAX_REL) + MAX_REL

    def body(j, acc):
        return jnp.where(idx == j, rb_ref[l, h, j], acc)

    diag = lax.fori_loop(0, 2 * MAX_REL + 1, body, jnp.zeros((8, ROLL_W), F32))
    full = pltpu.roll(jnp.broadcast_to(diag[0:1, :], (SUPER, ROLL_W)), 0, 1, stride=1, stride_axis=0)
    bias = full[:, :BAND]
    row = lax.broadcasted_iota(I32, (SUPER, BAND), 0)
    col = lax.broadcasted_iota(I32, (SUPER, BAND), 1)
    dist = 2 * (SUPER // CHUNK) + row // CHUNK - col // CHUNK
    in_band = (dist >= 0) & (dist <= ATT_BAND_CHUNKS)
    for kind in range(BAND_KINDS):
        first_valid_col = (BAND_KINDS - 1 - kind) * SUPER
        o_ref[kind, 0] = jnp.where(in_band & (col >= first_valid_col), bias, NEG_INF)


def _att_bias(rel_bias):
    depth = rel_bias.shape[0]
    return pl.pallas_call(
        _bias_kernel,
        grid=(depth, ATT_HEADS),
        in_specs=[pl.BlockSpec(memory_space=pltpu.SMEM)],
        out_specs=pl.BlockSpec((None, BAND_KINDS, 1, SUPER, BAND), lambda l, h: (l, 0, h, 0, 0)),
        out_shape=jax.ShapeDtypeStruct((depth, BAND_KINDS, ATT_HEADS, SUPER, BAND), F32),
        compiler_params=_cparams(2),
        name="att_bias",
    )(rel_bias)


def _att_head(h, q_ref, k_refs, v_refs, bias_ref, ms_ref, o_ref):
    sl = slice(h * HEAD_DIM, (h + 1) * HEAD_DIM)
    scale = HEAD_DIM ** -0.5
    qb = (q_ref[:, sl].astype(F32) * scale).astype(BF16)
    sc = [_dot_nt(qb, k_refs[j][:, sl]) + bias_ref[h, :, j * SUPER:(j + 1) * SUPER] for j in range(3)]
    m = jnp.max(jnp.maximum(jnp.maximum(sc[0], sc[1]), sc[2]), axis=-1, keepdims=True)
    ps = [jnp.exp(sj - m) for sj in sc]
    den = jnp.sum(ps[0] + ps[1] + ps[2], axis=-1, keepdims=True)
    acc = _dot(ps[0].astype(BF16), v_refs[0][:, sl])
    for j in (1, 2):
        acc = acc + _dot(ps[j].astype(BF16), v_refs[j][:, sl])
    o = acc / den
    on = o * lax.rsqrt(jnp.mean(o * o, axis=-1, keepdims=True) + EPS)
    o_ref[:, sl] = (on * ms_ref[:, sl]).astype(BF16)


def _mixer_kernel(rq_ref, rk_ref, rv_ref, rg_ref, cos_ref, sin_ref, rms_ref,
                  gq_ref, gk_ref, gv_ref, gg_ref, ga_ref, wa_ref, ba_ref, gms_ref,
                  aq_ref, k0_ref, k1_ref, k2_ref, v0_ref, v1_ref, v2_ref, bias_ref, ams_ref,
                  oret_ref, ogla_ref, oatt_ref,
                  rstate_ref, dmat_ref, xi_ref, zeta_ref, gstate_ref):
    @pl.when(pl.program_id(1) == 0)
    def _():
        _ret_init(rstate_ref, dmat_ref, xi_ref, zeta_ref)
        gstate_ref[...] = jnp.zeros_like(gstate_ref)

    cos = cos_ref[...]
    sin = sin_ref[...]
    la = _gla_log_alpha(ga_ref, wa_ref, ba_ref)
    k_refs = (k0_ref, k1_ref, k2_ref)
    v_refs = (v0_ref, v1_ref, v2_ref)
    n_chunks = SUPER // CHUNK
    for h in range(max(RET_HEADS, ATT_HEADS)):
        if h < RET_HEADS:
            _ret_head(h, rq_ref, rk_ref, rv_ref, rg_ref, cos, sin, rms_ref, oret_ref,
                      rstate_ref, dmat_ref, xi_ref, zeta_ref)
        if h < ATT_HEADS:
            _att_head(h, aq_ref, k_refs, v_refs, bias_ref, ams_ref, oatt_ref)
        if h < n_chunks:
            _gla_chunk(h, la, gq_ref, gk_ref, gv_ref, gg_ref, gms_ref, ogla_ref, gstate_ref)


def _mixers(z, cos2, sin2, wa_all, ba, bias, ms, layer, batch, seq):
    t = z.shape[0]
    ns = seq // SUPER
    row = lambda b, s: b * ns + s
    zspec = lambda width, off: pl.BlockSpec((SUPER, width), lambda b, s: (row(b, s), off // width))
    const = lambda width: pl.BlockSpec((1, width), lambda b, s: (0, 0))

    def kv_spec(off, back):
        return pl.BlockSpec((SUPER, ATT_W), lambda b, s: (b * ns + jnp.maximum(s - back, 0), off // ATT_W))

    ak, av = Z_ATT + ATT_W, Z_ATT + 2 * ATT_W
    return pl.pallas_call(
        _mixer_kernel,
        grid=(batch, ns),
        in_specs=[
            zspec(RET_W, 0), zspec(RET_W, RET_W), zspec(RET_W, 2 * RET_W), zspec(RET_W, 3 * RET_W),
            pl.BlockSpec((SUPER, LANES), lambda b, s: (s, 0)),
            pl.BlockSpec((SUPER, LANES), lambda b, s: (s, 0)),
            const(RET_W),
            zspec(GLA_QK, Z_GQ), zspec(GLA_QK, Z_GK), zspec(GLA_V, Z_GV), zspec(GLA_V, Z_GG),
            zspec(LANES, Z_GA),
            pl.BlockSpec((None, LANES, GLA_QK), lambda b, s: (layer, 0, 0)),
            const(GLA_QK), const(GLA_V),
            zspec(ATT_W, Z_ATT),
            kv_spec(ak, 2), kv_spec(ak, 1), kv_spec(ak, 0),
            kv_spec(av, 2), kv_spec(av, 1), kv_spec(av, 0),
            pl.BlockSpec((None, None, ATT_HEADS, SUPER, BAND),
                         lambda b, s: (layer, jnp.minimum(s, BAND_KINDS - 1), 0, 0, 0)),
            const(ATT_W),
        ],
        out_specs=[
            pl.BlockSpec((SUPER, RET_W), lambda b, s: (row(b, s), 0)),
            pl.BlockSpec((SUPER, GLA_V), lambda b, s: (row(b, s), 0)),
            pl.BlockSpec((SUPER, ATT_W), lambda b, s: (row(b, s), 0)),
        ],
        out_shape=[
            jax.ShapeDtypeStruct((t, RET_W), BF16),
            jax.ShapeDtypeStruct((t, GLA_V), BF16),
            jax.ShapeDtypeStruct((t, ATT_W), BF16),
        ],
        scratch_shapes=[
            pltpu.VMEM((RET_HEADS, HEAD_DIM, HEAD_DIM), F32),
            pltpu.VMEM((RET_HEADS, SUPER, SUPER), F32),
            pltpu.VMEM((RET_HEADS, SUPER, HEAD_DIM), F32),
            pltpu.VMEM((RET_HEADS, SUPER, HEAD_DIM), F32),
            pltpu.VMEM((GLA_HEADS, GLA_DK, GLA_DV), F32),
        ],
        compiler_params=_cparams(2),
        name="mixers",
    )(z, z, z, z, cos2, sin2, ms[:RET_W].reshape(1, RET_W),
      z, z, z, z, z, wa_all, ba.reshape(1, GLA_QK), ms[RET_W:RET_W + GLA_V].reshape(1, GLA_V),
      z, z, z, z, z, z, z, bias, ms[RET_W + GLA_V:].reshape(1, ATT_W))


def _store_token_tiles(ref, packed):
    tm = packed.shape[0]
    for s in range(ROW_SUB):
        ref[pl.ds(s, tm, stride=ROW_SUB), :] = packed[:, s * LANES:(s + 1) * LANES]


def _load_token_tiles(ref, tm):
    return jnp.concatenate([ref[pl.ds(s, tm, stride=ROW_SUB), :] for s in range(ROW_SUB)], axis=1)


def _tile_rows(ref, row, n_rows=1):
    start = row * ROW_SUB
    if not isinstance(start, int):
        start = pl.multiple_of(start, ROW_SUB)
    return ref.at[pl.ds(start, n_rows * ROW_SUB), :]


def _route_kernel(x_ref, a_ref, b_ref, c_ref, w_ref, g_ref, wr_ref, br_ref,
                  x1_ref, hp_ref, meta_ref, metat_ref, cnt_ref, stage_ref, run_ref):
    i = pl.program_id(0)
    tm = x_ref.shape[0]
    half = x_ref.shape[1] // 2

    @pl.when(i == 0)
    def _():
        run_ref[...] = jnp.zeros_like(run_ref)
        stage_ref[...] = jnp.zeros_like(stage_ref)

    xr = stage_ref[...]
    live = jnp.where(i > 0, 1.0, 0.0)

    h = _rms(xr, g_ref[...])
    _store_token_tiles(hp_ref, _pack_bf16_pair(h[:, :half], h[:, half:]))
    lg = _dot(h.astype(BF16), wr_ref[...]) + br_ref[...]
    lane = lax.broadcasted_iota(I32, (tm, LANES), 1)
    neg = jnp.float32(-1e30)
    big = jnp.int32(LANES)
    gl = jnp.where(lane < N_GROUPS, lg, neg)
    gm = jnp.max(gl, axis=-1, keepdims=True)
    p_group = 1.0 / jnp.sum(jnp.exp(gl - gm), axis=-1, keepdims=True)
    g_idx = jnp.min(jnp.where(gl == gm, lane, big), axis=-1, keepdims=True)
    lo = ROUTE_LANE0 + EXPERTS_PER_GROUP * g_idx
    el = jnp.where((lane >= lo) & (lane < lo + EXPERTS_PER_GROUP), lg, neg)
    m1 = jnp.max(el, axis=-1, keepdims=True)
    i1 = jnp.min(jnp.where(el == m1, lane, big), axis=-1, keepdims=True)
    el2 = jnp.where(lane == i1, neg, el)
    m2 = jnp.max(el2, axis=-1, keepdims=True)
    i2 = jnp.min(jnp.where(el2 == m2, lane, big), axis=-1, keepdims=True)
    e2 = jnp.exp(m2 - m1)
    c1 = p_group / (1.0 + e2)
    c2 = p_group * e2 / (1.0 + e2)
    oh1 = jnp.where(lane == i1, 1.0, 0.0)
    oh2 = jnp.where(lane == i2, 1.0, 0.0)
    oh = (oh1 + oh2) * live
    r = lax.broadcasted_iota(I32, (tm, tm), 0)
    c = lax.broadcasted_iota(I32, (tm, tm), 1)
    stril = jnp.where(c < r, 1.0, 0.0).astype(BF16)
    before = _dot(stril, oh.astype(BF16)) + run_ref[0:1, :]
    rank1 = jnp.sum(before * oh1, axis=-1, keepdims=True)
    rank2 = jnp.sum(before * oh2, axis=-1, keepdims=True)
    run_ref[0:1, :] = run_ref[0:1, :] + jnp.sum(oh, axis=0, keepdims=True)
    cnt_ref[...] = run_ref[...]
    e1f = (i1 - ROUTE_LANE0).astype(F32)
    e2f = (i2 - ROUTE_LANE0).astype(F32)
    meta = jnp.where(lane == 0, e1f, 0.0)
    meta = jnp.where(lane == 1, e2f, meta)
    meta = jnp.where(lane == 2, rank1, meta)
    meta = jnp.where(lane == 3, rank2, meta)
    meta = jnp.where(lane == 4, c1, meta)
    meta = jnp.where(lane == 5, c2, meta)
    meta_ref[...] = meta
    mt = meta.T
    pairs = [jnp.concatenate([mt[2 * k:2 * k + 1, :], mt[2 * k + 1:2 * k + 2, :]], axis=1) for k in range(3)]
    metat_ref[0] = jnp.concatenate(pairs + [jnp.zeros((8 - len(pairs), 2 * tm), F32)], axis=0)

    x1 = (x_ref[...] + _dot(a_ref[...], w_ref[:RET_W, :]) + _dot(b_ref[...], w_ref[RET_W:RET_W + GLA_V, :])
          + _dot(c_ref[...], w_ref[RET_W + GLA_V:, :]))
    x1_ref[...] = x1
    stage_ref[...] = x1


def _out_route(x, o_ret, o_gla, o_att, w_all, g, wr_all, br_all, layer):
    t, d = x.shape
    tm = min(t, TOK_TILE)
    nt = t // tm
    cur = lambda w: pl.BlockSpec((tm, w), lambda i: (jnp.minimum(i, nt - 1), 0))
    prev = lambda i: jnp.maximum(i - 1, 0)
    return pl.pallas_call(
        _route_kernel,
        grid=(nt + 1,),
        in_specs=[
            cur(d), cur(RET_W), cur(GLA_V), cur(ATT_W),
            pl.BlockSpec((None, w_all.shape[1], d), lambda i: (layer, 0, 0), pipeline_mode=pl.Buffered(1)),
            pl.BlockSpec((1, d), lambda i: (0, 0)),
            pl.BlockSpec((None, d, LANES), lambda i: (layer, 0, 0)),
            pl.BlockSpec((None, 1, LANES), lambda i: (layer, 0, 0)),
        ],
        out_specs=[
            cur(d),
            pl.BlockSpec((tm * ROW_SUB, LANES), lambda i: (prev(i), 0)),
            pl.BlockSpec((tm, LANES), lambda i: (prev(i), 0)),
            pl.BlockSpec((1, 8, 2 * tm), lambda i: (prev(i), 0, 0)),
            pl.BlockSpec((8, LANES), lambda i: (0, 0)),
        ],
        out_shape=[
            jax.ShapeDtypeStruct((t, d), F32),
            jax.ShapeDtypeStruct((t * ROW_SUB, LANES), U32),
            jax.ShapeDtypeStruct((t, LANES), F32),
            jax.ShapeDtypeStruct((nt, 8, 2 * tm), F32),
            jax.ShapeDtypeStruct((8, LANES), F32),
        ],
        scratch_shapes=[pltpu.VMEM((tm, d), F32), pltpu.VMEM((8, LANES), F32)],
        compiler_params=_cparams(1),
        name="out_proj_router",
    )(x, o_ret, o_gla, o_att, w_all, g.reshape(1, d), wr_all, br_all)


_PAD_BITS = tuple(reversed(range(MOE_TILE.bit_length() - 1)))


def _dispatch_kernel(ps_ref, pn_ref, nu_ref, pos_ref, hp_ref, xs_ref, stage_ref, zero_ref, lsem_ref, sem_ref,
                     zsem_ref):
    i = pl.program_id(0)
    tm = pos_ref.shape[2] // 2
    zrows = zero_ref.shape[0] // ROW_SUB
    n_tiles = xs_ref.shape[0] // (ROW_SUB * MOE_TILE)

    def pad_copies(start):
        def zero_copy(off, k):
            cp = pltpu.make_async_copy(_tile_rows(zero_ref, 0, k), _tile_rows(xs_ref, off, k), zsem_ref.at[0])
            if start:
                cp.start()
            else:
                cp.wait()

        def body(e, carry):
            n = pn_ref[e]
            base = ps_ref[e]
            for bit in _PAD_BITS:
                k = 1 << bit

                @pl.when((n & k) != 0)
                def _():
                    zero_copy(base + ((n >> (bit + 1)) << (bit + 1)), k)
            return carry

        lax.fori_loop(0, N_EXPERTS, body, 0)

        def tail(tile, carry):
            for part in range(MOE_TILE // zrows):
                zero_copy(tile * MOE_TILE + part * zrows, zrows)
            return carry

        lax.fori_loop(nu_ref[0], n_tiles, tail, 0)

    n = pl.num_programs(0)

    def tile_load(tile, slot):
        return pltpu.make_async_copy(_tile_rows(hp_ref, tile * tm, tm), stage_ref.at[slot], lsem_ref.at[slot])

    @pl.when(i == 0)
    def _():
        zero_ref[...] = jnp.zeros_like(zero_ref)
        pad_copies(True)
        pad_copies(False)
        tile_load(0, 0).start()

    @pl.when(i + 1 < n)
    def _():
        tile_load(i + 1, (i + 1) % 3).start()

    tile_load(i, i % 3).wait()
    src = stage_ref.at[i % 3]
    for j in range(2 * tm):
        pltpu.make_async_copy(_tile_rows(src, j % tm), _tile_rows(xs_ref, pos_ref[0, 0, j]),
                              sem_ref.at[i % 2]).start(priority=j % 2)

    def wait_rows(slot):
        for j in range(2 * tm):
            pltpu.make_async_copy(_tile_rows(src, 0), _tile_rows(xs_ref, 0), sem_ref.at[slot]).wait()

    @pl.when(i > 0)
    def _():
        wait_rows((i + 1) % 2)

    @pl.when(i == n - 1)
    def _():
        wait_rows(i % 2)


def _dispatch(hp, pos_tiles, pad_start, pad_n, n_used, n_rows):
    nt = pos_tiles.shape[0]
    tm = pos_tiles.shape[2] // 2
    grid_spec = pltpu.PrefetchScalarGridSpec(
        num_scalar_prefetch=3,
        grid=(nt,),
        in_specs=[
            pl.BlockSpec((1, 1, 2 * tm), lambda i, ps, pn, nu: (i, 0, 0), memory_space=pltpu.SMEM),
            pl.BlockSpec(memory_space=pl.ANY),
        ],
        out_specs=pl.BlockSpec(memory_space=pl.ANY),
        scratch_shapes=[
            pltpu.VMEM((3, tm * ROW_SUB, LANES), U32),
            pltpu.VMEM((MOE_TILE // 2 * ROW_SUB, LANES), U32),
            pltpu.SemaphoreType.DMA((3,)),
            pltpu.SemaphoreType.DMA((2,)),
            pltpu.SemaphoreType.DMA((1,)),
        ],
    )
    return pl.pallas_call(
        _dispatch_kernel,
        grid_spec=grid_spec,
        out_shape=jax.ShapeDtypeStruct((n_rows * ROW_SUB, LANES), U32),
        compiler_params=_cparams(1),
        name="dispatch",
    )(pad_start, pad_n, n_used, pos_tiles, hp)


def _moe_kernel(te_ref, nu_ref, xs_ref, wg_ref, wu_ref, wd_ref, y_ref, wgb_ref, wub_ref, wdb_ref):
    i = pl.program_id(0)
    n_used = nu_ref[0]
    tm = xs_ref.shape[0] // ROW_SUB
    half = ROW_SUB * LANES
    e = te_ref[i]
    e_prev = te_ref[jnp.maximum(i - 1, 0)]

    @pl.when((i == 0) | (e != e_prev))
    def _():
        wgb_ref[...] = wg_ref[...].astype(BF16)
        wub_ref[...] = wu_ref[...].astype(BF16)
        wdb_ref[...] = wd_ref[...].astype(BF16)

    @pl.when(i < n_used)
    def _():
        lo, hi = _unpack_bf16_pair(_load_token_tiles(xs_ref, tm))
        xl = lo.astype(BF16)
        xh = hi.astype(BF16)
        a = _dot(xl, wgb_ref[:half, :]) + _dot(xh, wgb_ref[half:, :])
        u = _dot(xl, wub_ref[:half, :]) + _dot(xh, wub_ref[half:, :])
        hid = (_silu(a) * u).astype(BF16)
        y = _dot(hid, wdb_ref[...])
        _store_token_tiles(y_ref, _pack_bf16_pair(y[:, :half], y[:, half:]))

    @pl.when(i >= n_used)
    def _():
        y_ref[...] = jnp.zeros_like(y_ref)


def _moe(xs, tile_expert, n_used, wg, wu, wd, layer):
    nt = tile_expert.shape[0]
    tm = xs.shape[0] // ROW_SUB // nt
    d, ff = wg.shape[-2], wg.shape[-1]

    def w_spec(shape):
        return pl.BlockSpec((None, None, None) + shape,
                            lambda i, te, nu: (layer, te[i] // EXPERTS_PER_GROUP, te[i] % EXPERTS_PER_GROUP, 0, 0))

    grid_spec = pltpu.PrefetchScalarGridSpec(
        num_scalar_prefetch=2,
        grid=(nt,),
        in_specs=[
            pl.BlockSpec((tm * ROW_SUB, LANES), lambda i, te, nu: (jnp.minimum(i, nu[0] - 1), 0)),
            w_spec((d, ff)), w_spec((d, ff)), w_spec((ff, d)),
        ],
        out_specs=pl.BlockSpec((tm * ROW_SUB, LANES), lambda i, te, nu: (i, 0)),
        scratch_shapes=[
            pltpu.VMEM((d, ff), BF16),
            pltpu.VMEM((d, ff), BF16),
            pltpu.VMEM((ff, d), BF16),
        ],
    )
    return pl.pallas_call(
        _moe_kernel,
        grid_spec=grid_spec,
        out_shape=jax.ShapeDtypeStruct(xs.shape, U32),
        compiler_params=_cparams(1),
        name="expert_mlp",
    )(tile_expert, n_used, xs, wg, wu, wd)


def _ple_kernel(pos_ref, x_ref, meta_ref, y_ref, p_ref, g_ref, wg_ref, wp_ref, gn_ref, *rest, final):
    if final:
        o_ref, buf_ref, sem_ref = rest
    else:
        o_ref, hn_ref, buf_ref, sem_ref = rest
    i = pl.program_id(0)
    n = pl.num_programs(0) - 1
    tm = x_ref.shape[0]
    half = x_ref.shape[1] // 2

    for par in range(2):
        @pl.when((i < n) & (i % 2 == par))
        def _():
            for j in range(2 * tm):
                pltpu.make_async_copy(_tile_rows(y_ref, pos_ref[0, 0, j]), _tile_rows(buf_ref.at[par], j),
                                      sem_ref.at[par]).start(priority=j % 2)

    @pl.when(i > 0)
    def _():
        slot = (i + 1) % 2
        for j in range(2 * tm):
            pltpu.make_async_copy(_tile_rows(y_ref, 0), _tile_rows(buf_ref.at[slot], j), sem_ref.at[slot]).wait()
        rows = _load_token_tiles(buf_ref.at[slot], 2 * tm)
        meta = meta_ref[...]
        c1 = meta[:, 4:5]
        c2 = meta[:, 5:6]
        lo1, hi1 = _unpack_bf16_pair(rows[:tm])
        lo2, hi2 = _unpack_bf16_pair(rows[tm:])
        xl = x_ref[:, :half] + c1 * lo1 + c2 * lo2
        xh = x_ref[:, half:] + c1 * hi1 + c2 * hi2
        ms = (jnp.sum(xl * xl, axis=-1, keepdims=True) + jnp.sum(xh * xh, axis=-1, keepdims=True)) / (2 * half)
        inv = lax.rsqrt(ms + EPS)
        hl = (xl * inv * g_ref[:, :half]).astype(BF16)
        hh = (xh * inv * g_ref[:, half:]).astype(BF16)
        gate = _dot(hl, wg_ref[:half, :]) + _dot(hh, wg_ref[half:, :])
        gate = 1.0 / (1.0 + jnp.exp(-gate))
        pp = _dot(p_ref[...].astype(BF16), wp_ref[...])
        ol = xl + gate[:, :half] * pp[:, :half]
        oh = xh + gate[:, half:] * pp[:, half:]
        ms2 = (jnp.sum(ol * ol, axis=-1, keepdims=True) + jnp.sum(oh * oh, axis=-1, keepdims=True)) / (2 * half)
        inv2 = lax.rsqrt(ms2 + EPS)
        nl = ol * inv2 * gn_ref[:, :half]
        nh = oh * inv2 * gn_ref[:, half:]
        if final:
            o_ref[:, :half] = nl
            o_ref[:, half:] = nh
        else:
            o_ref[:, :half] = ol
            o_ref[:, half:] = oh
            hn_ref[:, :half] = nl.astype(BF16)
            hn_ref[:, half:] = nh.astype(BF16)


def _combine_ple(x, meta, pos_tiles, ys, p_all, g_ple, w_gate_all, w_proj_all, layer, g_next, final):
    t, d = x.shape
    nt = pos_tiles.shape[0]
    tm = pos_tiles.shape[2] // 2
    pd = p_all.shape[2]
    const = lambda shape: pl.BlockSpec(shape, lambda i: (0, 0))
    prev = lambda w: pl.BlockSpec((tm, w), lambda i: (jnp.maximum(i - 1, 0), 0))
    return pl.pallas_call(
        functools.partial(_ple_kernel, final=final),
        grid=(nt + 1,),
        in_specs=[
            pl.BlockSpec((1, 1, 2 * tm), lambda i: (jnp.minimum(i, nt - 1), 0, 0), memory_space=pltpu.SMEM),
            prev(d), prev(LANES),
            pl.BlockSpec(memory_space=pl.ANY),
            pl.BlockSpec((None, tm, pd), lambda i: (layer, jnp.maximum(i - 1, 0), 0)),
            const((1, d)),
            pl.BlockSpec((None, d, d), lambda i: (layer, 0, 0)),
            pl.BlockSpec((None, pd, d), lambda i: (layer, 0, 0)),
            const((1, d)),
        ],
        out_specs=[prev(d)] if final else [prev(d), prev(d)],
        out_shape=[jax.ShapeDtypeStruct((t, d), F32)] + ([] if final else [jax.ShapeDtypeStruct((t, d), BF16)]),
        scratch_shapes=[pltpu.VMEM((2, 2 * tm * ROW_SUB, LANES), U32), pltpu.SemaphoreType.DMA((2,))],
        compiler_params=_cparams(1),
        name="combine_ple",
    )(pos_tiles, x, meta, ys, p_all, g_ple.reshape(1, d), w_gate_all, w_proj_all, g_next.reshape(1, d))


W_GA0 = 4 * RET_W + 2 * GLA_QK + 2 * GLA_V


PREP_COLS = 256
PREP_ALIGNED = W_GA0 // PREP_COLS
PREP_SHIFTED = (Z_GA - W_GA0) // PREP_COLS


def _prep_kernel(wt_ref, o_ref):
    j = pl.program_id(1)
    t = wt_ref[0].T

    @pl.when(j < PREP_ALIGNED + PREP_SHIFTED)
    def _():
        o_ref[...] = t.astype(BF16)

    @pl.when(j >= PREP_ALIGNED + PREP_SHIFTED)
    def _():
        lane = lax.broadcasted_iota(I32, t.shape, 1)
        o_ref[...] = jnp.where(lane < GLA_RANK, t, 0.0).astype(BF16)


def _prep_w_in(w):
    depth, d, n = w.shape
    wt = jnp.transpose(w, (0, 2, 1))

    def src_row(j):
        shifted = (W_GA0 + GLA_RANK) // 8 + (j - PREP_ALIGNED) * (PREP_COLS // 8)
        r8 = jnp.where(j < PREP_ALIGNED, j * (PREP_COLS // 8),
                       jnp.where(j < PREP_ALIGNED + PREP_SHIFTED, shifted, W_GA0 // 8))
        return r8 * 8

    return pl.pallas_call(
        _prep_kernel,
        grid=(depth, Z_W // PREP_COLS),
        in_specs=[pl.BlockSpec((pl.Element(1), pl.Element(PREP_COLS), pl.Element(d)),
                               lambda l, j: (l, src_row(j), 0))],
        out_specs=pl.BlockSpec((None, d, PREP_COLS), lambda l, j: (l, 0, j)),
        out_shape=jax.ShapeDtypeStruct((depth, d, Z_W), BF16),
        compiler_params=_cparams(2),
        name="w_in_layout",
    )(wt)


def _pad_lanes(parts, width, dtype):
    cat = jnp.concatenate(parts, axis=-1)
    pad = jnp.zeros(cat.shape[:-1] + (width - cat.shape[-1],), cat.dtype)
    return jnp.concatenate([cat, pad], axis=-1).astype(dtype)


def _routing_tables(meta_t, counts, n_tok, tile):
    eid = meta_t[:, 0, :].astype(I32)
    rank = meta_t[:, 1, :].astype(I32)
    cnt = counts[0, ROUTE_LANE0:ROUTE_LANE0 + N_EXPERTS].astype(I32)
    padded = ((cnt + tile - 1) // tile) * tile
    ends = jnp.cumsum(padded)
    offs = ends - padded
    base = jnp.zeros_like(eid)
    for e in range(N_EXPERTS):
        base = jnp.where(eid == e, offs[e], base)
    pos = base + rank
    n_rows = 2 * n_tok + N_EXPERTS * tile
    nt = n_rows // tile
    tile_start = jnp.arange(nt, dtype=I32) * tile
    tile_expert = jnp.minimum(jnp.sum((tile_start[:, None] >= ends[None, :]).astype(I32), axis=1),
                              N_EXPERTS - 1)
    n_used = (ends[-1] // tile).reshape(1)
    return pos, tile_expert, n_used, offs + cnt, padded - cnt, n_rows


def kernel(x, p, g_mix, w_in, gla_w_alpha, gla_b_alpha, rel_bias, mix_scale, w_out, g_ffn,
           w_router_group, b_router_group, w_router_expert, b_router_expert,
           w_expert_gate, w_expert_up, w_expert_down, g_ple, w_ple_gate, w_ple_proj, g_final):
    batch, seq, d = x.shape
    depth = g_mix.shape[0]
    t = batch * seq
    assert seq % SUPER == 0 and d == 2 * ROW_SUB * LANES
    tok_tile = min(t, TOK_TILE)
    ple_tile = min(t, PLE_TILE)
    xf = x.reshape(t, d)
    cos2, sin2 = _rotary_tables(seq)
    w_in_r = _prep_w_in(w_in)
    w_out_b = w_out.astype(BF16)
    wa_all = jnp.concatenate(
        [gla_w_alpha, jnp.zeros((depth, LANES - GLA_RANK, GLA_QK), gla_w_alpha.dtype)], axis=1).astype(BF16)
    wr_all = _pad_lanes([w_router_group, w_router_expert.reshape(depth, d, N_EXPERTS)], LANES, BF16)
    br_all = _pad_lanes([b_router_group, b_router_expert.reshape(depth, N_EXPERTS)], LANES, F32)
    br_all = br_all.reshape(depth, 1, LANES)
    p_all = p.reshape(depth, t, p.shape[-1])
    w_pg_b = w_ple_gate.astype(BF16)
    w_pp_b = w_ple_proj.astype(BF16)
    bias_all = _att_bias(rel_bias)
    h = _norm_cast(xf, g_mix[0])
    for i in range(depth):
        z = _in_proj(h, w_in_r, i)
        ms = mix_scale[i]
        o_ret, o_gla, o_att = _mixers(z, cos2, sin2, wa_all, gla_b_alpha[i], bias_all, ms, i, batch, seq)
        x1, hp, meta, meta_t, counts = _out_route(xf, o_ret, o_gla, o_att, w_out_b, g_ffn[i], wr_all, br_all, i)
        pos, tile_expert, n_used, pad_start, pad_n, n_rows = _routing_tables(meta_t, counts, t, MOE_TILE)
        xs = _dispatch(hp, pos.reshape(t // tok_tile, 1, 2 * tok_tile), pad_start, pad_n, n_used, n_rows)
        ys = _moe(xs, tile_expert, n_used, w_expert_gate, w_expert_up, w_expert_down, i)
        r = tok_tile // ple_tile
        pos_ple = pos.reshape(t // tok_tile, 2, r, ple_tile).transpose(0, 2, 1, 3).reshape(t // ple_tile, 1, 2 * ple_tile)
        final = i == depth - 1
        outs = _combine_ple(x1, meta, pos_ple, ys, p_all, g_ple[i], w_pg_b, w_pp_b, i,
                            g_final if final else g_mix[i + 1], final=final)
        xf = outs[0]
        if not final:
            h = outs[1]
    return xf.reshape(batch, seq, d)
```

```python
import functools
import math

import numpy as np
import jax
import jax.numpy as jnp
from jax import lax
from jax.experimental import pallas as pl
from jax.experimental.pallas import tpu as pltpu

F32 = jnp.float32
BF16 = jnp.bfloat16
U32 = jnp.uint32
I32 = jnp.int32

CHUNK = 64
HEAD_DIM = 128
RET_HEADS = 6
GLA_HEADS = 4
GLA_DK = 64
GLA_DV = 128
GLA_RANK = 16
GLA_TAU = 16.0
ATT_HEADS = 6
ATT_BAND_CHUNKS = 8
MAX_REL = 128
N_GROUPS = 4
EXPERTS_PER_GROUP = 8
N_EXPERTS = N_GROUPS * EXPERTS_PER_GROUP
EXPERT_FF = 256
EPS = 1e-6

RET_W = RET_HEADS * HEAD_DIM
GLA_QK = GLA_HEADS * GLA_DK
GLA_V = GLA_HEADS * GLA_DV
ATT_W = ATT_HEADS * HEAD_DIM

LANES = 128
V7X_VMEM_LIMIT = 56 * 1024 * 1024

Z_RET = 0
Z_GQ = 4 * RET_W
Z_GK = Z_GQ + GLA_QK
Z_GV = Z_GK + GLA_QK
Z_GG = Z_GV + GLA_V
Z_ATT = Z_GG + GLA_V
Z_GA = Z_ATT + 3 * ATT_W
Z_W = 7 * 1024

SUPER = 4 * CHUNK
ROUTE_LANE0 = N_GROUPS
MOE_TILE = 512
TOK_TILE = 512
PLE_TILE = 256
ROW_SUB = 8
PROJ_ROWS = 2048
PROJ_COLS = 1024
NORM_TILE = 512
ROT_ROWS = 1024


def _cparams(n_axes):
    return pltpu.CompilerParams(
        dimension_semantics=("arbitrary",) * n_axes,
        vmem_limit_bytes=V7X_VMEM_LIMIT,
    )


def _dot(a, b):
    return jnp.dot(a, b, preferred_element_type=F32)


def _dot_nt(a, b):
    return lax.dot_general(a, b, (((1,), (1,)), ((), ())), preferred_element_type=F32)


def _dot_tn(a, b):
    return lax.dot_general(a, b, (((0,), (0,)), ((), ())), preferred_element_type=F32)


def _rms(x, g):
    return x * lax.rsqrt(jnp.mean(x * x, axis=-1, keepdims=True) + EPS) * g


def _silu(x):
    return x / (1.0 + jnp.exp(-x))


def _pack_bf16_pair(lo, hi):
    lo_b = lax.bitcast_convert_type(lo.astype(BF16).astype(F32), U32)
    hi_b = lax.bitcast_convert_type(hi.astype(BF16).astype(F32), U32)
    return (lo_b >> 16) | (hi_b & jnp.uint32(0xFFFF0000))


def _unpack_bf16_pair(w):
    lo = lax.bitcast_convert_type(w << 16, F32)
    hi = lax.bitcast_convert_type(w & jnp.uint32(0xFFFF0000), F32)
    return lo, hi


def _rot_kernel(inv_ref, cos_ref, sin_ref):
    rows = cos_ref.shape[0]
    pos = (lax.broadcasted_iota(I32, (rows, LANES), 0) + pl.program_id(0) * rows).astype(F32)
    lane = lax.broadcasted_iota(I32, (rows, LANES), 1)
    ang = pos * inv_ref[...]
    cos_ref[...] = jnp.cos(ang)
    s = jnp.sin(ang)
    sin_ref[...] = jnp.where(lane < HEAD_DIM // 2, -s, s)


def _rotary_tables(seq):
    half = HEAD_DIM // 2
    inv = np.float32(1.0) / (np.float32(10000.0) ** (np.arange(half, dtype=np.float32) / np.float32(half)))
    inv2 = jnp.asarray(np.concatenate([inv, inv]).reshape(1, LANES).astype(np.float32))
    rows = min(seq, ROT_ROWS)
    return pl.pallas_call(
        _rot_kernel,
        grid=(seq // rows,),
        in_specs=[pl.BlockSpec((1, LANES), lambda i: (0, 0))],
        out_specs=[pl.BlockSpec((rows, LANES), lambda i: (i, 0))] * 2,
        out_shape=[jax.ShapeDtypeStruct((seq, LANES), F32)] * 2,
        compiler_params=_cparams(1),
        name="rotary_tables",
    )(inv2)


def _norm_kernel(x_ref, g_ref, h_ref):
    h_ref[...] = _rms(x_ref[...], g_ref[...]).astype(BF16)


def _norm_cast(x, g):
    t, d = x.shape
    tm = min(t, NORM_TILE)
    return pl.pallas_call(
        _norm_kernel,
        grid=(t // tm,),
        in_specs=[pl.BlockSpec((tm, d), lambda i: (i, 0)), pl.BlockSpec((1, d), lambda i: (0, 0))],
        out_specs=pl.BlockSpec((tm, d), lambda i: (i, 0)),
        out_shape=jax.ShapeDtypeStruct((t, d), BF16),
        compiler_params=_cparams(1),
        name="norm_cast",
    )(x, g.reshape(1, d))


def _in_kernel(h_ref, w_ref, z_ref):
    z_ref[...] = _dot(h_ref[...], w_ref[...]).astype(BF16)


def _in_proj(h, w_all, layer):
    t, d = h.shape
    n = w_all.shape[2]
    tm = min(t, PROJ_ROWS)
    tn = min(n, PROJ_COLS)
    return pl.pallas_call(
        _in_kernel,
        grid=(t // tm, n // tn),
        in_specs=[
            pl.BlockSpec((tm, d), lambda i, j: (i, 0)),
            pl.BlockSpec((None, d, tn), lambda i, j: (layer, 0, j)),
        ],
        out_specs=pl.BlockSpec((tm, tn), lambda i, j: (i, j)),
        out_shape=jax.ShapeDtypeStruct((t, n), BF16),
        compiler_params=_cparams(2),
        name="in_proj",
    )(h, w_all)


def _ret_log_gamma(h):
    return math.log(1.0 - 2.0 ** (-5.0 - h))


def _ret_init(state_ref, dmat_ref, xi_ref, zeta_ref):
    state_ref[...] = jnp.zeros_like(state_ref)
    row = lax.broadcasted_iota(I32, (SUPER, SUPER), 0)
    col = lax.broadcasted_iota(I32, (SUPER, SUPER), 1)
    dist = jnp.abs(row - col).astype(F32)
    keep = (col <= row) | ((row // CHUNK) == (col // CHUNK))
    t = lax.broadcasted_iota(I32, (SUPER, HEAD_DIM), 0).astype(F32)
    for h in range(RET_HEADS):
        lg = _ret_log_gamma(h)
        dmat_ref[h] = jnp.where(keep, jnp.exp(lg * dist), 0.0)
        xi_ref[h] = jnp.exp(lg * (t + 1.0))
        zeta_ref[h] = jnp.exp(lg * (SUPER - 1.0 - t))


def _ret_head(h, q_ref, k_ref, v_ref, g_ref, cos, sin, ms_ref, o_ref, state_ref, dmat_ref, xi_ref, zeta_ref):
    sl = slice(h * HEAD_DIM, (h + 1) * HEAD_DIM)
    scale = HEAD_DIM ** -0.5
    q = q_ref[:, sl].astype(F32)
    k = k_ref[:, sl].astype(F32)
    vb = v_ref[:, sl]
    qr = q * cos + pltpu.roll(q, HEAD_DIM // 2, 1) * sin
    kr = (k * cos + pltpu.roll(k, HEAD_DIM // 2, 1) * sin) * scale
    qb = qr.astype(BF16)
    kb = kr.astype(BF16)
    sc = _dot_nt(qb, kb) * dmat_ref[h]
    intra = _dot(sc.astype(BF16), vb)
    st = state_ref[h]
    cross = _dot(qb, st.astype(BF16)) * xi_ref[h]
    kz = (kr * zeta_ref[h]).astype(BF16)
    upd = _dot_tn(kz, vb)
    state_ref[h] = math.exp(_ret_log_gamma(h) * SUPER) * st + upd
    o = intra + cross
    mu = jnp.mean(o, axis=-1, keepdims=True)
    oc = o - mu
    var = jnp.mean(oc * oc, axis=-1, keepdims=True)
    on = oc * lax.rsqrt(var + EPS)
    gate = _silu(g_ref[:, sl].astype(F32))
    o_ref[:, sl] = (on * gate * ms_ref[:, sl]).astype(BF16)


def _gla_log_alpha(a_ref, wa_ref, ba_ref):
    pre = _dot(a_ref[...], wa_ref[...]) + ba_ref[...]
    return (jnp.minimum(pre, 0.0) - jnp.log(1.0 + jnp.exp(-jnp.abs(pre)))) * (1.0 / GLA_TAU)


def _gla_chunk(ci, la, q_ref, k_ref, v_ref, g_ref, ms_ref, o_ref, state_ref):
    r = lax.broadcasted_iota(I32, (CHUNK, CHUNK), 0)
    c = lax.broadcasted_iota(I32, (CHUNK, CHUNK), 1)
    causal = c <= r
    tril = jnp.where(causal, 1.0, 0.0).astype(BF16)
    ones = jnp.ones((CHUNK, GLA_DV), BF16)
    scale = GLA_DK ** -0.5
    rows = slice(ci * CHUNK, (ci + 1) * CHUNK)
    la_c = la[rows, :]
    hi = la_c.astype(BF16)
    lo = (la_c - hi.astype(F32)).astype(BF16)
    bcs = _dot(tril, hi) + _dot(tril, lo)
    gl = _dot_tn(hi, ones) + _dot_tn(lo, ones)
    blast = bcs[CHUNK - 1:CHUNK, :]
    eb = jnp.exp(bcs)
    enb = jnp.exp(-bcs)
    ekb = jnp.exp(blast - bcs)
    qc = q_ref[rows, :].astype(F32) * scale * eb
    kc = k_ref[rows, :].astype(F32)
    kin = (kc * enb).astype(BF16)
    kup = (kc * ekb).astype(BF16)
    qin = qc.astype(BF16)
    for h in range(GLA_HEADS):
        ks = slice(h * GLA_DK, (h + 1) * GLA_DK)
        vs = slice(h * GLA_DV, (h + 1) * GLA_DV)
        vb = v_ref[rows, vs]
        a = jnp.where(causal, _dot_nt(qin[:, ks], kin[:, ks]), 0.0)
        intra = _dot(a.astype(BF16), vb)
        st = state_ref[h]
        cross = _dot(qin[:, ks], st.astype(BF16))
        upd = _dot_tn(kup[:, ks], vb)
        state_ref[h] = jnp.exp(gl[ks, :]) * st + upd
        o = intra + cross
        on = o * lax.rsqrt(jnp.mean(o * o, axis=-1, keepdims=True) + EPS)
        gate = _silu(g_ref[rows, vs].astype(F32))
        o_ref[rows, vs] = (on * gate * ms_ref[:, vs]).astype(BF16)


BAND = 3 * SUPER


BAND_KINDS = 3
ROLL_W = 1024
NEG_INF = -1e30


def _bias_kernel(rb_ref, o_ref):
    l = pl.program_id(0)
    h = pl.program_id(1)
    m = lax.broadcasted_iota(I32, (8, ROLL_W), 1)
    d = jnp.where(m < BAND, m, m - ROLL_W)
    idx = jnp.clip(2 * SUPER - d, -MAX_REL, MAX_REL) + MAX_REL

    n_rel = 2 * MAX_REL + 1
    base = (l * ATT_HEADS + h) * n_rel

    def body(j, acc):
        return jnp.where(idx == j, rb_ref[base + j], acc)

    diag = lax.fori_loop(0, n_rel, body, jnp.zeros((8, ROLL_W), F32), unroll=8)
    full = pltpu.roll(jnp.broadcast_to(diag[0:1, :], (SUPER, ROLL_W)), 0, 1, stride=1, stride_axis=0)
    bias = full[:, :BAND]
    row = lax.broadcasted_iota(I32, (SUPER, BAND), 0)
    col = lax.broadcasted_iota(I32, (SUPER, BAND), 1)
    dist = 2 * (SUPER // CHUNK) + row // CHUNK - col // CHUNK
    in_band = (dist >= 0) & (dist <= ATT_BAND_CHUNKS)
    for kind in range(BAND_KINDS):
        first_valid_col = (BAND_KINDS - 1 - kind) * SUPER
        o_ref[kind, 0] = jnp.where(in_band & (col >= first_valid_col), bias, NEG_INF)


def _att_bias(rel_bias):
    depth = rel_bias.shape[0]
    return pl.pallas_call(
        _bias_kernel,
        grid=(depth, ATT_HEADS),
        in_specs=[pl.BlockSpec(memory_space=pltpu.SMEM)],
        out_specs=pl.BlockSpec((None, BAND_KINDS, 1, SUPER, BAND), lambda l, h: (l, 0, h, 0, 0)),
        out_shape=jax.ShapeDtypeStruct((depth, BAND_KINDS, ATT_HEADS, SUPER, BAND), F32),
        compiler_params=_cparams(2),
        name="att_bias",
    )(rel_bias.reshape(-1))


def _att_head(h, q_ref, k_refs, v_refs, bias_ref, ms_ref, o_ref):
    sl = slice(h * HEAD_DIM, (h + 1) * HEAD_DIM)
    scale = HEAD_DIM ** -0.5
    qb = (q_ref[:, sl].astype(F32) * scale).astype(BF16)
    sc = [_dot_nt(qb, k_refs[j][:, sl]) + bias_ref[h, :, j * SUPER:(j + 1) * SUPER] for j in range(3)]
    m = jnp.max(jnp.maximum(jnp.maximum(sc[0], sc[1]), sc[2]), axis=-1, keepdims=True)
    ps = [jnp.exp(sj - m) for sj in sc]
    den = jnp.sum(ps[0] + ps[1] + ps[2], axis=-1, keepdims=True)
    acc = _dot(ps[0].astype(BF16), v_refs[0][:, sl])
    for j in (1, 2):
        acc = acc + _dot(ps[j].astype(BF16), v_refs[j][:, sl])
    o = acc / den
    on = o * lax.rsqrt(jnp.mean(o * o, axis=-1, keepdims=True) + EPS)
    o_ref[:, sl] = (on * ms_ref[:, sl]).astype(BF16)


def _mixer_kernel(rq_ref, rk_ref, rv_ref, rg_ref, cos_ref, sin_ref, rms_ref,
                  gq_ref, gk_ref, gv_ref, gg_ref, ga_ref, wa_ref, ba_ref, gms_ref,
                  aq_ref, k0_ref, k1_ref, k2_ref, v0_ref, v1_ref, v2_ref, bias_ref, ams_ref,
                  oret_ref, ogla_ref, oatt_ref,
                  rstate_ref, dmat_ref, xi_ref, zeta_ref, gstate_ref):
    @pl.when(pl.program_id(1) == 0)
    def _():
        _ret_init(rstate_ref, dmat_ref, xi_ref, zeta_ref)
        gstate_ref[...] = jnp.zeros_like(gstate_ref)

    cos = cos_ref[...]
    sin = sin_ref[...]
    la = _gla_log_alpha(ga_ref, wa_ref, ba_ref)
    k_refs = (k0_ref, k1_ref, k2_ref)
    v_refs = (v0_ref, v1_ref, v2_ref)
    n_chunks = SUPER // CHUNK
    for h in range(max(RET_HEADS, ATT_HEADS)):
        if h < RET_HEADS:
            _ret_head(h, rq_ref, rk_ref, rv_ref, rg_ref, cos, sin, rms_ref, oret_ref,
                      rstate_ref, dmat_ref, xi_ref, zeta_ref)
        if h < ATT_HEADS:
            _att_head(h, aq_ref, k_refs, v_refs, bias_ref, ams_ref, oatt_ref)
        if h < n_chunks:
            _gla_chunk(h, la, gq_ref, gk_ref, gv_ref, gg_ref, gms_ref, ogla_ref, gstate_ref)


def _mixers(z, cos2, sin2, wa_all, ba, bias, ms, layer, batch, seq):
    t = z.shape[0]
    ns = seq // SUPER
    row = lambda b, s: b * ns + s
    zspec = lambda width, off: pl.BlockSpec((SUPER, width), lambda b, s: (row(b, s), off // width))
    const = lambda width: pl.BlockSpec((1, width), lambda b, s: (0, 0))

    def kv_spec(off, back):
        return pl.BlockSpec((SUPER, ATT_W), lambda b, s: (b * ns + jnp.maximum(s - back, 0), off // ATT_W))

    ak, av = Z_ATT + ATT_W, Z_ATT + 2 * ATT_W
    return pl.pallas_call(
        _mixer_kernel,
        grid=(batch, ns),
        in_specs=[
            zspec(RET_W, 0), zspec(RET_W, RET_W), zspec(RET_W, 2 * RET_W), zspec(RET_W, 3 * RET_W),
            pl.BlockSpec((SUPER, LANES), lambda b, s: (s, 0)),
            pl.BlockSpec((SUPER, LANES), lambda b, s: (s, 0)),
            const(RET_W),
            zspec(GLA_QK, Z_GQ), zspec(GLA_QK, Z_GK), zspec(GLA_V, Z_GV), zspec(GLA_V, Z_GG),
            zspec(LANES, Z_GA),
            pl.BlockSpec((None, LANES, GLA_QK), lambda b, s: (layer, 0, 0)),
            const(GLA_QK), const(GLA_V),
            zspec(ATT_W, Z_ATT),
            kv_spec(ak, 2), kv_spec(ak, 1), kv_spec(ak, 0),
            kv_spec(av, 2), kv_spec(av, 1), kv_spec(av, 0),
            pl.BlockSpec((None, None, ATT_HEADS, SUPER, BAND),
                         lambda b, s: (layer, jnp.minimum(s, BAND_KINDS - 1), 0, 0, 0)),
            const(ATT_W),
        ],
        out_specs=[
            pl.BlockSpec((SUPER, RET_W), lambda b, s: (row(b, s), 0)),
            pl.BlockSpec((SUPER, GLA_V), lambda b, s: (row(b, s), 0)),
            pl.BlockSpec((SUPER, ATT_W), lambda b, s: (row(b, s), 0)),
        ],
        out_shape=[
            jax.ShapeDtypeStruct((t, RET_W), BF16),
            jax.ShapeDtypeStruct((t, GLA_V), BF16),
            jax.ShapeDtypeStruct((t, ATT_W), BF16),
        ],
        scratch_shapes=[
            pltpu.VMEM((RET_HEADS, HEAD_DIM, HEAD_DIM), F32),
            pltpu.VMEM((RET_HEADS, SUPER, SUPER), F32),
            pltpu.VMEM((RET_HEADS, SUPER, HEAD_DIM), F32),
            pltpu.VMEM((RET_HEADS, SUPER, HEAD_DIM), F32),
            pltpu.VMEM((GLA_HEADS, GLA_DK, GLA_DV), F32),
        ],
        compiler_params=_cparams(2),
        name="mixers",
    )(z, z, z, z, cos2, sin2, ms[:RET_W].reshape(1, RET_W),
      z, z, z, z, z, wa_all, ba.reshape(1, GLA_QK), ms[RET_W:RET_W + GLA_V].reshape(1, GLA_V),
      z, z, z, z, z, z, z, bias, ms[RET_W + GLA_V:].reshape(1, ATT_W))


def _store_token_tiles(ref, packed):
    tm = packed.shape[0]
    for s in range(ROW_SUB):
        ref[pl.ds(s, tm, stride=ROW_SUB), :] = packed[:, s * LANES:(s + 1) * LANES]


def _load_token_tiles(ref, tm):
    return jnp.concatenate([ref[pl.ds(s, tm, stride=ROW_SUB), :] for s in range(ROW_SUB)], axis=1)


def _tile_rows(ref, row, n_rows=1):
    start = row * ROW_SUB
    if not isinstance(start, int):
        start = pl.multiple_of(start, ROW_SUB)
    return ref.at[pl.ds(start, n_rows * ROW_SUB), :]


def _route_kernel(x_ref, a_ref, b_ref, c_ref, w_ref, g_ref, wr_ref, br_ref,
                  x1_ref, hp_ref, meta_ref, metat_ref, cnt_ref, stage_ref, run_ref):
    i = pl.program_id(0)
    tm = x_ref.shape[0]
    half = x_ref.shape[1] // 2

    @pl.when(i == 0)
    def _():
        run_ref[...] = jnp.zeros_like(run_ref)
        stage_ref[...] = jnp.zeros_like(stage_ref)

    xr = stage_ref[...]
    live = jnp.where(i > 0, 1.0, 0.0)

    h = _rms(xr, g_ref[...])
    _store_token_tiles(hp_ref, _pack_bf16_pair(h[:, :half], h[:, half:]))
    lg = _dot(h.astype(BF16), wr_ref[...]) + br_ref[...]
    lane = lax.broadcasted_iota(I32, (tm, LANES), 1)
    neg = jnp.float32(-1e30)
    big = jnp.int32(LANES)
    gl = jnp.where(lane < N_GROUPS, lg, neg)
    gm = jnp.max(gl, axis=-1, keepdims=True)
    p_group = 1.0 / jnp.sum(jnp.exp(gl - gm), axis=-1, keepdims=True)
    g_idx = jnp.min(jnp.where(gl == gm, lane, big), axis=-1, keepdims=True)
    lo = ROUTE_LANE0 + EXPERTS_PER_GROUP * g_idx
    el = jnp.where((lane >= lo) & (lane < lo + EXPERTS_PER_GROUP), lg, neg)
    m1 = jnp.max(el, axis=-1, keepdims=True)
    i1 = jnp.min(jnp.where(el == m1, lane, big), axis=-1, keepdims=True)
    el2 = jnp.where(lane == i1, neg, el)
    m2 = jnp.max(el2, axis=-1, keepdims=True)
    i2 = jnp.min(jnp.where(el2 == m2, lane, big), axis=-1, keepdims=True)
    e2 = jnp.exp(m2 - m1)
    c1 = p_group / (1.0 + e2)
    c2 = p_group * e2 / (1.0 + e2)
    oh1 = jnp.where(lane == i1, 1.0, 0.0)
    oh2 = jnp.where(lane == i2, 1.0, 0.0)
    oh = (oh1 + oh2) * live
    r = lax.broadcasted_iota(I32, (tm, tm), 0)
    c = lax.broadcasted_iota(I32, (tm, tm), 1)
    stril = jnp.where(c < r, 1.0, 0.0).astype(BF16)
    before = _dot(stril, oh.astype(BF16)) + run_ref[0:1, :]
    rank1 = jnp.sum(before * oh1, axis=-1, keepdims=True)
    rank2 = jnp.sum(before * oh2, axis=-1, keepdims=True)
    run_ref[0:1, :] = run_ref[0:1, :] + jnp.sum(oh, axis=0, keepdims=True)
    cnt_ref[...] = run_ref[...]
    e1f = (i1 - ROUTE_LANE0).astype(F32)
    e2f = (i2 - ROUTE_LANE0).astype(F32)
    meta = jnp.where(lane == 0, e1f, 0.0)
    meta = jnp.where(lane == 1, e2f, meta)
    meta = jnp.where(lane == 2, rank1, meta)
    meta = jnp.where(lane == 3, rank2, meta)
    meta = jnp.where(lane == 4, c1, meta)
    meta = jnp.where(lane == 5, c2, meta)
    meta_ref[...] = meta
    metat_ref[0] = meta.T[0:8, :]

    x1 = (x_ref[...] + _dot(a_ref[...], w_ref[:RET_W, :]) + _dot(b_ref[...], w_ref[RET_W:RET_W + GLA_V, :])
          + _dot(c_ref[...], w_ref[RET_W + GLA_V:, :]))
    x1_ref[...] = x1
    stage_ref[...] = x1


def _out_route(x, o_ret, o_gla, o_att, w_all, g, wr_all, br_all, layer):
    t, d = x.shape
    tm = min(t, TOK_TILE)
    nt = t // tm
    cur = lambda w: pl.BlockSpec((tm, w), lambda i: (jnp.minimum(i, nt - 1), 0))
    prev = lambda i: jnp.maximum(i - 1, 0)
    return pl.pallas_call(
        _route_kernel,
        grid=(nt + 1,),
        in_specs=[
            cur(d), cur(RET_W), cur(GLA_V), cur(ATT_W),
            pl.BlockSpec((None, w_all.shape[1], d), lambda i: (layer, 0, 0), pipeline_mode=pl.Buffered(1)),
            pl.BlockSpec((1, d), lambda i: (0, 0)),
            pl.BlockSpec((None, d, LANES), lambda i: (layer, 0, 0)),
            pl.BlockSpec((None, 1, LANES), lambda i: (layer, 0, 0)),
        ],
        out_specs=[
            cur(d),
            pl.BlockSpec((tm * ROW_SUB, LANES), lambda i: (prev(i), 0)),
            pl.BlockSpec((tm, LANES), lambda i: (prev(i), 0)),
            pl.BlockSpec((1, 8, tm), lambda i: (prev(i), 0, 0)),
            pl.BlockSpec((8, LANES), lambda i: (0, 0)),
        ],
        out_shape=[
            jax.ShapeDtypeStruct((t, d), F32),
            jax.ShapeDtypeStruct((t * ROW_SUB, LANES), U32),
            jax.ShapeDtypeStruct((t, LANES), F32),
            jax.ShapeDtypeStruct((nt, 8, tm), F32),
            jax.ShapeDtypeStruct((8, LANES), F32),
        ],
        scratch_shapes=[pltpu.VMEM((tm, d), F32), pltpu.VMEM((8, LANES), F32)],
        compiler_params=_cparams(1),
        name="out_proj_router",
    )(x, o_ret, o_gla, o_att, w_all, g.reshape(1, d), wr_all, br_all)


_PAD_BITS = tuple(reversed(range(MOE_TILE.bit_length() - 1)))


def _dispatch_kernel(ps_ref, pn_ref, nu_ref, pos_ref, hp_ref, xs_ref, stage_ref, zero_ref, lsem_ref, sem_ref,
                     zsem_ref):
    i = pl.program_id(0)
    tm = pos_ref.shape[2] // 2
    zrows = zero_ref.shape[0] // ROW_SUB
    n_tiles = xs_ref.shape[0] // (ROW_SUB * MOE_TILE)

    def pad_copies(start):
        def zero_copy(off, k):
            cp = pltpu.make_async_copy(_tile_rows(zero_ref, 0, k), _tile_rows(xs_ref, off, k), zsem_ref.at[0])
            if start:
                cp.start()
            else:
                cp.wait()

        def body(e, carry):
            n = pn_ref[e]
            base = ps_ref[e]
            for bit in _PAD_BITS:
                k = 1 << bit

                @pl.when((n & k) != 0)
                def _():
                    zero_copy(base + ((n >> (bit + 1)) << (bit + 1)), k)
            return carry

        lax.fori_loop(0, N_EXPERTS, body, 0)

        def tail(tile, carry):
            for part in range(MOE_TILE // zrows):
                zero_copy(tile * MOE_TILE + part * zrows, zrows)
            return carry

        lax.fori_loop(nu_ref[0], n_tiles, tail, 0)

    n = pl.num_programs(0)

    def tile_load(tile, slot):
        return pltpu.make_async_copy(_tile_rows(hp_ref, tile * tm, tm), stage_ref.at[slot], lsem_ref.at[slot])

    @pl.when(i == 0)
    def _():
        zero_ref[...] = jnp.zeros_like(zero_ref)
        pad_copies(True)
        pad_copies(False)
        tile_load(0, 0).start()

    @pl.when(i + 1 < n)
    def _():
        tile_load(i + 1, (i + 1) % 3).start()

    tile_load(i, i % 3).wait()
    src = stage_ref.at[i % 3]
    for j in range(2 * tm):
        pltpu.make_async_copy(_tile_rows(src, j % tm), _tile_rows(xs_ref, pos_ref[0, 0, j]),
                              sem_ref.at[i % 2]).start(priority=j % 2)

    def wait_rows(slot):
        for j in range(2 * tm):
            pltpu.make_async_copy(_tile_rows(src, 0), _tile_rows(xs_ref, 0), sem_ref.at[slot]).wait()

    @pl.when(i > 0)
    def _():
        wait_rows((i + 1) % 2)

    @pl.when(i == n - 1)
    def _():
        wait_rows(i % 2)


def _dispatch(hp, pos_tiles, pad_start, pad_n, n_used, n_rows):
    nt = pos_tiles.shape[0]
    tm = pos_tiles.shape[2] // 2
    grid_spec = pltpu.PrefetchScalarGridSpec(
        num_scalar_prefetch=3,
        grid=(nt,),
        in_specs=[
            pl.BlockSpec((1, 1, 2 * tm), lambda i, ps, pn, nu: (i, 0, 0), memory_space=pltpu.SMEM),
            pl.BlockSpec(memory_space=pl.ANY),
        ],
        out_specs=pl.BlockSpec(memory_space=pl.ANY),
        scratch_shapes=[
            pltpu.VMEM((3, tm * ROW_SUB, LANES), U32),
            pltpu.VMEM((MOE_TILE // 2 * ROW_SUB, LANES), U32),
            pltpu.SemaphoreType.DMA((3,)),
            pltpu.SemaphoreType.DMA((2,)),
            pltpu.SemaphoreType.DMA((1,)),
        ],
    )
    return pl.pallas_call(
        _dispatch_kernel,
        grid_spec=grid_spec,
        out_shape=jax.ShapeDtypeStruct((n_rows * ROW_SUB, LANES), U32),
        compiler_params=_cparams(1),
        name="dispatch",
    )(pad_start, pad_n, n_used, pos_tiles, hp)


def _moe_kernel(te_ref, nu_ref, xs_ref, wg_ref, wu_ref, wd_ref, y_ref, wgb_ref, wub_ref, wdb_ref):
    i = pl.program_id(0)
    n_used = nu_ref[0]
    tm = xs_ref.shape[0] // ROW_SUB
    half = ROW_SUB * LANES
    e = te_ref[i]
    e_prev = te_ref[jnp.maximum(i - 1, 0)]

    @pl.when((i == 0) | (e != e_prev))
    def _():
        wgb_ref[...] = wg_ref[...].astype(BF16)
        wub_ref[...] = wu_ref[...].astype(BF16)
        wdb_ref[...] = wd_ref[...].astype(BF16)

    @pl.when(i < n_used)
    def _():
        lo, hi = _unpack_bf16_pair(_load_token_tiles(xs_ref, tm))
        xl = lo.astype(BF16)
        xh = hi.astype(BF16)
        a = _dot(xl, wgb_ref[:half, :]) + _dot(xh, wgb_ref[half:, :])
        u = _dot(xl, wub_ref[:half, :]) + _dot(xh, wub_ref[half:, :])
        hid = (_silu(a) * u).astype(BF16)
        y = _dot(hid, wdb_ref[...])
        _store_token_tiles(y_ref, _pack_bf16_pair(y[:, :half], y[:, half:]))

    @pl.when(i >= n_used)
    def _():
        y_ref[...] = jnp.zeros_like(y_ref)


def _moe(xs, tile_expert, n_used, wg, wu, wd, layer):
    nt = tile_expert.shape[0]
    tm = xs.shape[0] // ROW_SUB // nt
    d, ff = wg.shape[-2], wg.shape[-1]

    def w_spec(shape):
        return pl.BlockSpec((None, None, None) + shape,
                            lambda i, te, nu: (layer, te[i] // EXPERTS_PER_GROUP, te[i] % EXPERTS_PER_GROUP, 0, 0))

    grid_spec = pltpu.PrefetchScalarGridSpec(
        num_scalar_prefetch=2,
        grid=(nt,),
        in_specs=[
            pl.BlockSpec((tm * ROW_SUB, LANES), lambda i, te, nu: (jnp.minimum(i, nu[0] - 1), 0)),
            w_spec((d, ff)), w_spec((d, ff)), w_spec((ff, d)),
        ],
        out_specs=pl.BlockSpec((tm * ROW_SUB, LANES), lambda i, te, nu: (i, 0)),
        scratch_shapes=[
            pltpu.VMEM((d, ff), BF16),
            pltpu.VMEM((d, ff), BF16),
            pltpu.VMEM((ff, d), BF16),
        ],
    )
    return pl.pallas_call(
        _moe_kernel,
        grid_spec=grid_spec,
        out_shape=jax.ShapeDtypeStruct(xs.shape, U32),
        compiler_params=_cparams(1),
        name="expert_mlp",
    )(tile_expert, n_used, xs, wg, wu, wd)


def _ple_kernel(pos_ref, x_ref, meta_ref, y_ref, p_ref, g_ref, wg_ref, wp_ref, gn_ref, *rest, final):
    if final:
        o_ref, buf_ref, sem_ref = rest
    else:
        o_ref, hn_ref, buf_ref, sem_ref = rest
    i = pl.program_id(0)
    n = pl.num_programs(0) - 1
    tm = x_ref.shape[0]
    half = x_ref.shape[1] // 2

    for par in range(2):
        @pl.when((i < n) & (i % 2 == par))
        def _():
            for j in range(2 * tm):
                pltpu.make_async_copy(_tile_rows(y_ref, pos_ref[0, 0, j]), _tile_rows(buf_ref.at[par], j),
                                      sem_ref.at[par]).start(priority=j % 2)

    @pl.when(i > 0)
    def _():
        slot = (i + 1) % 2
        for j in range(2 * tm):
            pltpu.make_async_copy(_tile_rows(y_ref, 0), _tile_rows(buf_ref.at[slot], j), sem_ref.at[slot]).wait()
        rows = _load_token_tiles(buf_ref.at[slot], 2 * tm)
        meta = meta_ref[...]
        c1 = meta[:, 4:5]
        c2 = meta[:, 5:6]
        lo1, hi1 = _unpack_bf16_pair(rows[:tm])
        lo2, hi2 = _unpack_bf16_pair(rows[tm:])
        xl = x_ref[:, :half] + c1 * lo1 + c2 * lo2
        xh = x_ref[:, half:] + c1 * hi1 + c2 * hi2
        ms = (jnp.sum(xl * xl, axis=-1, keepdims=True) + jnp.sum(xh * xh, axis=-1, keepdims=True)) / (2 * half)
        inv = lax.rsqrt(ms + EPS)
        hl = (xl * inv * g_ref[:, :half]).astype(BF16)
        hh = (xh * inv * g_ref[:, half:]).astype(BF16)
        gate = _dot(hl, wg_ref[:half, :]) + _dot(hh, wg_ref[half:, :])
        gate = 1.0 / (1.0 + jnp.exp(-gate))
        pp = _dot(p_ref[...].astype(BF16), wp_ref[...])
        ol = xl + gate[:, :half] * pp[:, :half]
        oh = xh + gate[:, half:] * pp[:, half:]
        ms2 = (jnp.sum(ol * ol, axis=-1, keepdims=True) + jnp.sum(oh * oh, axis=-1, keepdims=True)) / (2 * half)
        inv2 = lax.rsqrt(ms2 + EPS)
        nl = ol * inv2 * gn_ref[:, :half]
        nh = oh * inv2 * gn_ref[:, half:]
        if final:
            o_ref[:, :half] = nl
            o_ref[:, half:] = nh
        else:
            o_ref[:, :half] = ol
            o_ref[:, half:] = oh
            hn_ref[:, :half] = nl.astype(BF16)
            hn_ref[:, half:] = nh.astype(BF16)


def _combine_ple(x, meta, pos_tiles, ys, p_all, g_ple, w_gate_all, w_proj_all, layer, g_next, final):
    t, d = x.shape
    nt = pos_tiles.shape[0]
    tm = pos_tiles.shape[2] // 2
    pd = p_all.shape[2]
    const = lambda shape: pl.BlockSpec(shape, lambda i: (0, 0))
    prev = lambda w: pl.BlockSpec((tm, w), lambda i: (jnp.maximum(i - 1, 0), 0))
    return pl.pallas_call(
        functools.partial(_ple_kernel, final=final),
        grid=(nt + 1,),
        in_specs=[
            pl.BlockSpec((1, 1, 2 * tm), lambda i: (jnp.minimum(i, nt - 1), 0, 0), memory_space=pltpu.SMEM),
            prev(d), prev(LANES),
            pl.BlockSpec(memory_space=pl.ANY),
            pl.BlockSpec((None, tm, pd), lambda i: (layer, jnp.maximum(i - 1, 0), 0)),
            const((1, d)),
            pl.BlockSpec((None, d, d), lambda i: (layer, 0, 0)),
            pl.BlockSpec((None, pd, d), lambda i: (layer, 0, 0)),
            const((1, d)),
        ],
        out_specs=[prev(d)] if final else [prev(d), prev(d)],
        out_shape=[jax.ShapeDtypeStruct((t, d), F32)] + ([] if final else [jax.ShapeDtypeStruct((t, d), BF16)]),
        scratch_shapes=[pltpu.VMEM((2, 2 * tm * ROW_SUB, LANES), U32), pltpu.SemaphoreType.DMA((2,))],
        compiler_params=_cparams(1),
        name="combine_ple",
    )(pos_tiles, x, meta, ys, p_all, g_ple.reshape(1, d), w_gate_all, w_proj_all, g_next.reshape(1, d))


W_GA0 = 4 * RET_W + 2 * GLA_QK + 2 * GLA_V


PREP_COLS = 256
PREP_ALIGNED = W_GA0 // PREP_COLS
PREP_SHIFTED = (Z_GA - W_GA0) // PREP_COLS


def _prep_kernel(wt_ref, o_ref):
    j = pl.program_id(1)
    t = wt_ref[0].T

    @pl.when(j < PREP_ALIGNED + PREP_SHIFTED)
    def _():
        o_ref[...] = t.astype(BF16)

    @pl.when(j >= PREP_ALIGNED + PREP_SHIFTED)
    def _():
        lane = lax.broadcasted_iota(I32, t.shape, 1)
        o_ref[...] = jnp.where(lane < GLA_RANK, t, 0.0).astype(BF16)


def _prep_w_in(w):
    depth, d, n = w.shape
    wt = jnp.transpose(w, (0, 2, 1))

    def src_row(j):
        shifted = (W_GA0 + GLA_RANK) // 8 + (j - PREP_ALIGNED) * (PREP_COLS // 8)
        r8 = jnp.where(j < PREP_ALIGNED, j * (PREP_COLS // 8),
                       jnp.where(j < PREP_ALIGNED + PREP_SHIFTED, shifted, W_GA0 // 8))
        return r8 * 8

    return pl.pallas_call(
        _prep_kernel,
        grid=(depth, Z_W // PREP_COLS),
        in_specs=[pl.BlockSpec((pl.Element(1), pl.Element(PREP_COLS), pl.Element(d)),
                               lambda l, j: (l, src_row(j), 0))],
        out_specs=pl.BlockSpec((None, d, PREP_COLS), lambda l, j: (l, 0, j)),
        out_shape=jax.ShapeDtypeStruct((depth, d, Z_W), BF16),
        compiler_params=_cparams(2),
        name="w_in_layout",
    )(wt)


def _pad_lanes(parts, width, dtype):
    cat = jnp.concatenate(parts, axis=-1)
    pad = jnp.zeros(cat.shape[:-1] + (width - cat.shape[-1],), cat.dtype)
    return jnp.concatenate([cat, pad], axis=-1).astype(dtype)


def _routing_tables(meta_t, counts, n_tok, tile):
    eid = meta_t[:, 0:2, :].astype(I32)
    rank = meta_t[:, 2:4, :].astype(I32)
    cnt = counts[0, ROUTE_LANE0:ROUTE_LANE0 + N_EXPERTS].astype(I32)
    padded = ((cnt + tile - 1) // tile) * tile
    ends = jnp.cumsum(padded)
    offs = ends - padded
    base = jnp.zeros_like(eid)
    for e in range(N_EXPERTS):
        base = jnp.where(eid == e, offs[e], base)
    pos = base + rank
    n_rows = 2 * n_tok + N_EXPERTS * tile
    nt = n_rows // tile
    tile_start = jnp.arange(nt, dtype=I32) * tile
    tile_expert = jnp.minimum(jnp.sum((tile_start[:, None] >= ends[None, :]).astype(I32), axis=1),
                              N_EXPERTS - 1)
    n_used = (ends[-1] // tile).reshape(1)
    return pos, tile_expert, n_used, offs + cnt, padded - cnt, n_rows


def kernel(x, p, g_mix, w_in, gla_w_alpha, gla_b_alpha, rel_bias, mix_scale, w_out, g_ffn,
           w_router_group, b_router_group, w_router_expert, b_router_expert,
           w_expert_gate, w_expert_up, w_expert_down, g_ple, w_ple_gate, w_ple_proj, g_final):
    batch, seq, d = x.shape
    depth = g_mix.shape[0]
    t = batch * seq
    assert seq % SUPER == 0 and d == 2 * ROW_SUB * LANES
    tok_tile = min(t, TOK_TILE)
    ple_tile = min(t, PLE_TILE)
    xf = x.reshape(t, d)
    cos2, sin2 = _rotary_tables(seq)
    w_in_r = _prep_w_in(w_in)
    w_out_b = w_out.astype(BF16)
    wa_all = jnp.concatenate(
        [gla_w_alpha, jnp.zeros((depth, LANES - GLA_RANK, GLA_QK), gla_w_alpha.dtype)], axis=1).astype(BF16)
    wr_all = _pad_lanes([w_router_group, w_router_expert.reshape(depth, d, N_EXPERTS)], LANES, BF16)
    br_all = _pad_lanes([b_router_group, b_router_expert.reshape(depth, N_EXPERTS)], LANES, F32)
    br_all = br_all.reshape(depth, 1, LANES)
    p_all = p.reshape(depth, t, p.shape[-1])
    w_pg_b = w_ple_gate.astype(BF16)
    w_pp_b = w_ple_proj.astype(BF16)
    bias_all = _att_bias(rel_bias)
    h = _norm_cast(xf, g_mix[0])
    for i in range(depth):
        z = _in_proj(h, w_in_r, i)
        ms = mix_scale[i]
        o_ret, o_gla, o_att = _mixers(z, cos2, sin2, wa_all, gla_b_alpha[i], bias_all, ms, i, batch, seq)
        x1, hp, meta, meta_t, counts = _out_route(xf, o_ret, o_gla, o_att, w_out_b, g_ffn[i], wr_all, br_all, i)
        pos, tile_expert, n_used, pad_start, pad_n, n_rows = _routing_tables(meta_t, counts, t, MOE_TILE)
        xs = _dispatch(hp, pos.reshape(t // tok_tile, 1, 2 * tok_tile), pad_start, pad_n, n_used, n_rows)
        ys = _moe(xs, tile_expert, n_used, w_expert_gate, w_expert_up, w_expert_down, i)
        r = tok_tile // ple_tile
        pos_ple = pos.reshape(t // tok_tile, 2, r, ple_tile).transpose(0, 2, 1, 3).reshape(t // ple_tile, 1, 2 * ple_tile)
        final = i == depth - 1
        outs = _combine_ple(x1, meta, pos_ple, ys, p_all, g_ple[i], w_pg_b, w_pp_b, i,
                            g_final if final else g_mix[i + 1], final=final)
        xf = outs[0]
        if not final:
            h = outs[1]
    return xf.reshape(batch, seq, d)
```

```python
import functools
import math

import numpy as np
import jax
import jax.numpy as jnp
from jax import lax
from jax.experimental import pallas as pl
from jax.experimental.pallas import tpu as pltpu

F32 = jnp.float32
BF16 = jnp.bfloat16
U32 = jnp.uint32
I32 = jnp.int32

CHUNK = 64
HEAD_DIM = 128
RET_HEADS = 6
GLA_HEADS = 4
GLA_DK = 64
GLA_DV = 128
GLA_RANK = 16
GLA_TAU = 16.0
ATT_HEADS = 6
ATT_BAND_CHUNKS = 8
MAX_REL = 128
N_GROUPS = 4
EXPERTS_PER_GROUP = 8
N_EXPERTS = N_GROUPS * EXPERTS_PER_GROUP
EXPERT_FF = 256
EPS = 1e-6

RET_W = RET_HEADS * HEAD_DIM
GLA_QK = GLA_HEADS * GLA_DK
GLA_V = GLA_HEADS * GLA_DV
ATT_W = ATT_HEADS * HEAD_DIM

LANES = 128
V7X_VMEM_LIMIT = 56 * 1024 * 1024

Z_RET = 0
Z_GQ = 4 * RET_W
Z_GK = Z_GQ + GLA_QK
Z_GV = Z_GK + GLA_QK
Z_GG = Z_GV + GLA_V
Z_ATT = Z_GG + GLA_V
Z_GA = Z_ATT + 3 * ATT_W
Z_W = 7 * 1024

SUPER = 4 * CHUNK
ROUTE_LANE0 = N_GROUPS
MOE_TILE = 512
TOK_TILE = 512
PLE_TILE = 512
ROW_SUB = 8
PROJ_ROWS = 2048
PROJ_COLS = 1024
NORM_TILE = 512
ROT_ROWS = 1024


def _cparams(n_axes):
    return pltpu.CompilerParams(
        dimension_semantics=("arbitrary",) * n_axes,
        vmem_limit_bytes=V7X_VMEM_LIMIT,
    )


def _dot(a, b):
    return jnp.dot(a, b, preferred_element_type=F32)


def _dot_nt(a, b):
    return lax.dot_general(a, b, (((1,), (1,)), ((), ())), preferred_element_type=F32)


def _dot_tn(a, b):
    return lax.dot_general(a, b, (((0,), (0,)), ((), ())), preferred_element_type=F32)


def _rms(x, g):
    return x * lax.rsqrt(jnp.mean(x * x, axis=-1, keepdims=True) + EPS) * g


def _silu(x):
    return x / (1.0 + jnp.exp(-x))


def _pack_bf16_pair(lo, hi):
    lo_b = lax.bitcast_convert_type(lo.astype(BF16).astype(F32), U32)
    hi_b = lax.bitcast_convert_type(hi.astype(BF16).astype(F32), U32)
    return (lo_b >> 16) | (hi_b & jnp.uint32(0xFFFF0000))


def _unpack_bf16_pair(w):
    lo = lax.bitcast_convert_type(w << 16, F32)
    hi = lax.bitcast_convert_type(w & jnp.uint32(0xFFFF0000), F32)
    return lo, hi


def _rot_kernel(inv_ref, cos_ref, sin_ref):
    rows = cos_ref.shape[0]
    pos = (lax.broadcasted_iota(I32, (rows, LANES), 0) + pl.program_id(0) * rows).astype(F32)
    lane = lax.broadcasted_iota(I32, (rows, LANES), 1)
    ang = pos * inv_ref[...]
    cos_ref[...] = jnp.cos(ang)
    s = jnp.sin(ang)
    sin_ref[...] = jnp.where(lane < HEAD_DIM // 2, -s, s)


def _rotary_tables(seq):
    half = HEAD_DIM // 2
    inv = np.float32(1.0) / (np.float32(10000.0) ** (np.arange(half, dtype=np.float32) / np.float32(half)))
    inv2 = jnp.asarray(np.concatenate([inv, inv]).reshape(1, LANES).astype(np.float32))
    rows = min(seq, ROT_ROWS)
    return pl.pallas_call(
        _rot_kernel,
        grid=(seq // rows,),
        in_specs=[pl.BlockSpec((1, LANES), lambda i: (0, 0))],
        out_specs=[pl.BlockSpec((rows, LANES), lambda i: (i, 0))] * 2,
        out_shape=[jax.ShapeDtypeStruct((seq, LANES), F32)] * 2,
        compiler_params=_cparams(1),
        name="rotary_tables",
    )(inv2)


def _norm_kernel(x_ref, g_ref, h_ref):
    h_ref[...] = _rms(x_ref[...], g_ref[...]).astype(BF16)


def _norm_cast(x, g):
    t, d = x.shape
    tm = min(t, NORM_TILE)
    return pl.pallas_call(
        _norm_kernel,
        grid=(t // tm,),
        in_specs=[pl.BlockSpec((tm, d), lambda i: (i, 0)), pl.BlockSpec((1, d), lambda i: (0, 0))],
        out_specs=pl.BlockSpec((tm, d), lambda i: (i, 0)),
        out_shape=jax.ShapeDtypeStruct((t, d), BF16),
        compiler_params=_cparams(1),
        name="norm_cast",
    )(x, g.reshape(1, d))


def _in_kernel(h_ref, w_ref, z_ref):
    z_ref[...] = _dot(h_ref[...], w_ref[...]).astype(BF16)


def _in_proj(h, w_all, layer):
    t, d = h.shape
    n = w_all.shape[2]
    tm = min(t, PROJ_ROWS)
    tn = min(n, PROJ_COLS)
    return pl.pallas_call(
        _in_kernel,
        grid=(t // tm, n // tn),
        in_specs=[
            pl.BlockSpec((tm, d), lambda i, j: (i, 0)),
            pl.BlockSpec((None, d, tn), lambda i, j: (layer, 0, j)),
        ],
        out_specs=pl.BlockSpec((tm, tn), lambda i, j: (i, j)),
        out_shape=jax.ShapeDtypeStruct((t, n), BF16),
        compiler_params=_cparams(2),
        name="in_proj",
    )(h, w_all)


def _ret_log_gamma(h):
    return math.log(1.0 - 2.0 ** (-5.0 - h))


def _ret_init(state_ref, dmat_ref, xi_ref, zeta_ref):
    state_ref[...] = jnp.zeros_like(state_ref)
    row = lax.broadcasted_iota(I32, (SUPER, SUPER), 0)
    col = lax.broadcasted_iota(I32, (SUPER, SUPER), 1)
    dist = jnp.abs(row - col).astype(F32)
    keep = (col <= row) | ((row // CHUNK) == (col // CHUNK))
    t = lax.broadcasted_iota(I32, (SUPER, HEAD_DIM), 0).astype(F32)
    for h in range(RET_HEADS):
        lg = _ret_log_gamma(h)
        dmat_ref[h] = jnp.where(keep, jnp.exp(lg * dist), 0.0)
        xi_ref[h] = jnp.exp(lg * (t + 1.0))
        zeta_ref[h] = jnp.exp(lg * (SUPER - 1.0 - t))


def _ret_head(h, q_ref, k_ref, v_ref, g_ref, cos, sin, ms_ref, o_ref, state_ref, dmat_ref, xi_ref, zeta_ref):
    sl = slice(h * HEAD_DIM, (h + 1) * HEAD_DIM)
    scale = HEAD_DIM ** -0.5
    q = q_ref[:, sl].astype(F32)
    k = k_ref[:, sl].astype(F32)
    vb = v_ref[:, sl]
    qr = q * cos + pltpu.roll(q, HEAD_DIM // 2, 1) * sin
    kr = (k * cos + pltpu.roll(k, HEAD_DIM // 2, 1) * sin) * scale
    qb = qr.astype(BF16)
    kb = kr.astype(BF16)
    sc = _dot_nt(qb, kb) * dmat_ref[h]
    intra = _dot(sc.astype(BF16), vb)
    st = state_ref[h]
    cross = _dot(qb, st.astype(BF16)) * xi_ref[h]
    kz = (kr * zeta_ref[h]).astype(BF16)
    upd = _dot_tn(kz, vb)
    state_ref[h] = math.exp(_ret_log_gamma(h) * SUPER) * st + upd
    o = intra + cross
    mu = jnp.mean(o, axis=-1, keepdims=True)
    oc = o - mu
    var = jnp.mean(oc * oc, axis=-1, keepdims=True)
    on = oc * lax.rsqrt(var + EPS)
    gate = _silu(g_ref[:, sl].astype(F32))
    o_ref[:, sl] = (on * gate * ms_ref[:, sl]).astype(BF16)


def _gla_log_alpha(a_ref, wa_ref, ba_ref):
    pre = _dot(a_ref[...], wa_ref[...]) + ba_ref[...]
    return (jnp.minimum(pre, 0.0) - jnp.log(1.0 + jnp.exp(-jnp.abs(pre)))) * (1.0 / GLA_TAU)


def _gla_chunk(ci, la, q_ref, k_ref, v_ref, g_ref, ms_ref, o_ref, state_ref):
    r = lax.broadcasted_iota(I32, (CHUNK, CHUNK), 0)
    c = lax.broadcasted_iota(I32, (CHUNK, CHUNK), 1)
    causal = c <= r
    tril = jnp.where(causal, 1.0, 0.0).astype(BF16)
    ones = jnp.ones((CHUNK, GLA_DV), BF16)
    scale = GLA_DK ** -0.5
    rows = slice(ci * CHUNK, (ci + 1) * CHUNK)
    la_c = la[rows, :]
    hi = la_c.astype(BF16)
    lo = (la_c - hi.astype(F32)).astype(BF16)
    bcs = _dot(tril, hi) + _dot(tril, lo)
    gl = _dot_tn(hi, ones) + _dot_tn(lo, ones)
    blast = bcs[CHUNK - 1:CHUNK, :]
    eb = jnp.exp(bcs)
    enb = jnp.exp(-bcs)
    ekb = jnp.exp(blast - bcs)
    qc = q_ref[rows, :].astype(F32) * scale * eb
    kc = k_ref[rows, :].astype(F32)
    kin = (kc * enb).astype(BF16)
    kup = (kc * ekb).astype(BF16)
    qin = qc.astype(BF16)
    for h in range(GLA_HEADS):
        ks = slice(h * GLA_DK, (h + 1) * GLA_DK)
        vs = slice(h * GLA_DV, (h + 1) * GLA_DV)
        vb = v_ref[rows, vs]
        a = jnp.where(causal, _dot_nt(qin[:, ks], kin[:, ks]), 0.0)
        intra = _dot(a.astype(BF16), vb)
        st = state_ref[h]
        cross = _dot(qin[:, ks], st.astype(BF16))
        upd = _dot_tn(kup[:, ks], vb)
        state_ref[h] = jnp.exp(gl[ks, :]) * st + upd
        o = intra + cross
        on = o * lax.rsqrt(jnp.mean(o * o, axis=-1, keepdims=True) + EPS)
        gate = _silu(g_ref[rows, vs].astype(F32))
        o_ref[rows, vs] = (on * gate * ms_ref[:, vs]).astype(BF16)


BAND = 3 * SUPER


BAND_KINDS = 3
ROLL_W = 1024
NEG_INF = -1e30


def _bias_kernel(rb_ref, o_ref):
    l = pl.program_id(0)
    h = pl.program_id(1)
    m = lax.broadcasted_iota(I32, (8, ROLL_W), 1)
    d = jnp.where(m < BAND, m, m - ROLL_W)
    idx = jnp.clip(2 * SUPER - d, -MAX_REL, MAX_REL) + MAX_REL

    n_rel = 2 * MAX_REL + 1
    base = (l * ATT_HEADS + h) * n_rel

    def body(j, acc):
        return jnp.where(idx == j, rb_ref[base + j], acc)

    diag = lax.fori_loop(0, n_rel, body, jnp.zeros((8, ROLL_W), F32), unroll=8)
    full = pltpu.roll(jnp.broadcast_to(diag[0:1, :], (SUPER, ROLL_W)), 0, 1, stride=1, stride_axis=0)
    bias = full[:, :BAND]
    row = lax.broadcasted_iota(I32, (SUPER, BAND), 0)
    col = lax.broadcasted_iota(I32, (SUPER, BAND), 1)
    dist = 2 * (SUPER // CHUNK) + row // CHUNK - col // CHUNK
    in_band = (dist >= 0) & (dist <= ATT_BAND_CHUNKS)
    for kind in range(BAND_KINDS):
        first_valid_col = (BAND_KINDS - 1 - kind) * SUPER
        o_ref[kind, 0] = jnp.where(in_band & (col >= first_valid_col), bias, NEG_INF)


def _att_bias(rel_bias):
    depth = rel_bias.shape[0]
    return pl.pallas_call(
        _bias_kernel,
        grid=(depth, ATT_HEADS),
        in_specs=[pl.BlockSpec(memory_space=pltpu.SMEM)],
        out_specs=pl.BlockSpec((None, BAND_KINDS, 1, SUPER, BAND), lambda l, h: (l, 0, h, 0, 0)),
        out_shape=jax.ShapeDtypeStruct((depth, BAND_KINDS, ATT_HEADS, SUPER, BAND), F32),
        compiler_params=_cparams(2),
        name="att_bias",
    )(rel_bias.reshape(-1))


def _att_head(h, q_ref, k_refs, v_refs, bias_ref, ms_ref, o_ref):
    sl = slice(h * HEAD_DIM, (h + 1) * HEAD_DIM)
    scale = HEAD_DIM ** -0.5
    qb = (q_ref[:, sl].astype(F32) * scale).astype(BF16)
    sc = [_dot_nt(qb, k_refs[j][:, sl]) + bias_ref[h, :, j * SUPER:(j + 1) * SUPER] for j in range(3)]
    m = jnp.max(jnp.maximum(jnp.maximum(sc[0], sc[1]), sc[2]), axis=-1, keepdims=True)
    ps = [jnp.exp(sj - m) for sj in sc]
    den = jnp.sum(ps[0] + ps[1] + ps[2], axis=-1, keepdims=True)
    acc = _dot(ps[0].astype(BF16), v_refs[0][:, sl])
    for j in (1, 2):
        acc = acc + _dot(ps[j].astype(BF16), v_refs[j][:, sl])
    o = acc / den
    on = o * lax.rsqrt(jnp.mean(o * o, axis=-1, keepdims=True) + EPS)
    o_ref[:, sl] = (on * ms_ref[:, sl]).astype(BF16)


def _mixer_kernel(rq_ref, rk_ref, rv_ref, rg_ref, cos_ref, sin_ref, rms_ref,
                  gq_ref, gk_ref, gv_ref, gg_ref, ga_ref, wa_ref, ba_ref, gms_ref,
                  aq_ref, k0_ref, k1_ref, k2_ref, v0_ref, v1_ref, v2_ref, bias_ref, ams_ref,
                  oret_ref, ogla_ref, oatt_ref,
                  rstate_ref, dmat_ref, xi_ref, zeta_ref, gstate_ref):
    @pl.when(pl.program_id(1) == 0)
    def _():
        _ret_init(rstate_ref, dmat_ref, xi_ref, zeta_ref)
        gstate_ref[...] = jnp.zeros_like(gstate_ref)

    cos = cos_ref[...]
    sin = sin_ref[...]
    la = _gla_log_alpha(ga_ref, wa_ref, ba_ref)
    k_refs = (k0_ref, k1_ref, k2_ref)
    v_refs = (v0_ref, v1_ref, v2_ref)
    n_chunks = SUPER // CHUNK
    for h in range(max(RET_HEADS, ATT_HEADS)):
        if h < RET_HEADS:
            _ret_head(h, rq_ref, rk_ref, rv_ref, rg_ref, cos, sin, rms_ref, oret_ref,
                      rstate_ref, dmat_ref, xi_ref, zeta_ref)
        if h < ATT_HEADS:
            _att_head(h, aq_ref, k_refs, v_refs, bias_ref, ams_ref, oatt_ref)
        if h < n_chunks:
            _gla_chunk(h, la, gq_ref, gk_ref, gv_ref, gg_ref, gms_ref, ogla_ref, gstate_ref)


def _mixers(z, cos2, sin2, wa_all, ba, bias, ms, layer, batch, seq):
    t = z.shape[0]
    ns = seq // SUPER
    row = lambda b, s: b * ns + s
    zspec = lambda width, off: pl.BlockSpec((SUPER, width), lambda b, s: (row(b, s), off // width))
    const = lambda width: pl.BlockSpec((1, width), lambda b, s: (0, 0))

    def kv_spec(off, back):
        return pl.BlockSpec((SUPER, ATT_W), lambda b, s: (b * ns + jnp.maximum(s - back, 0), off // ATT_W))

    ak, av = Z_ATT + ATT_W, Z_ATT + 2 * ATT_W
    return pl.pallas_call(
        _mixer_kernel,
        grid=(batch, ns),
        in_specs=[
            zspec(RET_W, 0), zspec(RET_W, RET_W), zspec(RET_W, 2 * RET_W), zspec(RET_W, 3 * RET_W),
            pl.BlockSpec((SUPER, LANES), lambda b, s: (s, 0)),
            pl.BlockSpec((SUPER, LANES), lambda b, s: (s, 0)),
            const(RET_W),
            zspec(GLA_QK, Z_GQ), zspec(GLA_QK, Z_GK), zspec(GLA_V, Z_GV), zspec(GLA_V, Z_GG),
            zspec(LANES, Z_GA),
            pl.BlockSpec((None, LANES, GLA_QK), lambda b, s: (layer, 0, 0)),
            const(GLA_QK), const(GLA_V),
            zspec(ATT_W, Z_ATT),
            kv_spec(ak, 2), kv_spec(ak, 1), kv_spec(ak, 0),
            kv_spec(av, 2), kv_spec(av, 1), kv_spec(av, 0),
            pl.BlockSpec((None, None, ATT_HEADS, SUPER, BAND),
                         lambda b, s: (layer, jnp.minimum(s, BAND_KINDS - 1), 0, 0, 0)),
            const(ATT_W),
        ],
        out_specs=[
            pl.BlockSpec((SUPER, RET_W), lambda b, s: (row(b, s), 0)),
            pl.BlockSpec((SUPER, GLA_V), lambda b, s: (row(b, s), 0)),
            pl.BlockSpec((SUPER, ATT_W), lambda b, s: (row(b, s), 0)),
        ],
        out_shape=[
            jax.ShapeDtypeStruct((t, RET_W), BF16),
            jax.ShapeDtypeStruct((t, GLA_V), BF16),
            jax.ShapeDtypeStruct((t, ATT_W), BF16),
        ],
        scratch_shapes=[
            pltpu.VMEM((RET_HEADS, HEAD_DIM, HEAD_DIM), F32),
            pltpu.VMEM((RET_HEADS, SUPER, SUPER), F32),
            pltpu.VMEM((RET_HEADS, SUPER, HEAD_DIM), F32),
            pltpu.VMEM((RET_HEADS, SUPER, HEAD_DIM), F32),
            pltpu.VMEM((GLA_HEADS, GLA_DK, GLA_DV), F32),
        ],
        compiler_params=_cparams(2),
        name="mixers",
    )(z, z, z, z, cos2, sin2, ms[:RET_W].reshape(1, RET_W),
      z, z, z, z, z, wa_all, ba.reshape(1, GLA_QK), ms[RET_W:RET_W + GLA_V].reshape(1, GLA_V),
      z, z, z, z, z, z, z, bias, ms[RET_W + GLA_V:].reshape(1, ATT_W))


def _store_token_tiles(ref, packed):
    tm = packed.shape[0]
    for s in range(ROW_SUB):
        ref[pl.ds(s, tm, stride=ROW_SUB), :] = packed[:, s * LANES:(s + 1) * LANES]


def _load_token_tiles(ref, tm):
    return jnp.concatenate([ref[pl.ds(s, tm, stride=ROW_SUB), :] for s in range(ROW_SUB)], axis=1)


def _tile_rows(ref, row, n_rows=1):
    start = row * ROW_SUB
    if not isinstance(start, int):
        start = pl.multiple_of(start, ROW_SUB)
    return ref.at[pl.ds(start, n_rows * ROW_SUB), :]


def _route_kernel(x_ref, a_ref, b_ref, c_ref, w_ref, g_ref, wr_ref, br_ref,
                  x1_ref, hp_ref, meta_ref, metat_ref, cnt_ref, stage_ref, run_ref):
    i = pl.program_id(0)
    tm = x_ref.shape[0]
    half = x_ref.shape[1] // 2

    @pl.when(i == 0)
    def _():
        run_ref[...] = jnp.zeros_like(run_ref)
        stage_ref[...] = jnp.zeros_like(stage_ref)

    xr = stage_ref[...]
    live = jnp.where(i > 0, 1.0, 0.0)

    h = _rms(xr, g_ref[...])
    _store_token_tiles(hp_ref, _pack_bf16_pair(h[:, :half], h[:, half:]))
    lg = _dot(h.astype(BF16), wr_ref[...]) + br_ref[...]
    lane = lax.broadcasted_iota(I32, (tm, LANES), 1)
    neg = jnp.float32(-1e30)
    big = jnp.int32(LANES)
    gl = jnp.where(lane < N_GROUPS, lg, neg)
    gm = jnp.max(gl, axis=-1, keepdims=True)
    p_group = 1.0 / jnp.sum(jnp.exp(gl - gm), axis=-1, keepdims=True)
    g_idx = jnp.min(jnp.where(gl == gm, lane, big), axis=-1, keepdims=True)
    lo = ROUTE_LANE0 + EXPERTS_PER_GROUP * g_idx
    el = jnp.where((lane >= lo) & (lane < lo + EXPERTS_PER_GROUP), lg, neg)
    m1 = jnp.max(el, axis=-1, keepdims=True)
    i1 = jnp.min(jnp.where(el == m1, lane, big), axis=-1, keepdims=True)
    el2 = jnp.where(lane == i1, neg, el)
    m2 = jnp.max(el2, axis=-1, keepdims=True)
    i2 = jnp.min(jnp.where(el2 == m2, lane, big), axis=-1, keepdims=True)
    e2 = jnp.exp(m2 - m1)
    c1 = p_group / (1.0 + e2)
    c2 = p_group * e2 / (1.0 + e2)
    oh1 = jnp.where(lane == i1, 1.0, 0.0)
    oh2 = jnp.where(lane == i2, 1.0, 0.0)
    oh = (oh1 + oh2) * live
    r = lax.broadcasted_iota(I32, (tm, tm), 0)
    c = lax.broadcasted_iota(I32, (tm, tm), 1)
    stril = jnp.where(c < r, 1.0, 0.0).astype(BF16)
    before = _dot(stril, oh.astype(BF16)) + run_ref[0:1, :]
    rank1 = jnp.sum(before * oh1, axis=-1, keepdims=True)
    rank2 = jnp.sum(before * oh2, axis=-1, keepdims=True)
    run_ref[0:1, :] = run_ref[0:1, :] + jnp.sum(oh, axis=0, keepdims=True)
    cnt_ref[...] = run_ref[...]
    e1f = (i1 - ROUTE_LANE0).astype(F32)
    e2f = (i2 - ROUTE_LANE0).astype(F32)
    meta = jnp.where(lane == 0, e1f, 0.0)
    meta = jnp.where(lane == 1, e2f, meta)
    meta = jnp.where(lane == 2, rank1, meta)
    meta = jnp.where(lane == 3, rank2, meta)
    meta = jnp.where(lane == 4, c1, meta)
    meta = jnp.where(lane == 5, c2, meta)
    meta_ref[...] = meta
    metat_ref[0] = meta.T[0:8, :]

    x1 = (x_ref[...] + _dot(a_ref[...], w_ref[:RET_W, :]) + _dot(b_ref[...], w_ref[RET_W:RET_W + GLA_V, :])
          + _dot(c_ref[...], w_ref[RET_W + GLA_V:, :]))
    x1_ref[...] = x1
    stage_ref[...] = x1


def _out_route(x, o_ret, o_gla, o_att, w_all, g, wr_all, br_all, layer):
    t, d = x.shape
    tm = min(t, TOK_TILE)
    nt = t // tm
    cur = lambda w: pl.BlockSpec((tm, w), lambda i: (jnp.minimum(i, nt - 1), 0))
    prev = lambda i: jnp.maximum(i - 1, 0)
    return pl.pallas_call(
        _route_kernel,
        grid=(nt + 1,),
        in_specs=[
            cur(d), cur(RET_W), cur(GLA_V), cur(ATT_W),
            pl.BlockSpec((None, w_all.shape[1], d), lambda i: (layer, 0, 0), pipeline_mode=pl.Buffered(1)),
            pl.BlockSpec((1, d), lambda i: (0, 0)),
            pl.BlockSpec((None, d, LANES), lambda i: (layer, 0, 0)),
            pl.BlockSpec((None, 1, LANES), lambda i: (layer, 0, 0)),
        ],
        out_specs=[
            cur(d),
            pl.BlockSpec((tm * ROW_SUB, LANES), lambda i: (prev(i), 0)),
            pl.BlockSpec((tm, LANES), lambda i: (prev(i), 0)),
            pl.BlockSpec((1, 8, tm), lambda i: (prev(i), 0, 0)),
            pl.BlockSpec((8, LANES), lambda i: (0, 0)),
        ],
        out_shape=[
            jax.ShapeDtypeStruct((t, d), F32),
            jax.ShapeDtypeStruct((t * ROW_SUB, LANES), U32),
            jax.ShapeDtypeStruct((t, LANES), F32),
            jax.ShapeDtypeStruct((nt, 8, tm), F32),
            jax.ShapeDtypeStruct((8, LANES), F32),
        ],
        scratch_shapes=[pltpu.VMEM((tm, d), F32), pltpu.VMEM((8, LANES), F32)],
        compiler_params=_cparams(1),
        name="out_proj_router",
    )(x, o_ret, o_gla, o_att, w_all, g.reshape(1, d), wr_all, br_all)


_PAD_BITS = tuple(reversed(range(MOE_TILE.bit_length() - 1)))


def _dispatch_kernel(ps_ref, pn_ref, nu_ref, pos_ref, hp_ref, xs_ref, stage_ref, zero_ref, lsem_ref, sem_ref,
                     zsem_ref):
    i = pl.program_id(0)
    tm = pos_ref.shape[2] // 2
    zrows = zero_ref.shape[0] // ROW_SUB
    n_tiles = xs_ref.shape[0] // (ROW_SUB * MOE_TILE)

    def pad_copies(start):
        def zero_copy(off, k):
            cp = pltpu.make_async_copy(_tile_rows(zero_ref, 0, k), _tile_rows(xs_ref, off, k), zsem_ref.at[0])
            if start:
                cp.start()
            else:
                cp.wait()

        def body(e, carry):
            n = pn_ref[e]
            base = ps_ref[e]
            for bit in _PAD_BITS:
                k = 1 << bit

                @pl.when((n & k) != 0)
                def _():
                    zero_copy(base + ((n >> (bit + 1)) << (bit + 1)), k)
            return carry

        lax.fori_loop(0, N_EXPERTS, body, 0)

        def tail(tile, carry):
            for part in range(MOE_TILE // zrows):
                zero_copy(tile * MOE_TILE + part * zrows, zrows)
            return carry

        lax.fori_loop(nu_ref[0], n_tiles, tail, 0)

    n = pl.num_programs(0)

    def tile_load(tile, slot):
        return pltpu.make_async_copy(_tile_rows(hp_ref, tile * tm, tm), stage_ref.at[slot], lsem_ref.at[slot])

    @pl.when(i == 0)
    def _():
        zero_ref[...] = jnp.zeros_like(zero_ref)
        pad_copies(True)
        pad_copies(False)
        tile_load(0, 0).start()

    @pl.when(i + 1 < n)
    def _():
        tile_load(i + 1, (i + 1) % 3).start()

    tile_load(i, i % 3).wait()
    src = stage_ref.at[i % 3]
    for j in range(2 * tm):
        pltpu.make_async_copy(_tile_rows(src, j % tm), _tile_rows(xs_ref, pos_ref[0, 0, j]),
                              sem_ref.at[i % 2]).start(priority=j % 2)

    def wait_rows(slot):
        for j in range(2 * tm):
            pltpu.make_async_copy(_tile_rows(src, 0), _tile_rows(xs_ref, 0), sem_ref.at[slot]).wait()

    @pl.when(i > 0)
    def _():
        wait_rows((i + 1) % 2)

    @pl.when(i == n - 1)
    def _():
        wait_rows(i % 2)


def _dispatch(hp, pos_tiles, pad_start, pad_n, n_used, n_rows):
    nt = pos_tiles.shape[0]
    tm = pos_tiles.shape[2] // 2
    grid_spec = pltpu.PrefetchScalarGridSpec(
        num_scalar_prefetch=3,
        grid=(nt,),
        in_specs=[
            pl.BlockSpec((1, 1, 2 * tm), lambda i, ps, pn, nu: (i, 0, 0), memory_space=pltpu.SMEM),
            pl.BlockSpec(memory_space=pl.ANY),
        ],
        out_specs=pl.BlockSpec(memory_space=pl.ANY),
        scratch_shapes=[
            pltpu.VMEM((3, tm * ROW_SUB, LANES), U32),
            pltpu.VMEM((MOE_TILE // 2 * ROW_SUB, LANES), U32),
            pltpu.SemaphoreType.DMA((3,)),
            pltpu.SemaphoreType.DMA((2,)),
            pltpu.SemaphoreType.DMA((1,)),
        ],
    )
    return pl.pallas_call(
        _dispatch_kernel,
        grid_spec=grid_spec,
        out_shape=jax.ShapeDtypeStruct((n_rows * ROW_SUB, LANES), U32),
        compiler_params=_cparams(1),
        name="dispatch",
    )(pad_start, pad_n, n_used, pos_tiles, hp)


def _moe_kernel(te_ref, nu_ref, xs_ref, wg_ref, wu_ref, wd_ref, y_ref, wgb_ref, wub_ref, wdb_ref):
    i = pl.program_id(0)
    n_used = nu_ref[0]
    tm = xs_ref.shape[0] // ROW_SUB
    half = ROW_SUB * LANES
    e = te_ref[i]
    e_prev = te_ref[jnp.maximum(i - 1, 0)]

    @pl.when((i == 0) | (e != e_prev))
    def _():
        wgb_ref[...] = wg_ref[...].astype(BF16)
        wub_ref[...] = wu_ref[...].astype(BF16)
        wdb_ref[...] = wd_ref[...].astype(BF16)

    @pl.when(i < n_used)
    def _():
        lo, hi = _unpack_bf16_pair(_load_token_tiles(xs_ref, tm))
        xl = lo.astype(BF16)
        xh = hi.astype(BF16)
        a = _dot(xl, wgb_ref[:half, :]) + _dot(xh, wgb_ref[half:, :])
        u = _dot(xl, wub_ref[:half, :]) + _dot(xh, wub_ref[half:, :])
        hid = (_silu(a) * u).astype(BF16)
        y = _dot(hid, wdb_ref[...])
        _store_token_tiles(y_ref, _pack_bf16_pair(y[:, :half], y[:, half:]))

    @pl.when(i >= n_used)
    def _():
        y_ref[...] = jnp.zeros_like(y_ref)


def _moe(xs, tile_expert, n_used, wg, wu, wd, layer):
    nt = tile_expert.shape[0]
    tm = xs.shape[0] // ROW_SUB // nt
    d, ff = wg.shape[-2], wg.shape[-1]

    def w_spec(shape):
        return pl.BlockSpec((None, None, None) + shape,
                            lambda i, te, nu: (layer, te[i] // EXPERTS_PER_GROUP, te[i] % EXPERTS_PER_GROUP, 0, 0))

    grid_spec = pltpu.PrefetchScalarGridSpec(
        num_scalar_prefetch=2,
        grid=(nt,),
        in_specs=[
            pl.BlockSpec((tm * ROW_SUB, LANES), lambda i, te, nu: (jnp.minimum(i, nu[0] - 1), 0)),
            w_spec((d, ff)), w_spec((d, ff)), w_spec((ff, d)),
        ],
        out_specs=pl.BlockSpec((tm * ROW_SUB, LANES), lambda i, te, nu: (i, 0)),
        scratch_shapes=[
            pltpu.VMEM((d, ff), BF16),
            pltpu.VMEM((d, ff), BF16),
            pltpu.VMEM((ff, d), BF16),
        ],
    )
    return pl.pallas_call(
        _moe_kernel,
        grid_spec=grid_spec,
        out_shape=jax.ShapeDtypeStruct(xs.shape, U32),
        compiler_params=_cparams(1),
        name="expert_mlp",
    )(tile_expert, n_used, xs, wg, wu, wd)


def _ple_kernel(pos_ref, x_ref, meta_ref, y_ref, p_ref, g_ref, wg_ref, wp_ref, gn_ref, *rest, final):
    if final:
        o_ref, buf_ref, sem_ref = rest
    else:
        o_ref, hn_ref, buf_ref, sem_ref = rest
    i = pl.program_id(0)
    n = pl.num_programs(0) - 1
    tm = x_ref.shape[0]
    half = x_ref.shape[1] // 2

    for par in range(2):
        @pl.when((i < n) & (i % 2 == par))
        def _():
            for j in range(2 * tm):
                pltpu.make_async_copy(_tile_rows(y_ref, pos_ref[0, 0, j]), _tile_rows(buf_ref.at[par], j),
                                      sem_ref.at[par]).start(priority=j % 2)

    @pl.when(i > 0)
    def _():
        slot = (i + 1) % 2
        for j in range(2 * tm):
            pltpu.make_async_copy(_tile_rows(y_ref, 0), _tile_rows(buf_ref.at[slot], j), sem_ref.at[slot]).wait()
        rows = _load_token_tiles(buf_ref.at[slot], 2 * tm)
        meta = meta_ref[...]
        c1 = meta[:, 4:5]
        c2 = meta[:, 5:6]
        lo1, hi1 = _unpack_bf16_pair(rows[:tm])
        lo2, hi2 = _unpack_bf16_pair(rows[tm:])
        xl = x_ref[:, :half] + c1 * lo1 + c2 * lo2
        xh = x_ref[:, half:] + c1 * hi1 + c2 * hi2
        ms = (jnp.sum(xl * xl, axis=-1, keepdims=True) + jnp.sum(xh * xh, axis=-1, keepdims=True)) / (2 * half)
        inv = lax.rsqrt(ms + EPS)
        hl = (xl * inv * g_ref[:, :half]).astype(BF16)
        hh = (xh * inv * g_ref[:, half:]).astype(BF16)
        gate = _dot(hl, wg_ref[:half, :]) + _dot(hh, wg_ref[half:, :])
        gate = 1.0 / (1.0 + jnp.exp(-gate))
        pp = _dot(p_ref[...].astype(BF16), wp_ref[...])
        ol = xl + gate[:, :half] * pp[:, :half]
        oh = xh + gate[:, half:] * pp[:, half:]
        ms2 = (jnp.sum(ol * ol, axis=-1, keepdims=True) + jnp.sum(oh * oh, axis=-1, keepdims=True)) / (2 * half)
        inv2 = lax.rsqrt(ms2 + EPS)
        nl = ol * inv2 * gn_ref[:, :half]
        nh = oh * inv2 * gn_ref[:, half:]
        if final:
            o_ref[:, :half] = nl
            o_ref[:, half:] = nh
        else:
            o_ref[:, :half] = ol
            o_ref[:, half:] = oh
            hn_ref[:, :half] = nl.astype(BF16)
            hn_ref[:, half:] = nh.astype(BF16)


def _combine_ple(x, meta, pos_tiles, ys, p_all, g_ple, w_gate_all, w_proj_all, layer, g_next, final):
    t, d = x.shape
    nt = pos_tiles.shape[0]
    tm = pos_tiles.shape[2] // 2
    pd = p_all.shape[2]
    const = lambda shape: pl.BlockSpec(shape, lambda i: (0, 0))
    prev = lambda w: pl.BlockSpec((tm, w), lambda i: (jnp.maximum(i - 1, 0), 0))
    return pl.pallas_call(
        functools.partial(_ple_kernel, final=final),
        grid=(nt + 1,),
        in_specs=[
            pl.BlockSpec((1, 1, 2 * tm), lambda i: (jnp.minimum(i, nt - 1), 0, 0), memory_space=pltpu.SMEM),
            prev(d), prev(LANES),
            pl.BlockSpec(memory_space=pl.ANY),
            pl.BlockSpec((None, tm, pd), lambda i: (layer, jnp.maximum(i - 1, 0), 0)),
            const((1, d)),
            pl.BlockSpec((None, d, d), lambda i: (layer, 0, 0)),
            pl.BlockSpec((None, pd, d), lambda i: (layer, 0, 0)),
            const((1, d)),
        ],
        out_specs=[prev(d)] if final else [prev(d), prev(d)],
        out_shape=[jax.ShapeDtypeStruct((t, d), F32)] + ([] if final else [jax.ShapeDtypeStruct((t, d), BF16)]),
        scratch_shapes=[pltpu.VMEM((2, 2 * tm * ROW_SUB, LANES), U32), pltpu.SemaphoreType.DMA((2,))],
        compiler_params=_cparams(1),
        name="combine_ple",
    )(pos_tiles, x, meta, ys, p_all, g_ple.reshape(1, d), w_gate_all, w_proj_all, g_next.reshape(1, d))


W_GA0 = 4 * RET_W + 2 * GLA_QK + 2 * GLA_V


PREP_COLS = 256
PREP_ALIGNED = W_GA0 // PREP_COLS
PREP_SHIFTED = (Z_GA - W_GA0) // PREP_COLS


def _prep_kernel(wt_ref, o_ref):
    j = pl.program_id(1)
    t = wt_ref[0].T

    @pl.when(j < PREP_ALIGNED + PREP_SHIFTED)
    def _():
        o_ref[...] = t.astype(BF16)

    @pl.when(j >= PREP_ALIGNED + PREP_SHIFTED)
    def _():
        lane = lax.broadcasted_iota(I32, t.shape, 1)
        o_ref[...] = jnp.where(lane < GLA_RANK, t, 0.0).astype(BF16)


def _prep_w_in(w):
    depth, d, n = w.shape
    wt = jnp.transpose(w, (0, 2, 1))

    def src_row(j):
        shifted = (W_GA0 + GLA_RANK) // 8 + (j - PREP_ALIGNED) * (PREP_COLS // 8)
        r8 = jnp.where(j < PREP_ALIGNED, j * (PREP_COLS // 8),
                       jnp.where(j < PREP_ALIGNED + PREP_SHIFTED, shifted, W_GA0 // 8))
        return r8 * 8

    return pl.pallas_call(
        _prep_kernel,
        grid=(depth, Z_W // PREP_COLS),
        in_specs=[pl.BlockSpec((pl.Element(1), pl.Element(PREP_COLS), pl.Element(d)),
                               lambda l, j: (l, src_row(j), 0))],
        out_specs=pl.BlockSpec((None, d, PREP_COLS), lambda l, j: (l, 0, j)),
        out_shape=jax.ShapeDtypeStruct((depth, d, Z_W), BF16),
        compiler_params=_cparams(2),
        name="w_in_layout",
    )(wt)


def _pad_lanes(parts, width, dtype):
    cat = jnp.concatenate(parts, axis=-1)
    pad = jnp.zeros(cat.shape[:-1] + (width - cat.shape[-1],), cat.dtype)
    return jnp.concatenate([cat, pad], axis=-1).astype(dtype)


def _routing_tables(meta_t, counts, n_tok, tile):
    eid = meta_t[:, 0:2, :].astype(I32)
    rank = meta_t[:, 2:4, :].astype(I32)
    cnt = counts[0, ROUTE_LANE0:ROUTE_LANE0 + N_EXPERTS].astype(I32)
    padded = ((cnt + tile - 1) // tile) * tile
    ends = jnp.cumsum(padded)
    offs = ends - padded
    base = jnp.zeros_like(eid)
    for e in range(N_EXPERTS):
        base = jnp.where(eid == e, offs[e], base)
    pos = base + rank
    n_rows = 2 * n_tok + N_EXPERTS * tile
    nt = n_rows // tile
    tile_start = jnp.arange(nt, dtype=I32) * tile
    tile_expert = jnp.minimum(jnp.sum((tile_start[:, None] >= ends[None, :]).astype(I32), axis=1),
                              N_EXPERTS - 1)
    n_used = (ends[-1] // tile).reshape(1)
    return pos, tile_expert, n_used, offs + cnt, padded - cnt, n_rows


def kernel(x, p, g_mix, w_in, gla_w_alpha, gla_b_alpha, rel_bias, mix_scale, w_out, g_ffn,
           w_router_group, b_router_group, w_router_expert, b_router_expert,
           w_expert_gate, w_expert_up, w_expert_down, g_ple, w_ple_gate, w_ple_proj, g_final):
    batch, seq, d = x.shape
    depth = g_mix.shape[0]
    t = batch * seq
    assert seq % SUPER == 0 and d == 2 * ROW_SUB * LANES
    tok_tile = min(t, TOK_TILE)
    ple_tile = min(t, PLE_TILE)
    xf = x.reshape(t, d)
    cos2, sin2 = _rotary_tables(seq)
    w_in_r = _prep_w_in(w_in)
    w_out_b = w_out.astype(BF16)
    wa_all = jnp.concatenate(
        [gla_w_alpha, jnp.zeros((depth, LANES - GLA_RANK, GLA_QK), gla_w_alpha.dtype)], axis=1).astype(BF16)
    wr_all = _pad_lanes([w_router_group, w_router_expert.reshape(depth, d, N_EXPERTS)], LANES, BF16)
    br_all = _pad_lanes([b_router_group, b_router_expert.reshape(depth, N_EXPERTS)], LANES, F32)
    br_all = br_all.reshape(depth, 1, LANES)
    p_all = p.reshape(depth, t, p.shape[-1])
    w_pg_b = w_ple_gate.astype(BF16)
    w_pp_b = w_ple_proj.astype(BF16)
    bias_all = _att_bias(rel_bias)
    h = _norm_cast(xf, g_mix[0])
    for i in range(depth):
        z = _in_proj(h, w_in_r, i)
        ms = mix_scale[i]
        o_ret, o_gla, o_att = _mixers(z, cos2, sin2, wa_all, gla_b_alpha[i], bias_all, ms, i, batch, seq)
        x1, hp, meta, meta_t, counts = _out_route(xf, o_ret, o_gla, o_att, w_out_b, g_ffn[i], wr_all, br_all, i)
        pos, tile_expert, n_used, pad_start, pad_n, n_rows = _routing_tables(meta_t, counts, t, MOE_TILE)
        xs = _dispatch(hp, pos.reshape(t // tok_tile, 1, 2 * tok_tile), pad_start, pad_n, n_used, n_rows)
        ys = _moe(xs, tile_expert, n_used, w_expert_gate, w_expert_up, w_expert_down, i)
        r = tok_tile // ple_tile
        pos_ple = pos.reshape(t // tok_tile, 2, r, ple_tile).transpose(0, 2, 1, 3).reshape(t // ple_tile, 1, 2 * ple_tile)
        final = i == depth - 1
        outs = _combine_ple(x1, meta, pos_ple, ys, p_all, g_ple[i], w_pg_b, w_pp_b, i,
                            g_final if final else g_mix[i + 1], final=final)
        xf = outs[0]
        if not final:
            h = outs[1]
    return xf.reshape(batch, seq, d)
```
